```python
import math
import jax, jax.numpy as jnp
from jax import lax
import numpy as np

D_MODEL = 1024
BATCH = 8
SEQ = 8192
DEPTH = 2

GRID_W = 64
CTX_LEN = 256
N_MIXERS = 2
N_HEADS = 16
N_KV_HEADS = 4
HEAD_DIM = D_MODEL // N_HEADS
GROUP = N_HEADS // N_KV_HEADS
WINDOW = 128
BLOCK = 128
ROPE_THETA = 10000.0
ROPE_HALF = HEAD_DIM // 2
AXIS_FREQS = ROPE_HALF // 2
ATTN_SCALE = HEAD_DIM ** -0.5
D_FF = ((-(-8 * D_MODEL // 3) + 255) // 256) * 256
HY_EMB = 33
HY_BANDS = (HY_EMB - 1) // 2
HY_HIDDEN = 64
HY_SHORT = 3
HY_DECAY_TARGET = 1e-2
HY_FAST_PCT = 0.3
HY_SLOW_PCT = 1.5
EPS = 1e-6
NEG = -1e30

kernel_name = "hybrid_swa_hyena_dit_block"


def rms_norm(x, g):
    xf = x.astype(jnp.float32)
    y = xf * lax.rsqrt(jnp.mean(xf * xf, axis=-1, keepdims=True) + EPS)
    return (y * g.astype(jnp.float32)).astype(x.dtype)


def modulate(h, shift, scale):
    return h * (1 + scale) + shift


def ada_chunks(cond, w, b):
    m = jax.nn.silu(cond) @ w + b
    return jnp.split(m[:, None, :], 6, axis=-1)


def axial_rope_tables(L):
    rows = L // GRID_W
    row = jnp.repeat(jnp.arange(rows, dtype=jnp.float32), GRID_W)
    col = jnp.tile(jnp.arange(GRID_W, dtype=jnp.float32), rows)
    inv_freq = ROPE_THETA ** (-jnp.arange(AXIS_FREQS, dtype=jnp.float32) / AXIS_FREQS)
    ang = jnp.concatenate([row[:, None] * inv_freq, col[:, None] * inv_freq], axis=-1)
    return jnp.cos(ang), jnp.sin(ang)


def apply_rope(x, cos, sin):
    c = cos[None, :, None, :].astype(x.dtype)
    s = sin[None, :, None, :].astype(x.dtype)
    x1, x2 = x[..., :ROPE_HALF], x[..., ROPE_HALF:]
    return jnp.concatenate([x1 * c - x2 * s, x1 * s + x2 * c], axis=-1)


def q_proj(h, wqkv, q_gain):
    B, L, _ = h.shape
    q = (h @ wqkv[:, :N_HEADS * HEAD_DIM]).reshape(B, L, N_HEADS, HEAD_DIM)
    return rms_norm(q, q_gain)


def kv_proj(h, wqkv, k_gain):
    B, L, _ = h.shape
    kv = h @ wqkv[:, N_HEADS * HEAD_DIM:]
    k = kv[..., :N_KV_HEADS * HEAD_DIM].reshape(B, L, N_KV_HEADS, HEAD_DIM)
    v = kv[..., N_KV_HEADS * HEAD_DIM:].reshape(B, L, N_KV_HEADS, HEAD_DIM)
    return rms_norm(k, k_gain), v


def sink_softmax(scores, values, sink):
    sink = sink.astype(jnp.float32).reshape(N_KV_HEADS, GROUP)[None, :, :, None, None]
    m = sink
    for s in scores:
        m = jnp.maximum(m, jnp.max(s, axis=-1, keepdims=True))
    denom = jnp.exp(sink - m)
    out = 0.0
    for s, v in zip(scores, values):
        p = jnp.exp(s - m)
        denom = denom + jnp.sum(p, axis=-1, keepdims=True)
        out = out + jnp.einsum('bkgts,bskd->bkgtd', p.astype(v.dtype), v).astype(jnp.float32)
    out = out / denom
    return out.transpose(0, 3, 1, 2, 4).astype(values[0].dtype)


def windowed_attention(q, k, v, kc, vc, sink):
    B, L, _, _ = q.shape
    nb = L // BLOCK
    qg = q.reshape(B, nb, BLOCK, N_KV_HEADS, GROUP, HEAD_DIM).transpose(1, 0, 2, 3, 4, 5)

    def band(t):
        tp = jnp.pad(t, ((0, 0), (BLOCK, BLOCK), (0, 0), (0, 0)))
        tp = tp.reshape(B, nb + 2, BLOCK, N_KV_HEADS, HEAD_DIM)
        tb = jnp.concatenate([tp[:, :nb], tp[:, 1:nb + 1], tp[:, 2:nb + 2]], axis=2)
        return tb.transpose(1, 0, 2, 3, 4)

    kb_all, vb_all = band(k), band(v)

    def block_fn(args):
        qb, kb, vb, b = args
        s_loc = jnp.einsum('btkgd,bskd->bkgts', qb, kb).astype(jnp.float32) * ATTN_SCALE
        qpos = b * BLOCK + jnp.arange(BLOCK)
        kpos = (b - 1) * BLOCK + jnp.arange(3 * BLOCK)
        valid = ((jnp.abs(qpos[:, None] - kpos[None, :]) <= WINDOW)
                 & (kpos[None, :] >= 0) & (kpos[None, :] < L))
        s_loc = jnp.where(valid, s_loc, NEG)
        s_ctx = jnp.einsum('btkgd,bckd->bkgtc', qb, kc).astype(jnp.float32) * ATTN_SCALE
        return sink_softmax([s_loc, s_ctx], [vb, vc], sink)

    o = lax.map(block_fn, (qg, kb_all, vb_all, jnp.arange(nb)))
    return o.transpose(1, 0, 2, 3, 4, 5).reshape(B, L, N_HEADS * HEAD_DIM)


def context_attention(qc, kc, vc, sink):
    B, Lc, _, _ = qc.shape
    qg = qc.reshape(B, Lc, N_KV_HEADS, GROUP, HEAD_DIM)
    s = jnp.einsum('bqkgd,bckd->bkgqc', qg, kc).astype(jnp.float32) * ATTN_SCALE
    return sink_softmax([s], [vc], sink).reshape(B, Lc, N_HEADS * HEAD_DIM)


def hyena_filter_fft(L, w1, b1, freq1, w2, b2, freq2, w_out, decay):
    f32 = jnp.float32
    t = jnp.linspace(0.0, 1.0, L, dtype=f32)[:, None]
    w = 2.0 * math.pi * jnp.arange(L, dtype=f32)[:, None] / L
    bands = jnp.linspace(1e-4, HY_BANDS - 1, HY_BANDS, dtype=f32)[None, :]
    z = jnp.concatenate([t, jnp.cos(bands * w), -jnp.sin(bands * w)], axis=-1)
    hdn = jnp.sin(freq1.astype(f32) * (z @ w1.astype(f32) + b1.astype(f32)))
    hdn = jnp.sin(freq2.astype(f32) * (hdn @ w2.astype(f32) + b2.astype(f32)))
    hh = hdn @ w_out.astype(f32)
    window = jnp.exp(-t * jnp.abs(decay.astype(f32))[None, :])
    h_fwd = hh[:, :D_MODEL] * window
    h_bwd = hh[:, D_MODEL:] * window
    k = jnp.concatenate([h_fwd, jnp.zeros((1, D_MODEL), f32), h_bwd[1:][::-1]], axis=0)
    k = k / jnp.sum(jnp.abs(k), axis=0, keepdims=True)
    return jnp.fft.rfft(k, n=2 * L, axis=0)


def hyena_mix(h, K, w_in, b_in, conv_w, conv_b, skip, w_out, b_out):
    B, L, _ = h.shape
    z = h @ w_in + b_in
    zp = jnp.pad(z, ((0, 0), (1, 1), (0, 0)))
    z = zp[:, :-2] * conv_w[0] + zp[:, 1:-1] * conv_w[1] + zp[:, 2:] * conv_w[2] + conv_b
    x0, x1, v = jnp.split(z, 3, axis=-1)
    u = (v * x1).astype(jnp.float32)
    U = jnp.fft.rfft(u, n=2 * L, axis=1)
    y = jnp.fft.irfft(U * K[None], n=2 * L, axis=1)[:, :L]
    y = (y + u * skip.astype(jnp.float32)).astype(h.dtype) * x0
    return y @ w_out + b_out


def swiglu(h, w1, w3, w2):
    return (jax.nn.silu(h @ w1) * (h @ w3)) @ w2


def setup_inputs(seed: int = 0) -> dict:
    key = jax.random.key(seed)
    ks = iter(jax.random.split(key, 48))
    f32 = jnp.float32

    def nrm(shape, scale):
        return scale * jax.random.normal(next(ks), shape, f32)

    D = D_MODEL
    NA = len(range(0, DEPTH, N_MIXERS))
    NB = len(range(1, DEPTH, N_MIXERS))
    qkv_w = (N_HEADS + 2 * N_KV_HEADS) * HEAD_DIM
    fast = -math.log(HY_DECAY_TARGET) / HY_FAST_PCT
    slow = -math.log(HY_DECAY_TARGET) / HY_SLOW_PCT
    decay_base = jnp.linspace(fast, slow, D, dtype=f32)[None, :]
    return {
        "x": nrm((BATCH, SEQ, D), 1.0),
        "c": nrm((BATCH, D), 1.0),
        "ctx": nrm((BATCH, CTX_LEN, D), 1.0),
        "c_ctx": nrm((D,), 1.0),
        "ada_w": nrm((DEPTH, D, 6 * D), 0.5 * D ** -0.5),
        "ada_b": nrm((DEPTH, 6 * D), 0.02),
        "norm1_g": 1.0 + nrm((DEPTH, D), 0.05),
        "norm2_g": 1.0 + nrm((DEPTH, D), 0.05),
        "attn_wqkv": nrm((NA, D, qkv_w), D ** -0.5),
        "attn_wo": nrm((NA, N_HEADS * HEAD_DIM, D), (N_HEADS * HEAD_DIM) ** -0.5),
        "attn_q_gain": 1.0 + nrm((NA, HEAD_DIM), 0.05),
        "attn_k_gain": 1.0 + nrm((NA, HEAD_DIM), 0.05),
        "attn_sink": nrm((NA, N_HEADS), 0.5),
        "hy_w_in": nrm((NB, D, 3 * D), D ** -0.5),
        "hy_b_in": nrm((NB, 3 * D), 0.02),
        "hy_conv_w": nrm((NB, HY_SHORT, 3 * D), HY_SHORT ** -0.5),
        "hy_conv_b": nrm((NB, 3 * D), 0.02),
        "hy_f_w1": nrm((NB, HY_EMB, HY_HIDDEN), HY_EMB ** -0.5),
        "hy_f_b1": nrm((NB, HY_HIDDEN), 0.5),
        "hy_f_freq1": 1.0 + nrm((NB, HY_HIDDEN), 0.1),
        "hy_f_w2": nrm((NB, HY_HIDDEN, HY_HIDDEN), HY_HIDDEN ** -0.5),
        "hy_f_b2": nrm((NB, HY_HIDDEN), 0.5),
        "hy_f_freq2": 1.0 + nrm((NB, HY_HIDDEN), 0.1),
        "hy_f_wout": nrm((NB, HY_HIDDEN, 2 * D), HY_HIDDEN ** -0.5),
        "hy_decay": decay_base * (1.0 + nrm((NB, D), 0.05)),
        "hy_skip": nrm((NB, D), 1.0),
        "hy_w_out": nrm((NB, D, D), D ** -0.5),
        "hy_b_out": nrm((NB, D), 0.02),
        "ffn_w1": nrm((DEPTH, D, D_FF), D ** -0.5),
        "ffn_w3": nrm((DEPTH, D, D_FF), D ** -0.5),
        "ffn_w2": nrm((DEPTH, D_FF, D), D_FF ** -0.5),
    }


def reference(x, c, ctx, c_ctx, ada_w, ada_b, norm1_g, norm2_g,
              attn_wqkv, attn_wo, attn_q_gain, attn_k_gain, attn_sink,
              hy_w_in, hy_b_in, hy_conv_w, hy_conv_b,
              hy_f_w1, hy_f_b1, hy_f_freq1, hy_f_w2, hy_f_b2, hy_f_freq2, hy_f_wout,
              hy_decay, hy_skip, hy_w_out, hy_b_out,
              ffn_w1, ffn_w3, ffn_w2):
    L = x.shape[1]
    Lc = ctx.shape[1]
    cos, sin = axial_rope_tables(L)
    last_ctx_reader = ((DEPTH - 1) // N_MIXERS) * N_MIXERS
    xc = ctx
    for i in range(DEPTH):
        upd = i < last_ctx_reader
        sh1, sc1, g1, sh2, sc2, g2 = ada_chunks(c, ada_w[i], ada_b[i])
        csh1, csc1, cg1, csh2, csc2, cg2 = ada_chunks(c_ctx[None, :], ada_w[i], ada_b[i])
        h = modulate(rms_norm(x, norm1_g[i]), sh1, sc1)
        if i % N_MIXERS == 0:
            a = i // N_MIXERS
            hc = modulate(rms_norm(xc, norm1_g[i]), csh1, csc1)
            kc, vc = kv_proj(hc, attn_wqkv[a], attn_k_gain[a])
            q = apply_rope(q_proj(h, attn_wqkv[a], attn_q_gain[a]), cos, sin)
            k, v = kv_proj(h, attn_wqkv[a], attn_k_gain[a])
            k = apply_rope(k, cos, sin)
            o = windowed_attention(q, k, v, kc, vc, attn_sink[a])
            x = x + g1 * (o @ attn_wo[a])
            if upd:
                qc = q_proj(hc, attn_wqkv[a], attn_q_gain[a])
                oc = context_attention(qc, kc, vc, attn_sink[a])
                xc = xc + cg1 * (oc @ attn_wo[a])
        else:
            j = i // N_MIXERS
            K = hyena_filter_fft(L, hy_f_w1[j], hy_f_b1[j], hy_f_freq1[j], hy_f_w2[j],
                                 hy_f_b2[j], hy_f_freq2[j], hy_f_wout[j], hy_decay[j])
            x = x + g1 * hyena_mix(h, K, hy_w_in[j], hy_b_in[j], hy_conv_w[j], hy_conv_b[j],
                                   hy_skip[j], hy_w_out[j], hy_b_out[j])
            if upd:
                hc = modulate(rms_norm(xc, norm1_g[i]), csh1, csc1)
                Kc = hyena_filter_fft(Lc, hy_f_w1[j], hy_f_b1[j], hy_f_freq1[j], hy_f_w2[j],
                                      hy_f_b2[j], hy_f_freq2[j], hy_f_wout[j], hy_decay[j])
                xc = xc + cg1 * hyena_mix(hc, Kc, hy_w_in[j], hy_b_in[j], hy_conv_w[j],
                                          hy_conv_b[j], hy_skip[j], hy_w_out[j], hy_b_out[j])
        x = x + g2 * swiglu(modulate(rms_norm(x, norm2_g[i]), sh2, sc2),
                            ffn_w1[i], ffn_w3[i], ffn_w2[i])
        if upd:
            xc = xc + cg2 * swiglu(modulate(rms_norm(xc, norm2_g[i]), csh2, csc2),
                                   ffn_w1[i], ffn_w3[i], ffn_w2[i])
    return x
```

```python
import functools
import math

import numpy as np
import jax
import jax.numpy as jnp
from jax import lax
from jax.experimental import pallas as pl
from jax.experimental.pallas import tpu as pltpu

N_HEADS = 16
N_KV_HEADS = 4
HEAD_DIM = 64
GROUP = N_HEADS // N_KV_HEADS
ROPE_HALF = HEAD_DIM // 2
AXIS_FREQS = ROPE_HALF // 2
BLOCK = 128
GRID_W = 64
ROPE_THETA = 10000.0
ATTN_SCALE = HEAD_DIM ** -0.5
HY_BANDS = 16
EPS = 1e-6
NEG = -1e30
LANES = 128
VMEM_LIMIT = 56 * 1024 * 1024

F32 = jnp.float32
BF16 = jnp.bfloat16
HI = lax.Precision.HIGHEST
NT = (((1,), (1,)), ((), ()))
TN = (((0,), (0,)), ((), ()))


def _params(*sem):
    return pltpu.CompilerParams(dimension_semantics=sem, vmem_limit_bytes=VMEM_LIMIT)


def _const_spec(shape):
    n = len(shape)
    return pl.BlockSpec(shape, lambda *_: (0,) * n, pipeline_mode=pl.Buffered(1))


def _norm_mod(x, g, sc, sh):
    y = x * lax.rsqrt(jnp.mean(x * x, axis=-1, keepdims=True) + EPS)
    return (y * g) * (1.0 + sc) + sh


def _ada_kernel(cond_ref, w_ref, b_ref, o_ref):
    cnd = cond_ref[...]
    s = (cnd * jax.nn.sigmoid(cnd)).astype(BF16)
    o_ref[0] = jnp.dot(s, w_ref[0].astype(BF16), preferred_element_type=F32) + b_ref[0]


def _ada(cond, ada_w, ada_b):
    depth, d, d6 = ada_w.shape
    r = cond.shape[0]
    cw = 1536
    return pl.pallas_call(
        _ada_kernel,
        out_shape=jax.ShapeDtypeStruct((depth, r, d6), F32),
        grid=(depth, d6 // cw),
        in_specs=[pl.BlockSpec((r, d), lambda i, j: (0, 0)),
                  pl.BlockSpec((1, d, cw), lambda i, j: (i, 0, j)),
                  pl.BlockSpec((1, 1, cw), lambda i, j: (i, 0, j))],
        out_specs=pl.BlockSpec((1, r, cw), lambda i, j: (i, 0, j)),
        compiler_params=_params("arbitrary", "arbitrary"),
        name="ada_mod",
    )(cond, ada_w, ada_b.reshape(depth, 1, d6))


def _head_norm(t, gain, n_heads):
    t3 = t.reshape(n_heads, HEAD_DIM, t.shape[-1])
    ms = jnp.mean(t3 * t3, axis=1, keepdims=True)
    return (t3 * lax.rsqrt(ms + EPS)) * gain[None]


def _rope(t3, cos, sin):
    x1, x2 = t3[:, :ROPE_HALF], t3[:, ROPE_HALF:]
    c, s = cos[None], sin[None]
    return jnp.concatenate([x1 * c - x2 * s, x1 * s + x2 * c], axis=1)


def _qkv_kernel(*refs, n_q, rope):
    if rope:
        x_ref, sh_ref, sc_ref, g_ref, w_ref, qg_ref, kg_ref, cos_ref, sin_ref = refs[:9]
        outs = refs[9:]
    else:
        x_ref, sh_ref, sc_ref, g_ref, w_ref, qg_ref, kg_ref = refs[:7]
        outs = refs[7:]
    h = _norm_mod(x_ref[0], g_ref[...], sc_ref[0], sh_ref[0]).astype(BF16)
    t = lax.dot_general(w_ref[...], h, NT, preferred_element_type=F32)
    nq = n_q * HEAD_DIM
    nk = N_KV_HEADS * HEAD_DIM
    tt = t.shape[-1]
    k3 = _head_norm(t[nq:nq + nk], kg_ref[...], N_KV_HEADS)
    if rope:
        k3 = _rope(k3, cos_ref[...], sin_ref[...])
    if n_q:
        q_ref, k_ref, v_ref = outs
        q3 = _head_norm(t[:nq], qg_ref[...], n_q)
        if rope:
            q3 = _rope(q3, cos_ref[...], sin_ref[...])
        q_ref[0] = (q3 * ATTN_SCALE).reshape(nq, tt).astype(BF16)
    else:
        k_ref, v_ref = outs
    k_ref[0] = k3.reshape(nk, tt).astype(BF16)
    v_ref[0] = t[nq + nk:].astype(BF16)


def _qkv(x, sh, sc, g, w_t, qg, kg, cos_t, sin_t, *, n_q, tile):
    b, l, d = x.shape
    nq, nk = n_q * HEAD_DIM, N_KV_HEADS * HEAD_DIM
    rope = cos_t is not None
    vec = pl.BlockSpec((1, 1, d), lambda i, j: (i, 0, 0))
    in_specs = [pl.BlockSpec((1, tile, d), lambda i, j: (i, j, 0)), vec, vec,
                _const_spec((1, d)), _const_spec(w_t.shape),
                _const_spec((HEAD_DIM, 1)), _const_spec((HEAD_DIM, 1))]
    args = [x, sh, sc, g, w_t, qg, kg]
    if rope:
        in_specs += [pl.BlockSpec((ROPE_HALF, tile), lambda i, j: (0, j))] * 2
        args += [cos_t, sin_t]
    out_shape, out_specs = [], []
    for rows in ([nq] if n_q else []) + [nk, nk]:
        out_shape.append(jax.ShapeDtypeStruct((b, rows, l), BF16))
        out_specs.append(pl.BlockSpec((1, rows, tile), lambda i, j: (i, 0, j)))
    return pl.pallas_call(
        functools.partial(_qkv_kernel, n_q=n_q, rope=rope),
        out_shape=out_shape, grid=(b, l // tile), in_specs=in_specs, out_specs=out_specs,
        compiler_params=_params("arbitrary", "arbitrary"),
        name="qkv_proj" if n_q else "ctx_kv_proj",
    )(*args)


def _attn_kernel(sink_ref, q_ref, kp_ref, kc_ref, kn_ref, vp_ref, vc_ref, vn_ref, kx_ref, vx_ref, o_ref):
    j = pl.program_id(1)
    nb = pl.num_programs(1)
    wq = GROUP * BLOCK
    key = lax.broadcasted_iota(jnp.int32, (BLOCK, wq), 0)
    qry = lax.broadcasted_iota(jnp.int32, (BLOCK, wq), 1) % BLOCK
    ok_prev = jnp.logical_and(key >= qry, j > 0)
    ok_next = jnp.logical_and(key <= qry, j < nb - 1)
    for kh in range(N_KV_HEADS):
        rows = slice(kh * HEAD_DIM, (kh + 1) * HEAD_DIM)
        kcat = jnp.concatenate([kp_ref[0, rows], kc_ref[0, rows], kn_ref[0, rows], kx_ref[0, rows]], axis=1)
        vcat = jnp.concatenate([vp_ref[0, rows], vc_ref[0, rows], vn_ref[0, rows], vx_ref[0, rows]], axis=1)
        heads = [kh * GROUP + g for g in range(GROUP)]
        q4 = jnp.concatenate([q_ref[0, h * HEAD_DIM:(h + 1) * HEAD_DIM] for h in heads], axis=1)
        s = lax.dot_general(kcat, q4, TN, preferred_element_type=F32)
        s_prev = jnp.where(ok_prev, s[:BLOCK], NEG)
        s_cur = s[BLOCK:2 * BLOCK]
        s_next = jnp.where(ok_next, s[2 * BLOCK:3 * BLOCK], NEG)
        s_ctx = s[3 * BLOCK:]
        sink = jnp.concatenate([jnp.full((1, BLOCK), sink_ref[h], F32) for h in heads], axis=1)
        m = sink
        for part in (s_prev, s_cur, s_next, s_ctx):
            m = jnp.maximum(m, jnp.max(part, axis=0, keepdims=True))
        denom = jnp.exp(sink - m)
        ps = []
        for part in (s_prev, s_cur, s_next, s_ctx):
            p = jnp.exp(part - m)
            denom = denom + jnp.sum(p, axis=0, keepdims=True)
            ps.append(p.astype(BF16))
        o4 = jnp.dot(vcat, jnp.concatenate(ps, axis=0), preferred_element_type=F32) / denom
        for g, h in enumerate(heads):
            o_ref[0, h * HEAD_DIM:(h + 1) * HEAD_DIM] = o4[:, g * BLOCK:(g + 1) * BLOCK].astype(BF16)


def _attention(sink, q_t, k_t, v_t, kx_t, vx_t):
    b, dq, l = q_t.shape
    nk = k_t.shape[1]
    lc = kx_t.shape[2]
    nb = l // BLOCK
    prev = pl.BlockSpec((1, nk, BLOCK), lambda i, j: (i, 0, jnp.maximum(j - 1, 0)))
    cur = pl.BlockSpec((1, nk, BLOCK), lambda i, j: (i, 0, j))
    nxt = pl.BlockSpec((1, nk, BLOCK), lambda i, j: (i, 0, jnp.minimum(j + 1, nb - 1)))
    ctx = pl.BlockSpec((1, nk, lc), lambda i, j: (i, 0, 0))
    return pl.pallas_call(
        _attn_kernel,
        out_shape=jax.ShapeDtypeStruct((b, dq, l), BF16),
        grid=(b, nb),
        in_specs=[pl.BlockSpec(memory_space=pltpu.SMEM),
                  pl.BlockSpec((1, dq, BLOCK), lambda i, j: (i, 0, j)),
                  prev, cur, nxt, prev, cur, nxt, ctx, ctx],
        out_specs=pl.BlockSpec((1, dq, BLOCK), lambda i, j: (i, 0, j)),
        compiler_params=_params("arbitrary", "arbitrary"),
        name="band_attention",
    )(sink, q_t, k_t, k_t, k_t, v_t, v_t, v_t, kx_t, vx_t)


def _proj_ffn_kernel(x_ref, a_ref, g1_ref, sh2_ref, sc2_ref, g2_ref, wo_ref, bo_ref, n2_ref,
                     w1_ref, w3_ref, w2_ref, o_ref, *, ff_chunk):
    proj = lax.dot_general(a_ref[0], wo_ref[...], TN, preferred_element_type=F32)
    x1 = x_ref[0] + g1_ref[0] * (proj + bo_ref[...])
    h = _norm_mod(x1, n2_ref[...], sc2_ref[0], sh2_ref[0]).astype(BF16)
    acc = None
    for c0 in range(0, w1_ref.shape[1], ff_chunk):
        a = jnp.dot(h, w1_ref[:, c0:c0 + ff_chunk], preferred_element_type=F32)
        b = jnp.dot(h, w3_ref[:, c0:c0 + ff_chunk], preferred_element_type=F32)
        act = ((a * jax.nn.sigmoid(a)) * b).astype(BF16)
        part = jnp.dot(act, w2_ref[c0:c0 + ff_chunk, :], preferred_element_type=F32)
        acc = part if acc is None else acc + part
    o_ref[0] = x1 + g2_ref[0] * acc


def _proj_ffn(x, a_t, g1, sh2, sc2, g2, wo, bo, n2, w1, w3, w2, *, tile, ff_chunk):
    b, l, d = x.shape
    c = a_t.shape[1]
    vec = pl.BlockSpec((1, 1, d), lambda i, j: (i, 0, 0))
    return pl.pallas_call(
        functools.partial(_proj_ffn_kernel, ff_chunk=ff_chunk),
        out_shape=jax.ShapeDtypeStruct((b, l, d), F32),
        grid=(b, l // tile),
        in_specs=[pl.BlockSpec((1, tile, d), lambda i, j: (i, j, 0)),
                  pl.BlockSpec((1, c, tile), lambda i, j: (i, 0, j)),
                  vec, vec, vec, vec,
                  _const_spec(wo.shape), _const_spec((1, d)), _const_spec((1, d)),
                  _const_spec(w1.shape), _const_spec(w3.shape), _const_spec(w2.shape)],
        out_specs=pl.BlockSpec((1, tile, d), lambda i, j: (i, j, 0)),
        compiler_params=_params("arbitrary", "arbitrary"),
        name="proj_swiglu",
    )(x, a_t, g1, sh2, sc2, g2, wo, bo, n2, w1, w3, w2)


def _inproj_kernel(x_ref, sh_ref, sc_ref, g_ref, w_ref, b_ref, z_ref):
    h = _norm_mod(x_ref[0], g_ref[...], sc_ref[0], sh_ref[0]).astype(BF16)
    z = lax.dot_general(w_ref[...], h, NT, preferred_element_type=F32) + b_ref[...]
    z_ref[0] = z.astype(BF16)


def _inproj(x, sh, sc, g, w_t, b_col, *, tile):
    b, l, d = x.shape
    c3 = w_t.shape[0]
    vec = pl.BlockSpec((1, 1, d), lambda i, j: (i, 0, 0))
    return pl.pallas_call(
        _inproj_kernel,
        out_shape=jax.ShapeDtypeStruct((b, c3, l), BF16),
        grid=(b, l // tile),
        in_specs=[pl.BlockSpec((1, tile, d), lambda i, j: (i, j, 0)), vec, vec,
                  _const_spec((1, d)), _const_spec(w_t.shape), _const_spec((c3, 1))],
        out_specs=pl.BlockSpec((1, c3, tile), lambda i, j: (i, 0, j)),
        compiler_params=_params("arbitrary", "arbitrary"),
        name="hyena_inproj",
    )(x, sh, sc, g, w_t, b_col)


def _filter_mlp_kernel(z_ref, w1_ref, b1_ref, f1_ref, w2_ref, b2_ref, f2_ref, o_ref):
    h = jnp.sin(f1_ref[...] * (jnp.dot(z_ref[...], w1_ref[...], precision=HI, preferred_element_type=F32)
                               + b1_ref[...]))
    o_ref[...] = jnp.sin(f2_ref[...] * (jnp.dot(h, w2_ref[...], precision=HI, preferred_element_type=F32)
                                        + b2_ref[...]))


def _filter_mlp(zz, w1, b1, f1, w2, b2, f2):
    n, e = zz.shape
    hid = w1.shape[1]
    rows = min(n, 2048)
    row = lambda a: a.reshape(1, -1)
    return pl.pallas_call(
        _filter_mlp_kernel,
        out_shape=jax.ShapeDtypeStruct((n, hid), F32),
        grid=(n // rows,),
        in_specs=[pl.BlockSpec((rows, e), lambda i: (i, 0)),
                  _const_spec(w1.shape), _const_spec((1, hid)), _const_spec((1, hid)),
                  _const_spec(w2.shape), _const_spec((1, hid)), _const_spec((1, hid))],
        out_specs=pl.BlockSpec((rows, hid), lambda i: (i, 0)),
        compiler_params=_params("arbitrary"),
        name="hyena_filter_mlp",
    )(zz, w1, row(b1), row(f1), w2, row(b2), row(f2))


def _filter_taps_kernel(hdn_ref, wf_ref, wb_ref, t_ref, dec_ref, k_ref):
    l = hdn_ref.shape[0] // 2
    hf = lax.dot_general(wf_ref[...], hdn_ref[:l], NT, precision=HI, preferred_element_type=F32)
    hb = lax.dot_general(wb_ref[...], hdn_ref[l:], NT, precision=HI, preferred_element_type=F32)
    win = jnp.exp(-t_ref[...] * jnp.abs(dec_ref[...]))
    lane = lax.broadcasted_iota(jnp.int32, win.shape, 1)
    k = jnp.where(lane == l, 0.0, jnp.concatenate([hf, hb], axis=1) * win)
    k_ref[...] = k / jnp.sum(jnp.abs(k), axis=1, keepdims=True)


def _filter_taps(hdn, wf_t, wb_t, t_row, dec_col, *, cb):
    n2, hid = hdn.shape
    c = wf_t.shape[0]
    return pl.pallas_call(
        _filter_taps_kernel,
        out_shape=jax.ShapeDtypeStruct((c, n2), F32),
        grid=(c // cb,),
        in_specs=[_const_spec((n2, hid)),
                  pl.BlockSpec((cb, hid), lambda i: (i, 0)), pl.BlockSpec((cb, hid), lambda i: (i, 0)),
                  _const_spec((1, n2)), pl.BlockSpec((cb, 1), lambda i: (i, 0))],
        out_specs=pl.BlockSpec((cb, n2), lambda i: (i, 0)),
        compiler_params=_params("arbitrary"),
        name="hyena_filter_taps",
    )(hdn, wf_t, wb_t, t_row, dec_col)


def _split(x):
    hi = x.astype(BF16)
    return hi, (x - hi.astype(F32)).astype(BF16)


def _dot3(a_hi, a_lo, b):
    b_hi, b_lo = _split(b)
    d = functools.partial(jnp.dot, preferred_element_type=F32)
    return d(a_hi, b_hi) + (d(a_hi, b_lo) + d(a_lo, b_hi))


def _dot3r(a, b_hi, b_lo):
    a_hi, a_lo = _split(a)
    d = functools.partial(jnp.dot, preferred_element_type=F32)
    return d(a_hi, b_hi) + (d(a_hi, b_lo) + d(a_lo, b_hi))


def _filter_fft_kernel(k_ref, f1h_ref, f1l_ref, twr_ref, twi_ref, gh_ref, gl_ref, o_ref, *, cb, n1):
    for c in range(cb):
        kc = k_ref[c * n1:(c + 1) * n1, :]
        ri = _dot3(f1h_ref[...], f1l_ref[...], kc)
        re, im = ri[:n1], ri[n1:]
        tr, ti = twr_ref[...], twi_ref[...]
        a = jnp.concatenate([re * tr - im * ti, re * ti + im * tr], axis=1)
        o_ref[c] = _dot3r(a, gh_ref[...], gl_ref[...])


def _filter_fft(k_rows, consts, *, c, cb, n1):
    f1h, f1l, twr, twi, gh, gl = consts
    return pl.pallas_call(
        functools.partial(_filter_fft_kernel, cb=cb, n1=n1),
        out_shape=jax.ShapeDtypeStruct((c, n1, 2 * LANES), F32),
        grid=(c // cb,),
        in_specs=[pl.BlockSpec((cb * n1, LANES), lambda i: (i, 0)),
                  _const_spec(f1h.shape), _const_spec(f1l.shape), _const_spec(twr.shape), _const_spec(twi.shape),
                  _const_spec(gh.shape), _const_spec(gl.shape)],
        out_specs=pl.BlockSpec((cb, n1, 2 * LANES), lambda i: (i, 0, 0)),
        compiler_params=_params("arbitrary"),
        name="hyena_filter_fft",
    )(k_rows, f1h, f1l, twr, twi, gh, gl)


def _shift_prev(z, lane, row):
    r = pltpu.roll(z, 1, axis=1)
    rr = pltpu.roll(r, 1, axis=0)
    return jnp.where(lane == 0, jnp.where(row == 0, 0.0, rr), r)


def _shift_next(z, lane, row):
    nh = z.shape[0]
    r = pltpu.roll(z, LANES - 1, axis=1)
    rr = pltpu.roll(r, nh - 1, axis=0)
    return jnp.where(lane == LANES - 1, jnp.where(row == nh - 1, 0.0, rr), r)


def _longconv_kernel(cw_ref, cbias_ref, skip_ref, x0_ref, x1_ref, v_ref, kh_ref,
                     a1_ref, twr_ref, twi_ref, g_ref, gc_ref, p_ref, o_ref, *, cb, nh, n1, c_total, group):
    c_base = pl.program_id(0) * cb
    lane = lax.broadcasted_iota(jnp.int32, (nh, LANES), 1)
    row = lax.broadcasted_iota(jnp.int32, (nh, LANES), 0)
    tr, ti = twr_ref[...], twi_ref[...]

    def conv3(ref, bi, r0, ch):
        z = ref[bi, pl.ds(r0, nh), :].astype(F32)
        return (cw_ref[0, ch] * _shift_prev(z, lane, row) + cw_ref[1, ch] * z
                + cw_ref[2, ch] * _shift_next(z, lane, row) + cbias_ref[ch])

    def body(gi, carry):
        us, x0s, st1 = [], [], []
        for k in range(group):
            ci = gi * group + k
            r0 = pl.multiple_of(ci * nh, nh)
            ch = c_base + ci
            u2, x02 = [], []
            for bi in range(2):
                x0 = conv3(x0_ref, bi, r0, ch)
                x1 = conv3(x1_ref, bi, r0, ch + c_total)
                v = conv3(v_ref, bi, r0, ch + 2 * c_total)
                u2.append(v * x1)
                x02.append(x0)
            u = jnp.concatenate(u2, axis=0)
            us.append(u)
            x0s.append(jnp.concatenate(x02, axis=0))
            ri = jnp.dot(a1_ref[...], u.astype(BF16), preferred_element_type=F32)
            re, im = ri[:n1], ri[n1:]
            st1.append(jnp.concatenate([re * tr - im * ti, re * ti + im * tr], axis=1))
        spec = jnp.dot(jnp.concatenate(st1, axis=0).astype(BF16), g_ref[...], preferred_element_type=F32)
        prods = []
        for k in range(group):
            xs = spec[k * n1:(k + 1) * n1]
            xr, xi = xs[:, :LANES], xs[:, LANES:]
            kk = kh_ref[gi * group + k]
            kr, ki = kk[:, :LANES], kk[:, LANES:]
            prods.append(jnp.concatenate([xr * kr - xi * ki, xr * ki + xi * kr], axis=1))
        back = jnp.dot(jnp.concatenate(prods, axis=0).astype(BF16), gc_ref[...], preferred_element_type=F32)
        for k in range(group):
            bs = back[k * n1:(k + 1) * n1]
            br, bi_ = bs[:, :LANES], bs[:, LANES:]
            st = jnp.concatenate([br * tr + bi_ * ti, bi_ * tr - br * ti], axis=0)
            y = jnp.dot(p_ref[...], st.astype(BF16), preferred_element_type=F32)
            ci = gi * group + k
            r0 = pl.multiple_of(ci * nh, nh)
            gated = (y + us[k] * skip_ref[c_base + ci]) * x0s[k]
            o_ref[0, pl.ds(r0, nh), :] = gated[:nh].astype(BF16)
            o_ref[1, pl.ds(r0, nh), :] = gated[nh:].astype(BF16)
        return carry

    lax.fori_loop(0, cb // group, body, 0)


def _longconv(z_rows, khat, conv_w, conv_b, skip, consts, *, c, cb, nh, n1, group):
    b = z_rows.shape[0]
    a1, twr, twi, g, gc, p = consts
    ncb = c // cb
    smem = pl.BlockSpec(memory_space=pltpu.SMEM)

    def zspec(part):
        return pl.BlockSpec((2, cb * nh, LANES), lambda i, j, part=part: (j, part * ncb + i, 0))

    return pl.pallas_call(
        functools.partial(_longconv_kernel, cb=cb, nh=nh, n1=n1, c_total=c, group=group),
        out_shape=jax.ShapeDtypeStruct((b, c * nh, LANES), BF16),
        grid=(ncb, b // 2),
        in_specs=[smem, smem, smem, zspec(0), zspec(1), zspec(2),
                  pl.BlockSpec((cb, n1, 2 * LANES), lambda i, j: (i, 0, 0)),
                  _const_spec(a1.shape), _const_spec(twr.shape), _const_spec(twi.shape),
                  _const_spec(g.shape), _const_spec(gc.shape), _const_spec(p.shape)],
        out_specs=pl.BlockSpec((2, cb * nh, LANES), lambda i, j: (j, i, 0)),
        compiler_params=_params("arbitrary", "arbitrary"),
        name="hyena_longconv",
    )(conv_w, conv_b, skip, z_rows, z_rows, z_rows, khat, a1, twr, twi, g, gc, p)


def _dft_tables(l):
    n = 2 * l
    n1 = n // LANES
    nh = l // LANES
    f1 = np.exp(-2j * np.pi * np.outer(np.arange(n1), np.arange(n1)) / n1)
    f2 = np.exp(-2j * np.pi * np.outer(np.arange(LANES), np.arange(LANES)) / LANES)
    tw = np.exp(-2j * np.pi * np.outer(np.arange(n1), np.arange(LANES)) / n)
    fh = f1[:, :nh]
    a1 = np.block([[fh.real, -fh.imag], [fh.imag, fh.real]])
    g = np.block([[f2.real, f2.imag], [-f2.imag, f2.real]])
    gc = np.block([[f2.real, -f2.imag], [f2.imag, f2.real]])
    ci = np.conj(f1)[:nh, :]
    p = np.block([[ci.real, -ci.imag], [ci.imag, ci.real]]) / n
    f1full = np.concatenate([f1.real, f1.imag], axis=0)

    def bf(a):
        return jnp.asarray(a, F32).astype(BF16)

    def hilo(a):
        a32 = jnp.asarray(a, F32)
        hi = a32.astype(BF16)
        return hi, (a32 - hi.astype(F32)).astype(BF16)

    data = (bf(a1), jnp.asarray(tw.real, F32), jnp.asarray(tw.imag, F32), bf(g), bf(gc), bf(p))
    filt = hilo(f1full) + (jnp.asarray(tw.real, F32), jnp.asarray(tw.imag, F32)) + hilo(g)
    return data, filt, n1, nh


def _rope_tables_t(l):
    rows = l // GRID_W
    row = jnp.repeat(jnp.arange(rows, dtype=F32), GRID_W)
    col = jnp.tile(jnp.arange(GRID_W, dtype=F32), rows)
    inv_freq = ROPE_THETA ** (-jnp.arange(AXIS_FREQS, dtype=F32) / AXIS_FREQS)
    ang = jnp.concatenate([inv_freq[:, None] * row[None, :], inv_freq[:, None] * col[None, :]], axis=0)
    return jnp.cos(ang), jnp.sin(ang)


def _filter_positions(l):
    t = jnp.linspace(0.0, 1.0, l, dtype=F32)[:, None]
    w = 2.0 * math.pi * jnp.arange(l, dtype=F32)[:, None] / l
    bands = jnp.linspace(1e-4, HY_BANDS - 1, HY_BANDS, dtype=F32)[None, :]
    z = jnp.concatenate([t, jnp.cos(bands * w), -jnp.sin(bands * w)], axis=-1)
    idx = jnp.concatenate([jnp.arange(l), jnp.zeros((1,), jnp.int32), jnp.arange(l - 1, 0, -1)])
    return z[idx], t[idx, 0][None, :]


def kernel(x, c, ctx, c_ctx, ada_w, ada_b, norm1_g, norm2_g, attn_wqkv, attn_wo, attn_q_gain, attn_k_gain, attn_sink, hy_w_in, hy_b_in, hy_conv_w, hy_conv_b, hy_f_w1, hy_f_b1, hy_f_freq1, hy_f_w2, hy_f_b2, hy_f_freq2, hy_f_wout, hy_decay, hy_skip, hy_w_out, hy_b_out, ffn_w1, ffn_w3, ffn_w2):
    b, l, d = x.shape
    tile = min(512, l)
    ff = ffn_w1.shape[-1]
    ff_chunk = ff // 2 if (ff // 2) % LANES == 0 else ff

    pad = (-(b + 1)) % 8
    cond = jnp.concatenate([c, c_ctx[None, :], jnp.zeros((pad, d), F32)], axis=0)
    mod = _ada(cond, ada_w, ada_b)

    def chunks(i, rows):
        m = mod[i, rows][:, None, :]
        return [m[..., k * d:(k + 1) * d] for k in range(6)]

    row = lambda a: a.reshape(1, -1)
    col = lambda a: a.reshape(-1, 1)

    sh1, sc1, g1, sh2, sc2, g2 = chunks(0, slice(0, b))
    csh1, csc1 = [jnp.broadcast_to(m, (b, 1, d)) for m in chunks(0, slice(b, b + 1))[:2]]
    wqkv_t = attn_wqkv[0].T.astype(BF16)
    cos_t, sin_t = _rope_tables_t(l)
    qg, kg = col(attn_q_gain[0]), col(attn_k_gain[0])
    q_t, k_t, v_t = _qkv(x, sh1, sc1, row(norm1_g[0]), wqkv_t, qg, kg, cos_t, sin_t, n_q=N_HEADS, tile=tile)
    kx_t, vx_t = _qkv(ctx, csh1, csc1, row(norm1_g[0]), wqkv_t[N_HEADS * HEAD_DIM:], qg, kg, None, None,
                      n_q=0, tile=ctx.shape[1])
    o_t = _attention(attn_sink[0], q_t, k_t, v_t, kx_t, vx_t)
    x = _proj_ffn(x, o_t, g1, sh2, sc2, g2, attn_wo[0].astype(BF16), jnp.zeros((1, d), F32), row(norm2_g[0]),
                  ffn_w1[0].astype(BF16), ffn_w3[0].astype(BF16), ffn_w2[0].astype(BF16),
                  tile=tile, ff_chunk=ff_chunk)

    sh1, sc1, g1, sh2, sc2, g2 = chunks(1, slice(0, b))
    data_consts, filt_consts, n1, nh = _dft_tables(l)
    zz, t_row = _filter_positions(l)
    hdn = _filter_mlp(zz, hy_f_w1[0], hy_f_b1[0], hy_f_freq1[0], hy_f_w2[0], hy_f_b2[0], hy_f_freq2[0])
    taps = _filter_taps(hdn, hy_f_wout[0][:, :d].T, hy_f_wout[0][:, d:].T, t_row, col(hy_decay[0]), cb=8)
    khat = _filter_fft(taps.reshape(d * n1, LANES), filt_consts, c=d, cb=8, n1=n1)
    z_t = _inproj(x, sh1, sc1, row(norm1_g[1]), hy_w_in[0].T.astype(BF16), col(hy_b_in[0]), tile=tile)
    gated = _longconv(z_t.reshape(b, 3 * d * nh, LANES), khat, hy_conv_w[0], hy_conv_b[0], hy_skip[0],
                      data_consts, c=d, cb=32, nh=nh, n1=n1, group=2)
    x = _proj_ffn(x, gated.reshape(b, d, l), g1, sh2, sc2, g2, hy_w_out[0].astype(BF16), row(hy_b_out[0]),
                  row(norm2_g[1]), ffn_w1[1].astype(BF16), ffn_w3[1].astype(BF16), ffn_w2[1].astype(BF16),
                  tile=tile, ff_chunk=ff_chunk)
    return x
```

```python
import functools
import math

import numpy as np
import jax
import jax.numpy as jnp
from jax import lax
from jax.experimental import pallas as pl
from jax.experimental.pallas import tpu as pltpu

N_HEADS = 16
N_KV_HEADS = 4
HEAD_DIM = 64
GROUP = N_HEADS // N_KV_HEADS
ROPE_HALF = HEAD_DIM // 2
AXIS_FREQS = ROPE_HALF // 2
BLOCK = 128
GRID_W = 64
ROPE_THETA = 10000.0
ATTN_SCALE = HEAD_DIM ** -0.5
HY_BANDS = 16
EPS = 1e-6
NEG = -1e30
LANES = 128
VMEM_LIMIT = 56 * 1024 * 1024

F32 = jnp.float32
BF16 = jnp.bfloat16
HI = lax.Precision.HIGHEST
NT = (((1,), (1,)), ((), ()))
TN = (((0,), (0,)), ((), ()))


def _params(*sem):
    return pltpu.CompilerParams(dimension_semantics=sem, vmem_limit_bytes=VMEM_LIMIT)


def _const_spec(shape):
    n = len(shape)
    return pl.BlockSpec(shape, lambda *_: (0,) * n, pipeline_mode=pl.Buffered(1))


def _norm_mod(x, g, sc, sh):
    y = x * lax.rsqrt(jnp.mean(x * x, axis=-1, keepdims=True) + EPS)
    return (y * g) * (1.0 + sc) + sh


def _ada_kernel(cond_ref, w_ref, b_ref, o_ref):
    cnd = cond_ref[...]
    s = (cnd * jax.nn.sigmoid(cnd)).astype(BF16)
    o_ref[0] = jnp.dot(s, w_ref[0].astype(BF16), preferred_element_type=F32) + b_ref[0]


def _ada(cond, ada_w, ada_b):
    depth, d, d6 = ada_w.shape
    r = cond.shape[0]
    cw = 1536
    return pl.pallas_call(
        _ada_kernel,
        out_shape=jax.ShapeDtypeStruct((depth, r, d6), F32),
        grid=(depth, d6 // cw),
        in_specs=[pl.BlockSpec((r, d), lambda i, j: (0, 0)),
                  pl.BlockSpec((1, d, cw), lambda i, j: (i, 0, j)),
                  pl.BlockSpec((1, 1, cw), lambda i, j: (i, 0, j))],
        out_specs=pl.BlockSpec((1, r, cw), lambda i, j: (i, 0, j)),
        compiler_params=_params("arbitrary", "arbitrary"),
        name="ada_mod",
    )(cond, ada_w, ada_b.reshape(depth, 1, d6))


def _head_norm(t, gain, n_heads):
    t3 = t.reshape(n_heads, HEAD_DIM, t.shape[-1])
    ms = jnp.mean(t3 * t3, axis=1, keepdims=True)
    return (t3 * lax.rsqrt(ms + EPS)) * gain[None]


def _rope(t3, cos, sin):
    x1, x2 = t3[:, :ROPE_HALF], t3[:, ROPE_HALF:]
    c, s = cos[None], sin[None]
    return jnp.concatenate([x1 * c - x2 * s, x1 * s + x2 * c], axis=1)


def _qkv_kernel(*refs, n_q, rope):
    if rope:
        x_ref, sh_ref, sc_ref, g_ref, w_ref, qg_ref, kg_ref, cos_ref, sin_ref = refs[:9]
        outs = refs[9:]
    else:
        x_ref, sh_ref, sc_ref, g_ref, w_ref, qg_ref, kg_ref = refs[:7]
        outs = refs[7:]
    h = _norm_mod(x_ref[0], g_ref[...], sc_ref[0], sh_ref[0]).astype(BF16)
    t = lax.dot_general(w_ref[...], h, NT, preferred_element_type=F32)
    nq = n_q * HEAD_DIM
    nk = N_KV_HEADS * HEAD_DIM
    tt = t.shape[-1]
    k3 = _head_norm(t[nq:nq + nk], kg_ref[...], N_KV_HEADS)
    if rope:
        k3 = _rope(k3, cos_ref[...], sin_ref[...])
    if n_q:
        q_ref, k_ref, v_ref = outs
        q3 = _head_norm(t[:nq], qg_ref[...], n_q)
        if rope:
            q3 = _rope(q3, cos_ref[...], sin_ref[...])
        q_ref[0] = (q3 * ATTN_SCALE).reshape(nq, tt).astype(BF16)
    else:
        k_ref, v_ref = outs
    k_ref[0] = k3.reshape(nk, tt).astype(BF16)
    v_ref[0] = t[nq + nk:].astype(BF16)


def _qkv(x, sh, sc, g, w_t, qg, kg, cos_t, sin_t, *, n_q, tile):
    b, l, d = x.shape
    nq, nk = n_q * HEAD_DIM, N_KV_HEADS * HEAD_DIM
    rope = cos_t is not None
    vec = pl.BlockSpec((1, 1, d), lambda i, j: (i, 0, 0))
    in_specs = [pl.BlockSpec((1, tile, d), lambda i, j: (i, j, 0)), vec, vec,
                _const_spec((1, d)), _const_spec(w_t.shape),
                _const_spec((HEAD_DIM, 1)), _const_spec((HEAD_DIM, 1))]
    args = [x, sh, sc, g, w_t, qg, kg]
    if rope:
        in_specs += [pl.BlockSpec((ROPE_HALF, tile), lambda i, j: (0, j))] * 2
        args += [cos_t, sin_t]
    out_shape, out_specs = [], []
    for rows in ([nq] if n_q else []) + [nk, nk]:
        out_shape.append(jax.ShapeDtypeStruct((b, rows, l), BF16))
        out_specs.append(pl.BlockSpec((1, rows, tile), lambda i, j: (i, 0, j)))
    return pl.pallas_call(
        functools.partial(_qkv_kernel, n_q=n_q, rope=rope),
        out_shape=out_shape, grid=(b, l // tile), in_specs=in_specs, out_specs=out_specs,
        compiler_params=_params("arbitrary", "arbitrary"),
        name="qkv_proj" if n_q else "ctx_kv_proj",
    )(*args)


def _attn_kernel(sink_ref, q_ref, kp_ref, kc_ref, kn_ref, vp_ref, vc_ref, vn_ref, kx_ref, vx_ref, o_ref):
    j = pl.program_id(1)
    nb = pl.num_programs(1)
    wq = GROUP * BLOCK
    key = lax.broadcasted_iota(jnp.int32, (BLOCK, wq), 0)
    qry = lax.broadcasted_iota(jnp.int32, (BLOCK, wq), 1) % BLOCK
    ok_prev = jnp.logical_and(key >= qry, j > 0)
    ok_next = jnp.logical_and(key <= qry, j < nb - 1)
    for kh in range(N_KV_HEADS):
        rows = slice(kh * HEAD_DIM, (kh + 1) * HEAD_DIM)
        kcat = jnp.concatenate([kp_ref[0, rows], kc_ref[0, rows], kn_ref[0, rows], kx_ref[0, rows]], axis=1)
        vcat = jnp.concatenate([vp_ref[0, rows], vc_ref[0, rows], vn_ref[0, rows], vx_ref[0, rows]], axis=1)
        heads = [kh * GROUP + g for g in range(GROUP)]
        q4 = jnp.concatenate([q_ref[0, h * HEAD_DIM:(h + 1) * HEAD_DIM] for h in heads], axis=1)
        s = lax.dot_general(kcat, q4, TN, preferred_element_type=F32)
        s_prev = jnp.where(ok_prev, s[:BLOCK], NEG)
        s_cur = s[BLOCK:2 * BLOCK]
        s_next = jnp.where(ok_next, s[2 * BLOCK:3 * BLOCK], NEG)
        s_ctx = s[3 * BLOCK:]
        sink = jnp.concatenate([jnp.full((1, BLOCK), sink_ref[h], F32) for h in heads], axis=1)
        m = sink
        for part in (s_prev, s_cur, s_next, s_ctx):
            m = jnp.maximum(m, jnp.max(part, axis=0, keepdims=True))
        denom = jnp.exp(sink - m)
        ps = []
        for part in (s_prev, s_cur, s_next, s_ctx):
            p = jnp.exp(part - m)
            denom = denom + jnp.sum(p, axis=0, keepdims=True)
            ps.append(p.astype(BF16))
        o4 = jnp.dot(vcat, jnp.concatenate(ps, axis=0), preferred_element_type=F32) / denom
        for g, h in enumerate(heads):
            o_ref[0, h * HEAD_DIM:(h + 1) * HEAD_DIM] = o4[:, g * BLOCK:(g + 1) * BLOCK].astype(BF16)


def _attention(sink, q_t, k_t, v_t, kx_t, vx_t):
    b, dq, l = q_t.shape
    nk = k_t.shape[1]
    lc = kx_t.shape[2]
    nb = l // BLOCK
    prev = pl.BlockSpec((1, nk, BLOCK), lambda i, j: (i, 0, jnp.maximum(j - 1, 0)))
    cur = pl.BlockSpec((1, nk, BLOCK), lambda i, j: (i, 0, j))
    nxt = pl.BlockSpec((1, nk, BLOCK), lambda i, j: (i, 0, jnp.minimum(j + 1, nb - 1)))
    ctx = pl.BlockSpec((1, nk, lc), lambda i, j: (i, 0, 0))
    return pl.pallas_call(
        _attn_kernel,
        out_shape=jax.ShapeDtypeStruct((b, dq, l), BF16),
        grid=(b, nb),
        in_specs=[pl.BlockSpec(memory_space=pltpu.SMEM),
                  pl.BlockSpec((1, dq, BLOCK), lambda i, j: (i, 0, j)),
                  prev, cur, nxt, prev, cur, nxt, ctx, ctx],
        out_specs=pl.BlockSpec((1, dq, BLOCK), lambda i, j: (i, 0, j)),
        compiler_params=_params("arbitrary", "arbitrary"),
        name="band_attention",
    )(sink, q_t, k_t, k_t, k_t, v_t, v_t, v_t, kx_t, vx_t)


def _proj_ffn_kernel(x_ref, a_ref, g1_ref, sh2_ref, sc2_ref, g2_ref, wo_ref, bo_ref, n2_ref,
                     w1_ref, w3_ref, w2_ref, o_ref, *, ff_chunk):
    proj = lax.dot_general(a_ref[0], wo_ref[...], TN, preferred_element_type=F32)
    x1 = x_ref[0] + g1_ref[0] * (proj + bo_ref[...])
    h = _norm_mod(x1, n2_ref[...], sc2_ref[0], sh2_ref[0]).astype(BF16)
    acc = None
    for c0 in range(0, w1_ref.shape[1], ff_chunk):
        a = jnp.dot(h, w1_ref[:, c0:c0 + ff_chunk], preferred_element_type=F32)
        b = jnp.dot(h, w3_ref[:, c0:c0 + ff_chunk], preferred_element_type=F32)
        act = ((a * jax.nn.sigmoid(a)) * b).astype(BF16)
        part = jnp.dot(act, w2_ref[c0:c0 + ff_chunk, :], preferred_element_type=F32)
        acc = part if acc is None else acc + part
    o_ref[0] = x1 + g2_ref[0] * acc


def _proj_ffn(x, a_t, g1, sh2, sc2, g2, wo, bo, n2, w1, w3, w2, *, tile, ff_chunk):
    b, l, d = x.shape
    c = a_t.shape[1]
    vec = pl.BlockSpec((1, 1, d), lambda i, j: (i, 0, 0))
    return pl.pallas_call(
        functools.partial(_proj_ffn_kernel, ff_chunk=ff_chunk),
        out_shape=jax.ShapeDtypeStruct((b, l, d), F32),
        grid=(b, l // tile),
        in_specs=[pl.BlockSpec((1, tile, d), lambda i, j: (i, j, 0)),
                  pl.BlockSpec((1, c, tile), lambda i, j: (i, 0, j)),
                  vec, vec, vec, vec,
                  _const_spec(wo.shape), _const_spec((1, d)), _const_spec((1, d)),
                  _const_spec(w1.shape), _const_spec(w3.shape), _const_spec(w2.shape)],
        out_specs=pl.BlockSpec((1, tile, d), lambda i, j: (i, j, 0)),
        compiler_params=_params("arbitrary", "arbitrary"),
        name="proj_swiglu",
    )(x, a_t, g1, sh2, sc2, g2, wo, bo, n2, w1, w3, w2)


def _inproj_kernel(x_ref, sh_ref, sc_ref, g_ref, w_ref, b_ref, z_ref):
    h = _norm_mod(x_ref[0], g_ref[...], sc_ref[0], sh_ref[0]).astype(BF16)
    z = lax.dot_general(w_ref[...], h, NT, preferred_element_type=F32) + b_ref[...]
    z_ref[0] = z.astype(BF16)


def _inproj(x, sh, sc, g, w_t, b_col, *, tile):
    b, l, d = x.shape
    c3 = w_t.shape[0]
    vec = pl.BlockSpec((1, 1, d), lambda i, j: (i, 0, 0))
    return pl.pallas_call(
        _inproj_kernel,
        out_shape=jax.ShapeDtypeStruct((b, c3, l), BF16),
        grid=(b, l // tile),
        in_specs=[pl.BlockSpec((1, tile, d), lambda i, j: (i, j, 0)), vec, vec,
                  _const_spec((1, d)), _const_spec(w_t.shape), _const_spec((c3, 1))],
        out_specs=pl.BlockSpec((1, c3, tile), lambda i, j: (i, 0, j)),
        compiler_params=_params("arbitrary", "arbitrary"),
        name="hyena_inproj",
    )(x, sh, sc, g, w_t, b_col)


def _filter_mlp_kernel(z_ref, w1_ref, b1_ref, f1_ref, w2_ref, b2_ref, f2_ref, o_ref):
    h = jnp.sin(f1_ref[...] * (jnp.dot(z_ref[...], w1_ref[...], precision=HI, preferred_element_type=F32)
                               + b1_ref[...]))
    o_ref[...] = jnp.sin(f2_ref[...] * (jnp.dot(h, w2_ref[...], precision=HI, preferred_element_type=F32)
                                        + b2_ref[...]))


def _filter_mlp(zz, w1, b1, f1, w2, b2, f2):
    n, e = zz.shape
    hid = w1.shape[1]
    rows = min(n, 2048)
    row = lambda a: a.reshape(1, -1)
    return pl.pallas_call(
        _filter_mlp_kernel,
        out_shape=jax.ShapeDtypeStruct((n, hid), F32),
        grid=(n // rows,),
        in_specs=[pl.BlockSpec((rows, e), lambda i: (i, 0)),
                  _const_spec(w1.shape), _const_spec((1, hid)), _const_spec((1, hid)),
                  _const_spec(w2.shape), _const_spec((1, hid)), _const_spec((1, hid))],
        out_specs=pl.BlockSpec((rows, hid), lambda i: (i, 0)),
        compiler_params=_params("arbitrary"),
        name="hyena_filter_mlp",
    )(zz, w1, row(b1), row(f1), w2, row(b2), row(f2))


def _split(x):
    hi = x.astype(BF16)
    return hi, (x - hi.astype(F32)).astype(BF16)


def _dot3(a_hi, a_lo, b):
    b_hi, b_lo = _split(b)
    d = functools.partial(jnp.dot, preferred_element_type=F32)
    return d(a_hi, b_hi) + (d(a_hi, b_lo) + d(a_lo, b_hi))


def _dot3r(a, b_hi, b_lo):
    a_hi, a_lo = _split(a)
    d = functools.partial(jnp.dot, preferred_element_type=F32)
    return d(a_hi, b_hi) + (d(a_hi, b_lo) + d(a_lo, b_hi))


def _filter_kernel(hdn_ref, wf_ref, wb_ref, t_ref, dec_ref, f1h_ref, f1l_ref, twr_ref, twi_ref, gh_ref, gl_ref,
                   o_ref, s_ref, *, cb, n1, pitch):
    l = hdn_ref.shape[0] // 2
    hf = lax.dot_general(wf_ref[...], hdn_ref[:l], NT, precision=HI, preferred_element_type=F32)
    hb = lax.dot_general(wb_ref[...], hdn_ref[l:], NT, precision=HI, preferred_element_type=F32)
    win = jnp.exp(-t_ref[...] * jnp.abs(dec_ref[...]))
    lane = lax.broadcasted_iota(jnp.int32, win.shape, 1)
    k = jnp.where(lane == l, 0.0, jnp.concatenate([hf, hb], axis=1) * win)
    k = k / jnp.sum(jnp.abs(k), axis=1, keepdims=True)
    for j in range(n1):
        s_ref[j * pitch:j * pitch + cb, :] = k[:, j * LANES:(j + 1) * LANES]
    tr, ti = twr_ref[...], twi_ref[...]

    def body(c, carry):
        kc = s_ref[pl.ds(c, n1, stride=pitch), :]
        ri = _dot3(f1h_ref[...], f1l_ref[...], kc)
        re, im = ri[:n1], ri[n1:]
        a = jnp.concatenate([re * tr - im * ti, re * ti + im * tr], axis=1)
        o_ref[c] = _dot3r(a, gh_ref[...], gl_ref[...])
        return carry

    lax.fori_loop(0, cb, body, 0)


def _filter(hdn, wf_t, wb_t, t_row, dec_col, consts, *, cb, n1):
    n2, hid = hdn.shape
    c = wf_t.shape[0]
    f1h, f1l, twr, twi, gh, gl = consts
    pitch = cb + 8
    return pl.pallas_call(
        functools.partial(_filter_kernel, cb=cb, n1=n1, pitch=pitch),
        out_shape=jax.ShapeDtypeStruct((c, n1, 2 * LANES), F32),
        grid=(c // cb,),
        in_specs=[_const_spec((n2, hid)),
                  pl.BlockSpec((cb, hid), lambda i: (i, 0)), pl.BlockSpec((cb, hid), lambda i: (i, 0)),
                  _const_spec((1, n2)), pl.BlockSpec((cb, 1), lambda i: (i, 0)),
                  _const_spec(f1h.shape), _const_spec(f1l.shape), _const_spec(twr.shape), _const_spec(twi.shape),
                  _const_spec(gh.shape), _const_spec(gl.shape)],
        out_specs=pl.BlockSpec((cb, n1, 2 * LANES), lambda i: (i, 0, 0)),
        scratch_shapes=[pltpu.VMEM((n1 * pitch, LANES), F32)],
        compiler_params=_params("arbitrary"),
        name="hyena_filter",
    )(hdn, wf_t, wb_t, t_row, dec_col, f1h, f1l, twr, twi, gh, gl)


def _conv3_tiles(ref, bi, w_ref, b_ref, nh):
    cb = ref.shape[1]
    lane = lax.broadcasted_iota(jnp.int32, (cb, LANES), 1)
    first, last = lane == 0, lane == LANES - 1
    w0, w1, w2 = [jnp.broadcast_to(w_ref[k], (cb, LANES)) for k in range(3)]
    bias = jnp.broadcast_to(b_ref[...], (cb, LANES))
    zero = jnp.zeros((cb, LANES), BF16)

    @functools.lru_cache(maxsize=None)
    def tile(j):
        z = ref[bi, :, j * LANES:(j + 1) * LANES]
        return z, pltpu.roll(z, 1, axis=1), pltpu.roll(z, LANES - 1, axis=1)

    def conv(j):
        z, fwd, bwd = tile(j)
        prev = jnp.where(first, tile(j - 1)[1] if j else zero, fwd)
        nxt = jnp.where(last, tile(j + 1)[2] if j + 1 < nh else zero, bwd)
        return w0 * prev.astype(F32) + w1 * z.astype(F32) + w2 * nxt.astype(F32) + bias

    return conv


def _longconv_kernel(skip_ref, x0_ref, x1_ref, v_ref, w0_ref, w1_ref, wv_ref, b0_ref, b1_ref, bv_ref, kh_ref,
                     a1_ref, twr_ref, twi_ref, g_ref, gc_ref, p_ref, o_ref, s_ref, so_ref,
                     *, cb, nh, n1, group, pitch):
    for bi in range(2):
        x0 = _conv3_tiles(x0_ref, bi, w0_ref, b0_ref, nh)
        x1 = _conv3_tiles(x1_ref, bi, w1_ref, b1_ref, nh)
        v = _conv3_tiles(v_ref, bi, wv_ref, bv_ref, nh)
        for j in range(nh):
            s_ref[bi, j * pitch:j * pitch + cb, :] = v(j) * x1(j)
            s_ref[2 + bi, j * pitch:j * pitch + cb, :] = x0(j)

    c_base = pl.program_id(0) * cb
    tr, ti = twr_ref[...], twi_ref[...]

    def body(gi, carry):
        us, st1 = [], []
        for k in range(group):
            ci = gi * group + k
            u = jnp.concatenate([s_ref[0, pl.ds(ci, nh, stride=pitch), :],
                                 s_ref[1, pl.ds(ci, nh, stride=pitch), :]], axis=0)
            us.append(u)
            ri = jnp.dot(a1_ref[...], u.astype(BF16), preferred_element_type=F32)
            re, im = ri[:n1], ri[n1:]
            st1.append(jnp.concatenate([re * tr - im * ti, re * ti + im * tr], axis=1))
        spec = jnp.dot(jnp.concatenate(st1, axis=0).astype(BF16), g_ref[...], preferred_element_type=F32)
        prods = []
        for k in range(group):
            xs = spec[k * n1:(k + 1) * n1]
            xr, xi = xs[:, :LANES], xs[:, LANES:]
            kk = kh_ref[gi * group + k]
            kr, ki = kk[:, :LANES], kk[:, LANES:]
            prods.append(jnp.concatenate([xr * kr - xi * ki, xr * ki + xi * kr], axis=1))
        back = jnp.dot(jnp.concatenate(prods, axis=0).astype(BF16), gc_ref[...], preferred_element_type=F32)
        for k in range(group):
            bs = back[k * n1:(k + 1) * n1]
            br, bim = bs[:, :LANES], bs[:, LANES:]
            st = jnp.concatenate([br * tr + bim * ti, bim * tr - br * ti], axis=0)
            y = jnp.dot(p_ref[...], st.astype(BF16), preferred_element_type=F32)
            ci = gi * group + k
            x0 = jnp.concatenate([s_ref[2, pl.ds(ci, nh, stride=pitch), :],
                                  s_ref[3, pl.ds(ci, nh, stride=pitch), :]], axis=0)
            gated = (y + us[k] * skip_ref[c_base + ci]) * x0
            so_ref[0, pl.ds(ci, nh, stride=pitch), :] = gated[:nh]
            so_ref[1, pl.ds(ci, nh, stride=pitch), :] = gated[nh:]
        return carry

    lax.fori_loop(0, cb // group, body, 0)
    for bi in range(2):
        for j in range(nh):
            o_ref[bi, :, j * LANES:(j + 1) * LANES] = so_ref[bi, j * pitch:j * pitch + cb, :].astype(BF16)


def _longconv(z_t, khat, conv_w, conv_b, skip, consts, *, cb, nh, n1, group):
    b, c3, l = z_t.shape
    c = c3 // 3
    a1, twr, twi, g, gc, p = consts
    ncb = c // cb
    pitch = cb + 8
    w3 = conv_w.reshape(3, c3, 1)
    b3 = conv_b.reshape(c3, 1)

    def zspec(part):
        return pl.BlockSpec((2, cb, l), lambda i, j, part=part: (j, part * ncb + i, 0))

    def wspec(part):
        return pl.BlockSpec((3, cb, 1), lambda i, j, part=part: (0, part * ncb + i, 0))

    def bspec(part):
        return pl.BlockSpec((cb, 1), lambda i, j, part=part: (part * ncb + i, 0))

    return pl.pallas_call(
        functools.partial(_longconv_kernel, cb=cb, nh=nh, n1=n1, group=group, pitch=pitch),
        out_shape=jax.ShapeDtypeStruct((b, c, l), BF16),
        grid=(ncb, b // 2),
        in_specs=[pl.BlockSpec(memory_space=pltpu.SMEM), zspec(0), zspec(1), zspec(2),
                  wspec(0), wspec(1), wspec(2), bspec(0), bspec(1), bspec(2),
                  pl.BlockSpec((cb, n1, 2 * LANES), lambda i, j: (i, 0, 0)),
                  _const_spec(a1.shape), _const_spec(twr.shape), _const_spec(twi.shape),
                  _const_spec(g.shape), _const_spec(gc.shape), _const_spec(p.shape)],
        out_specs=pl.BlockSpec((2, cb, l), lambda i, j: (j, i, 0)),
        scratch_shapes=[pltpu.VMEM((4, nh * pitch, LANES), F32), pltpu.VMEM((2, nh * pitch, LANES), F32)],
        compiler_params=_params("arbitrary", "arbitrary"),
        name="hyena_longconv",
    )(skip, z_t, z_t, z_t, w3, w3, w3, b3, b3, b3, khat, a1, twr, twi, g, gc, p)


def _dft_tables(l):
    n = 2 * l
    n1 = n // LANES
    nh = l // LANES
    f1 = np.exp(-2j * np.pi * np.outer(np.arange(n1), np.arange(n1)) / n1)
    f2 = np.exp(-2j * np.pi * np.outer(np.arange(LANES), np.arange(LANES)) / LANES)
    tw = np.exp(-2j * np.pi * np.outer(np.arange(n1), np.arange(LANES)) / n)
    fh = f1[:, :nh]
    a1 = np.block([[fh.real, -fh.imag], [fh.imag, fh.real]])
    g = np.block([[f2.real, f2.imag], [-f2.imag, f2.real]])
    gc = np.block([[f2.real, -f2.imag], [f2.imag, f2.real]])
    ci = np.conj(f1)[:nh, :]
    p = np.block([[ci.real, -ci.imag], [ci.imag, ci.real]]) / n
    f1full = np.concatenate([f1.real, f1.imag], axis=0)

    def bf(a):
        return jnp.asarray(a, F32).astype(BF16)

    def hilo(a):
        a32 = jnp.asarray(a, F32)
        hi = a32.astype(BF16)
        return hi, (a32 - hi.astype(F32)).astype(BF16)

    data = (bf(a1), jnp.asarray(tw.real, F32), jnp.asarray(tw.imag, F32), bf(g), bf(gc), bf(p))
    filt = hilo(f1full) + (jnp.asarray(tw.real, F32), jnp.asarray(tw.imag, F32)) + hilo(g)
    return data, filt, n1, nh


def _rope_tables_t(l):
    rows = l // GRID_W
    row = np.repeat(np.arange(rows, dtype=np.float64), GRID_W)
    col = np.tile(np.arange(GRID_W, dtype=np.float64), rows)
    inv_freq = ROPE_THETA ** (-np.arange(AXIS_FREQS, dtype=np.float64) / AXIS_FREQS)
    ang = np.concatenate([inv_freq[:, None] * row[None, :], inv_freq[:, None] * col[None, :]], axis=0)
    return jnp.asarray(np.cos(ang), F32), jnp.asarray(np.sin(ang), F32)


def _filter_positions(l):
    t = np.linspace(0.0, 1.0, l)
    w = 2.0 * math.pi * np.arange(l, dtype=np.float64) / l
    bands = np.linspace(1e-4, HY_BANDS - 1, HY_BANDS)[None, :]
    z = np.concatenate([t[:, None], np.cos(bands * w[:, None]), -np.sin(bands * w[:, None])], axis=-1)
    idx = np.concatenate([np.arange(l), np.zeros((1,), np.int64), np.arange(l - 1, 0, -1)])
    return jnp.asarray(z[idx], F32), jnp.asarray(t[idx][None, :], F32)


def kernel(x, c, ctx, c_ctx, ada_w, ada_b, norm1_g, norm2_g, attn_wqkv, attn_wo, attn_q_gain, attn_k_gain, attn_sink, hy_w_in, hy_b_in, hy_conv_w, hy_conv_b, hy_f_w1, hy_f_b1, hy_f_freq1, hy_f_w2, hy_f_b2, hy_f_freq2, hy_f_wout, hy_decay, hy_skip, hy_w_out, hy_b_out, ffn_w1, ffn_w3, ffn_w2):
    b, l, d = x.shape
    tile = min(512, l)
    ff = ffn_w1.shape[-1]
    ff_chunk = ff // 2 if (ff // 2) % LANES == 0 else ff

    pad = (-(b + 1)) % 8
    cond = jnp.concatenate([c, c_ctx[None, :], jnp.zeros((pad, d), F32)], axis=0)
    mod = _ada(cond, ada_w, ada_b)

    def chunks(i, rows):
        m = mod[i, rows][:, None, :]
        return [m[..., k * d:(k + 1) * d] for k in range(6)]

    row = lambda a: a.reshape(1, -1)
    col = lambda a: a.reshape(-1, 1)

    sh1, sc1, g1, sh2, sc2, g2 = chunks(0, slice(0, b))
    csh1, csc1 = [jnp.broadcast_to(m, (b, 1, d)) for m in chunks(0, slice(b, b + 1))[:2]]
    wqkv_t = attn_wqkv[0].T.astype(BF16)
    cos_t, sin_t = _rope_tables_t(l)
    qg, kg = col(attn_q_gain[0]), col(attn_k_gain[0])
    q_t, k_t, v_t = _qkv(x, sh1, sc1, row(norm1_g[0]), wqkv_t, qg, kg, cos_t, sin_t, n_q=N_HEADS, tile=tile)
    kx_t, vx_t = _qkv(ctx, csh1, csc1, row(norm1_g[0]), wqkv_t[N_HEADS * HEAD_DIM:], qg, kg, None, None,
                      n_q=0, tile=ctx.shape[1])
    o_t = _attention(attn_sink[0], q_t, k_t, v_t, kx_t, vx_t)
    x = _proj_ffn(x, o_t, g1, sh2, sc2, g2, attn_wo[0].astype(BF16), jnp.zeros((1, d), F32), row(norm2_g[0]),
                  ffn_w1[0].astype(BF16), ffn_w3[0].astype(BF16), ffn_w2[0].astype(BF16),
                  tile=tile, ff_chunk=ff_chunk)

    sh1, sc1, g1, sh2, sc2, g2 = chunks(1, slice(0, b))
    data_consts, filt_consts, n1, nh = _dft_tables(l)
    zz, t_row = _filter_positions(l)
    hdn = _filter_mlp(zz, hy_f_w1[0], hy_f_b1[0], hy_f_freq1[0], hy_f_w2[0], hy_f_b2[0], hy_f_freq2[0])
    khat = _filter(hdn, hy_f_wout[0][:, :d].T, hy_f_wout[0][:, d:].T, t_row, col(hy_decay[0]), filt_consts,
                   cb=32, n1=n1)
    z_t = _inproj(x, sh1, sc1, row(norm1_g[1]), hy_w_in[0].T.astype(BF16), col(hy_b_in[0]), tile=tile)
    gated = _longconv(z_t, khat, hy_conv_w[0], hy_conv_b[0], hy_skip[0], data_consts, cb=32, nh=nh, n1=n1, group=8)
    x = _proj_ffn(x, gated, g1, sh2, sc2, g2, hy_w_out[0].astype(BF16), row(hy_b_out[0]),
                  row(norm2_g[1]), ffn_w1[1].astype(BF16), ffn_w3[1].astype(BF16), ffn_w2[1].astype(BF16),
                  tile=tile, ff_chunk=ff_chunk)
    return x
```

```python
import functools
import math

import numpy as np
import jax
import jax.numpy as jnp
from jax import lax
from jax.experimental import pallas as pl
from jax.experimental.pallas import tpu as pltpu

N_HEADS = 16
N_KV_HEADS = 4
HEAD_DIM = 64
GROUP = N_HEADS // N_KV_HEADS
ROPE_HALF = HEAD_DIM // 2
AXIS_FREQS = ROPE_HALF // 2
BLOCK = 128
GRID_W = 64
ROPE_THETA = 10000.0
ATTN_SCALE = HEAD_DIM ** -0.5
LOG2E = math.log2(math.e)
HY_BANDS = 16
EPS = 1e-6
NEG = -1e30
LANES = 128
VMEM_LIMIT = 56 * 1024 * 1024

F32 = jnp.float32
BF16 = jnp.bfloat16
HI = lax.Precision.HIGHEST
NT = (((1,), (1,)), ((), ()))
TN = (((0,), (0,)), ((), ()))


def _params(*sem):
    return pltpu.CompilerParams(dimension_semantics=sem, vmem_limit_bytes=VMEM_LIMIT)


def _const_spec(shape):
    n = len(shape)
    return pl.BlockSpec(shape, lambda *_: (0,) * n, pipeline_mode=pl.Buffered(1))


def _norm_mod(x, g, sc, sh):
    y = x * lax.rsqrt(jnp.mean(x * x, axis=-1, keepdims=True) + EPS)
    return (y * g) * (1.0 + sc) + sh


def _ada_kernel(cond_ref, w_ref, b_ref, o_ref):
    cnd = cond_ref[...]
    s = (cnd * jax.nn.sigmoid(cnd)).astype(BF16)
    o_ref[0] = jnp.dot(s, w_ref[0].astype(BF16), preferred_element_type=F32) + b_ref[0]


def _ada(cond, ada_w, ada_b):
    depth, d, d6 = ada_w.shape
    r = cond.shape[0]
    cw = 1536
    return pl.pallas_call(
        _ada_kernel,
        out_shape=jax.ShapeDtypeStruct((depth, r, d6), F32),
        grid=(depth, d6 // cw),
        in_specs=[pl.BlockSpec((r, d), lambda i, j: (0, 0)),
                  pl.BlockSpec((1, d, cw), lambda i, j: (i, 0, j)),
                  pl.BlockSpec((1, 1, cw), lambda i, j: (i, 0, j))],
        out_specs=pl.BlockSpec((1, r, cw), lambda i, j: (i, 0, j)),
        compiler_params=_params("arbitrary", "arbitrary"),
        name="ada_mod",
    )(cond, ada_w, ada_b.reshape(depth, 1, d6))


def _head_norm(t, gain, n_heads):
    t3 = t.reshape(n_heads, HEAD_DIM, t.shape[-1])
    ms = jnp.mean(t3 * t3, axis=1, keepdims=True)
    return (t3 * lax.rsqrt(ms + EPS)) * gain[None]


def _rope(t3, cos, sin):
    x1, x2 = t3[:, :ROPE_HALF], t3[:, ROPE_HALF:]
    c, s = cos[None], sin[None]
    return jnp.concatenate([x1 * c - x2 * s, x1 * s + x2 * c], axis=1)


def _qkv_kernel(*refs, n_q, rope):
    if rope:
        x_ref, sh_ref, sc_ref, g_ref, w_ref, qg_ref, kg_ref, cos_ref, sin_ref = refs[:9]
        outs = refs[9:]
    else:
        x_ref, sh_ref, sc_ref, g_ref, w_ref, qg_ref, kg_ref = refs[:7]
        outs = refs[7:]
    h = _norm_mod(x_ref[0], g_ref[...], sc_ref[0], sh_ref[0]).astype(BF16)
    t = lax.dot_general(w_ref[...], h, NT, preferred_element_type=F32)
    nq = n_q * HEAD_DIM
    nk = N_KV_HEADS * HEAD_DIM
    tt = t.shape[-1]
    k3 = _head_norm(t[nq:nq + nk], kg_ref[...], N_KV_HEADS)
    if rope:
        k3 = _rope(k3, cos_ref[...], sin_ref[...])
    if n_q:
        q_ref, k_ref, v_ref = outs
        q3 = _head_norm(t[:nq], qg_ref[...], n_q)
        if rope:
            q3 = _rope(q3, cos_ref[...], sin_ref[...])
        q_ref[0] = (q3 * (ATTN_SCALE * LOG2E)).reshape(nq, tt).astype(BF16)
    else:
        k_ref, v_ref = outs
    k_ref[0] = k3.reshape(nk, tt).astype(BF16)
    v_ref[0] = t[nq + nk:].astype(BF16)


def _qkv(x, sh, sc, g, w_t, qg, kg, cos_t, sin_t, *, n_q, tile):
    b, l, d = x.shape
    nq, nk = n_q * HEAD_DIM, N_KV_HEADS * HEAD_DIM
    rope = cos_t is not None
    vec = pl.BlockSpec((1, 1, d), lambda i, j: (i, 0, 0))
    in_specs = [pl.BlockSpec((1, tile, d), lambda i, j: (i, j, 0)), vec, vec,
                _const_spec((1, d)), _const_spec(w_t.shape),
                _const_spec((HEAD_DIM, 1)), _const_spec((HEAD_DIM, 1))]
    args = [x, sh, sc, g, w_t, qg, kg]
    if rope:
        in_specs += [pl.BlockSpec((ROPE_HALF, tile), lambda i, j: (0, j))] * 2
        args += [cos_t, sin_t]
    out_shape, out_specs = [], []
    for rows in ([nq] if n_q else []) + [nk, nk]:
        out_shape.append(jax.ShapeDtypeStruct((b, rows, l), BF16))
        out_specs.append(pl.BlockSpec((1, rows, tile), lambda i, j: (i, 0, j)))
    return pl.pallas_call(
        functools.partial(_qkv_kernel, n_q=n_q, rope=rope),
        out_shape=out_shape, grid=(b, l // tile), in_specs=in_specs, out_specs=out_specs,
        compiler_params=_params("arbitrary", "arbitrary"),
        name="qkv_proj" if n_q else "ctx_kv_proj",
    )(*args)


def _attend(sink_ref, q_ref, o_ref, cols, kparts, vparts, biases):
    for kh in range(N_KV_HEADS):
        rows = slice(kh * HEAD_DIM, (kh + 1) * HEAD_DIM)
        kcat = jnp.concatenate([kp(rows) for kp in kparts], axis=1)
        vcat = jnp.concatenate([vp(rows) for vp in vparts], axis=1)
        heads = [kh * GROUP + g for g in range(GROUP)]
        q4 = jnp.concatenate([q_ref[0, h * HEAD_DIM:(h + 1) * HEAD_DIM, cols] for h in heads], axis=1)
        s = lax.dot_general(kcat, q4, TN, preferred_element_type=F32)
        parts = [s[:BLOCK] + biases[0], s[BLOCK:2 * BLOCK], s[2 * BLOCK:3 * BLOCK] + biases[1], s[3 * BLOCK:]]
        sink = jnp.concatenate([jnp.full((1, BLOCK), sink_ref[h] * LOG2E, F32) for h in heads], axis=1)
        m = sink
        for part in parts:
            m = jnp.maximum(m, jnp.max(part, axis=0, keepdims=True))
        denom = jnp.exp2(sink - m)
        ps = []
        for part in parts:
            p = jnp.exp2(part - m)
            denom = denom + jnp.sum(p, axis=0, keepdims=True)
            ps.append(p.astype(BF16))
        o4 = jnp.dot(vcat, jnp.concatenate(ps, axis=0), preferred_element_type=F32) / denom
        for g, h in enumerate(heads):
            o_ref[0, h * HEAD_DIM:(h + 1) * HEAD_DIM, cols] = o4[:, g * BLOCK:(g + 1) * BLOCK].astype(BF16)


def _attn_kernel(sink_ref, q_ref, kp_ref, km_ref, kn_ref, vp_ref, vm_ref, vn_ref, kx_ref, vx_ref, o_ref):
    j = pl.program_id(1)
    nj = pl.num_programs(1)
    wq = GROUP * BLOCK
    key = lax.broadcasted_iota(jnp.int32, (BLOCK, wq), 0)
    qry = lax.broadcasted_iota(jnp.int32, (BLOCK, wq), 1) % BLOCK
    band_prev = jnp.where(key >= qry, 0.0, NEG)
    band_next = jnp.where(key <= qry, 0.0, NEG)
    lo, hi = slice(0, BLOCK), slice(BLOCK, 2 * BLOCK)
    ctx_k = lambda r: kx_ref[0, r]
    ctx_v = lambda r: vx_ref[0, r]
    _attend(sink_ref, q_ref, o_ref, lo,
            [lambda r: kp_ref[0, r], lambda r: km_ref[0, r, lo], lambda r: km_ref[0, r, hi], ctx_k],
            [lambda r: vp_ref[0, r], lambda r: vm_ref[0, r, lo], lambda r: vm_ref[0, r, hi], ctx_v],
            [jnp.where(j > 0, band_prev, NEG), band_next])
    _attend(sink_ref, q_ref, o_ref, hi,
            [lambda r: km_ref[0, r, lo], lambda r: km_ref[0, r, hi], lambda r: kn_ref[0, r], ctx_k],
            [lambda r: vm_ref[0, r, lo], lambda r: vm_ref[0, r, hi], lambda r: vn_ref[0, r], ctx_v],
            [band_prev, jnp.where(j < nj - 1, band_next, NEG)])


def _attention(sink, q_t, k_t, v_t, kx_t, vx_t):
    b, dq, l = q_t.shape
    nk = k_t.shape[1]
    lc = kx_t.shape[2]
    nb = l // BLOCK
    prev = pl.BlockSpec((1, nk, BLOCK), lambda i, j: (i, 0, jnp.maximum(2 * j - 1, 0)))
    mid = pl.BlockSpec((1, nk, 2 * BLOCK), lambda i, j: (i, 0, j))
    nxt = pl.BlockSpec((1, nk, BLOCK), lambda i, j: (i, 0, jnp.minimum(2 * j + 2, nb - 1)))
    ctx = pl.BlockSpec((1, nk, lc), lambda i, j: (i, 0, 0))
    return pl.pallas_call(
        _attn_kernel,
        out_shape=jax.ShapeDtypeStruct((b, dq, l), BF16),
        grid=(b, nb // 2),
        in_specs=[pl.BlockSpec(memory_space=pltpu.SMEM),
                  pl.BlockSpec((1, dq, 2 * BLOCK), lambda i, j: (i, 0, j)),
                  prev, mid, nxt, prev, mid, nxt, ctx, ctx],
        out_specs=pl.BlockSpec((1, dq, 2 * BLOCK), lambda i, j: (i, 0, j)),
        compiler_params=_params("arbitrary", "arbitrary"),
        name="band_attention",
    )(sink, q_t, k_t, k_t, k_t, v_t, v_t, v_t, kx_t, vx_t)


def _proj_ffn_kernel(x_ref, a_ref, g1_ref, sh2_ref, sc2_ref, g2_ref, wo_ref, bo_ref, n2_ref,
                     w1_ref, w3_ref, w2_ref, o_ref, *, ff_chunk):
    proj = lax.dot_general(a_ref[0], wo_ref[...], TN, preferred_element_type=F32)
    x1 = x_ref[0] + g1_ref[0] * (proj + bo_ref[...])
    h = _norm_mod(x1, n2_ref[...], sc2_ref[0], sh2_ref[0]).astype(BF16)
    acc = None
    for c0 in range(0, w1_ref.shape[1], ff_chunk):
        a = jnp.dot(h, w1_ref[:, c0:c0 + ff_chunk], preferred_element_type=F32)
        b = jnp.dot(h, w3_ref[:, c0:c0 + ff_chunk], preferred_element_type=F32)
        act = ((a * jax.nn.sigmoid(a)) * b).astype(BF16)
        part = jnp.dot(act, w2_ref[c0:c0 + ff_chunk, :], preferred_element_type=F32)
        acc = part if acc is None else acc + part
    o_ref[0] = x1 + g2_ref[0] * acc


def _proj_ffn(x, a_t, g1, sh2, sc2, g2, wo, bo, n2, w1, w3, w2, *, tile, ff_chunk):
    b, l, d = x.shape
    c = a_t.shape[1]
    vec = pl.BlockSpec((1, 1, d), lambda i, j: (i, 0, 0))
    return pl.pallas_call(
        functools.partial(_proj_ffn_kernel, ff_chunk=ff_chunk),
        out_shape=jax.ShapeDtypeStruct((b, l, d), F32),
        grid=(b, l // tile),
        in_specs=[pl.BlockSpec((1, tile, d), lambda i, j: (i, j, 0)),
                  pl.BlockSpec((1, c, tile), lambda i, j: (i, 0, j)),
                  vec, vec, vec, vec,
                  _const_spec(wo.shape), _const_spec((1, d)), _const_spec((1, d)),
                  _const_spec(w1.shape), _const_spec(w3.shape), _const_spec(w2.shape)],
        out_specs=pl.BlockSpec((1, tile, d), lambda i, j: (i, j, 0)),
        compiler_params=_params("arbitrary", "arbitrary"),
        name="proj_swiglu",
    )(x, a_t, g1, sh2, sc2, g2, wo, bo, n2, w1, w3, w2)


def _inproj_kernel(x_ref, sh_ref, sc_ref, g_ref, w_ref, b_ref, z_ref):
    h = _norm_mod(x_ref[0], g_ref[...], sc_ref[0], sh_ref[0]).astype(BF16)
    z = lax.dot_general(w_ref[...], h, NT, preferred_element_type=F32) + b_ref[...]
    z_ref[0] = z.astype(BF16)


def _inproj(x, sh, sc, g, w_t, b_col, *, tile):
    b, l, d = x.shape
    c3 = w_t.shape[0]
    vec = pl.BlockSpec((1, 1, d), lambda i, j: (i, 0, 0))
    return pl.pallas_call(
        _inproj_kernel,
        out_shape=jax.ShapeDtypeStruct((b, c3, l), BF16),
        grid=(b, l // tile),
        in_specs=[pl.BlockSpec((1, tile, d), lambda i, j: (i, j, 0)), vec, vec,
                  _const_spec((1, d)), _const_spec(w_t.shape), _const_spec((c3, 1))],
        out_specs=pl.BlockSpec((1, c3, tile), lambda i, j: (i, 0, j)),
        compiler_params=_params("arbitrary", "arbitrary"),
        name="hyena_inproj",
    )(x, sh, sc, g, w_t, b_col)


def _filter_taps_kernel(z_ref, t_ref, w1_ref, b1_ref, f1_ref, w2_ref, b2_ref, f2_ref, wo_ref, dec_ref,
                        k_ref, norm_ref, *, l):
    i = pl.program_id(0)
    dot = functools.partial(jnp.dot, precision=HI, preferred_element_type=F32)
    h = jnp.sin(f1_ref[...] * (dot(z_ref[...], w1_ref[...]) + b1_ref[...]))
    h = jnp.sin(f2_ref[...] * (dot(h, w2_ref[...]) + b2_ref[...]))
    win = jnp.exp(-t_ref[...] * jnp.abs(dec_ref[...]))
    rows = z_ref.shape[0]
    slot = i * rows + lax.broadcasted_iota(jnp.int32, (rows, 1), 0)
    k = jnp.where(slot == l, 0.0, dot(h, wo_ref[...]) * win)

    @pl.when(i == 0)
    def _():
        norm_ref[...] = jnp.zeros_like(norm_ref)

    norm_ref[...] += jnp.sum(jnp.abs(k), axis=0, keepdims=True)
    k_ref[...] = k.T


def _filter_taps(zz, t_col, w1, b1, f1, w2, b2, f2, wout, dec_row, *, l, rows):
    n2, e = zz.shape
    hid = w1.shape[1]
    c = dec_row.shape[1]
    nblk = n2 // rows
    row = lambda a: a.reshape(1, -1)
    return pl.pallas_call(
        functools.partial(_filter_taps_kernel, l=l),
        out_shape=[jax.ShapeDtypeStruct((c, n2), F32), jax.ShapeDtypeStruct((1, c), F32)],
        grid=(nblk,),
        in_specs=[pl.BlockSpec((rows, e), lambda i: (i, 0)), pl.BlockSpec((rows, 1), lambda i: (i, 0)),
                  _const_spec(w1.shape), _const_spec((1, hid)), _const_spec((1, hid)),
                  _const_spec(w2.shape), _const_spec((1, hid)), _const_spec((1, hid)),
                  pl.BlockSpec((hid, c), lambda i: (0, (2 * i) // nblk)), _const_spec((1, c))],
        out_specs=[pl.BlockSpec((c, rows), lambda i: (0, i)), pl.BlockSpec((1, c), lambda i: (0, 0))],
        compiler_params=_params("arbitrary"),
        name="hyena_filter_taps",
    )(zz, t_col, w1, row(b1), row(f1), w2, row(b2), row(f2), wout, dec_row)


def _split(x):
    hi = x.astype(BF16)
    return hi, (x - hi.astype(F32)).astype(BF16)


def _dot3(a_hi, a_lo, b):
    b_hi, b_lo = _split(b)
    d = functools.partial(jnp.dot, preferred_element_type=F32)
    return d(a_hi, b_hi) + (d(a_hi, b_lo) + d(a_lo, b_hi))


def _dot3r(a, b_hi, b_lo):
    a_hi, a_lo = _split(a)
    d = functools.partial(jnp.dot, preferred_element_type=F32)
    return d(a_hi, b_hi) + (d(a_hi, b_lo) + d(a_lo, b_hi))


def _filter_fft_kernel(k_ref, norm_ref, f1h_ref, f1l_ref, twr_ref, twi_ref, gh_ref, gl_ref, o_ref, s_ref,
                       *, cb, n1, pitch, group):
    k = k_ref[...] / norm_ref[...]
    for j in range(n1):
        s_ref[j * pitch:j * pitch + cb, :] = k[:, j * LANES:(j + 1) * LANES]
    tr, ti = twr_ref[...], twi_ref[...]

    def body(gi, carry):
        st = []
        for g in range(group):
            kc = s_ref[pl.ds(gi * group + g, n1, stride=pitch), :]
            ri = _dot3(f1h_ref[...], f1l_ref[...], kc)
            re, im = ri[:n1], ri[n1:]
            st.append(jnp.concatenate([re * tr - im * ti, re * ti + im * tr], axis=1))
        spec = _dot3r(jnp.concatenate(st, axis=0), gh_ref[...], gl_ref[...])
        for g in range(group):
            o_ref[gi * group + g] = spec[g * n1:(g + 1) * n1]
        return carry

    lax.fori_loop(0, cb // group, body, 0)


def _filter_fft(taps, norm_col, consts, *, cb, n1, group):
    c, n2 = taps.shape
    f1h, f1l, twr, twi, gh, gl = consts
    pitch = cb + 8
    return pl.pallas_call(
        functools.partial(_filter_fft_kernel, cb=cb, n1=n1, pitch=pitch, group=group),
        out_shape=jax.ShapeDtypeStruct((c, n1, 2 * LANES), F32),
        grid=(c // cb,),
        in_specs=[pl.BlockSpec((cb, n2), lambda i: (i, 0)), pl.BlockSpec((cb, 1), lambda i: (i, 0)),
                  _const_spec(f1h.shape), _const_spec(f1l.shape), _const_spec(twr.shape), _const_spec(twi.shape),
                  _const_spec(gh.shape), _const_spec(gl.shape)],
        out_specs=pl.BlockSpec((cb, n1, 2 * LANES), lambda i: (i, 0, 0)),
        scratch_shapes=[pltpu.VMEM((n1 * pitch, LANES), F32)],
        compiler_params=_params("arbitrary"),
        name="hyena_filter_fft",
    )(taps, norm_col, f1h, f1l, twr, twi, gh, gl)


def _conv3_tiles(ref, bi, w_ref, b_ref, nh):
    cb = ref.shape[1]
    lane = lax.broadcasted_iota(jnp.int32, (cb, LANES), 1)
    first, last = lane == 0, lane == LANES - 1
    w0, w1, w2 = [jnp.broadcast_to(w_ref[k], (cb, LANES)) for k in range(3)]
    bias = jnp.broadcast_to(b_ref[...], (cb, LANES))
    zero = jnp.zeros((cb, LANES), BF16)

    @functools.lru_cache(maxsize=None)
    def tile(j):
        z = ref[bi, :, j * LANES:(j + 1) * LANES]
        return z, pltpu.roll(z, 1, axis=1), pltpu.roll(z, LANES - 1, axis=1)

    def conv(j):
        z, fwd, bwd = tile(j)
        prev = jnp.where(first, tile(j - 1)[1] if j else zero, fwd)
        nxt = jnp.where(last, tile(j + 1)[2] if j + 1 < nh else zero, bwd)
        return w0 * prev.astype(F32) + w1 * z.astype(F32) + w2 * nxt.astype(F32) + bias

    return conv


def _longconv_kernel(skip_ref, x0_ref, x1_ref, v_ref, w0_ref, w1_ref, wv_ref, b0_ref, b1_ref, bv_ref, kh_ref,
                     a1_ref, twr_ref, twi_ref, g_ref, gc_ref, p_ref, o_ref, s_ref, so_ref,
                     *, cb, nh, n1, group, pitch):
    for bi in range(2):
        x0 = _conv3_tiles(x0_ref, bi, w0_ref, b0_ref, nh)
        x1 = _conv3_tiles(x1_ref, bi, w1_ref, b1_ref, nh)
        v = _conv3_tiles(v_ref, bi, wv_ref, bv_ref, nh)
        for j in range(nh):
            s_ref[bi, j * pitch:j * pitch + cb, :] = v(j) * x1(j)
            s_ref[2 + bi, j * pitch:j * pitch + cb, :] = x0(j)

    c_base = pl.program_id(0) * cb
    tr, ti = twr_ref[...], twi_ref[...]

    def body(gi, carry):
        us, st1 = [], []
        for k in range(group):
            ci = gi * group + k
            u = jnp.concatenate([s_ref[0, pl.ds(ci, nh, stride=pitch), :],
                                 s_ref[1, pl.ds(ci, nh, stride=pitch), :]], axis=0)
            us.append(u)
            ri = jnp.dot(a1_ref[...], u.astype(BF16), preferred_element_type=F32)
            re, im = ri[:n1], ri[n1:]
            st1.append(jnp.concatenate([re * tr - im * ti, re * ti + im * tr], axis=1))
        spec = jnp.dot(jnp.concatenate(st1, axis=0).astype(BF16), g_ref[...], preferred_element_type=F32)
        prods = []
        for k in range(group):
            xs = spec[k * n1:(k + 1) * n1]
            xr, xi = xs[:, :LANES], xs[:, LANES:]
            kk = kh_ref[gi * group + k]
            kr, ki = kk[:, :LANES], kk[:, LANES:]
            prods.append(jnp.concatenate([xr * kr - xi * ki, xr * ki + xi * kr], axis=1))
        back = jnp.dot(jnp.concatenate(prods, axis=0).astype(BF16), gc_ref[...], preferred_element_type=F32)
        for k in range(group):
            bs = back[k * n1:(k + 1) * n1]
            br, bim = bs[:, :LANES], bs[:, LANES:]
            st = jnp.concatenate([br * tr + bim * ti, bim * tr - br * ti], axis=0)
            y = jnp.dot(p_ref[...], st.astype(BF16), preferred_element_type=F32)
            ci = gi * group + k
            x0 = jnp.concatenate([s_ref[2, pl.ds(ci, nh, stride=pitch), :],
                                  s_ref[3, pl.ds(ci, nh, stride=pitch), :]], axis=0)
            gated = (y + us[k] * skip_ref[c_base + ci]) * x0
            so_ref[0, pl.ds(ci, nh, stride=pitch), :] = gated[:nh]
            so_ref[1, pl.ds(ci, nh, stride=pitch), :] = gated[nh:]
        return carry

    lax.fori_loop(0, cb // group, body, 0)
    for bi in range(2):
        for j in range(nh):
            o_ref[bi, :, j * LANES:(j + 1) * LANES] = so_ref[bi, j * pitch:j * pitch + cb, :].astype(BF16)


def _longconv(z_t, khat, conv_w, conv_b, skip, consts, *, cb, nh, n1, group):
    b, c3, l = z_t.shape
    c = c3 // 3
    a1, twr, twi, g, gc, p = consts
    ncb = c // cb
    pitch = cb + 8
    w3 = conv_w.reshape(3, c3, 1)
    b3 = conv_b.reshape(c3, 1)

    def zspec(part):
        return pl.BlockSpec((2, cb, l), lambda i, j, part=part: (j, part * ncb + i, 0))

    def wspec(part):
        return pl.BlockSpec((3, cb, 1), lambda i, j, part=part: (0, part * ncb + i, 0))

    def bspec(part):
        return pl.BlockSpec((cb, 1), lambda i, j, part=part: (part * ncb + i, 0))

    return pl.pallas_call(
        functools.partial(_longconv_kernel, cb=cb, nh=nh, n1=n1, group=group, pitch=pitch),
        out_shape=jax.ShapeDtypeStruct((b, c, l), BF16),
        grid=(ncb, b // 2),
        in_specs=[pl.BlockSpec(memory_space=pltpu.SMEM), zspec(0), zspec(1), zspec(2),
                  wspec(0), wspec(1), wspec(2), bspec(0), bspec(1), bspec(2),
                  pl.BlockSpec((cb, n1, 2 * LANES), lambda i, j: (i, 0, 0)),
                  _const_spec(a1.shape), _const_spec(twr.shape), _const_spec(twi.shape),
                  _const_spec(g.shape), _const_spec(gc.shape), _const_spec(p.shape)],
        out_specs=pl.BlockSpec((2, cb, l), lambda i, j: (j, i, 0)),
        scratch_shapes=[pltpu.VMEM((4, nh * pitch, LANES), F32), pltpu.VMEM((2, nh * pitch, LANES), F32)],
        compiler_params=_params("arbitrary", "arbitrary"),
        name="hyena_longconv",
    )(skip, z_t, z_t, z_t, w3, w3, w3, b3, b3, b3, khat, a1, twr, twi, g, gc, p)


def _dft_tables(l):
    n = 2 * l
    n1 = n // LANES
    nh = l // LANES
    f1 = np.exp(-2j * np.pi * np.outer(np.arange(n1), np.arange(n1)) / n1)
    f2 = np.exp(-2j * np.pi * np.outer(np.arange(LANES), np.arange(LANES)) / LANES)
    tw = np.exp(-2j * np.pi * np.outer(np.arange(n1), np.arange(LANES)) / n)
    fh = f1[:, :nh]
    a1 = np.block([[fh.real, -fh.imag], [fh.imag, fh.real]])
    g = np.block([[f2.real, f2.imag], [-f2.imag, f2.real]])
    gc = np.block([[f2.real, -f2.imag], [f2.imag, f2.real]])
    ci = np.conj(f1)[:nh, :]
    p = np.block([[ci.real, -ci.imag], [ci.imag, ci.real]]) / n
    f1full = np.concatenate([f1.real, f1.imag], axis=0)

    def bf(a):
        return jnp.asarray(a, F32).astype(BF16)

    def hilo(a):
        a32 = jnp.asarray(a, F32)
        hi = a32.astype(BF16)
        return hi, (a32 - hi.astype(F32)).astype(BF16)

    data = (bf(a1), jnp.asarray(tw.real, F32), jnp.asarray(tw.imag, F32), bf(g), bf(gc), bf(p))
    filt = hilo(f1full) + (jnp.asarray(tw.real, F32), jnp.asarray(tw.imag, F32)) + hilo(g)
    return data, filt, n1, nh


def _rope_tables_t(l):
    rows = l // GRID_W
    row = np.repeat(np.arange(rows, dtype=np.float64), GRID_W)
    col = np.tile(np.arange(GRID_W, dtype=np.float64), rows)
    inv_freq = ROPE_THETA ** (-np.arange(AXIS_FREQS, dtype=np.float64) / AXIS_FREQS)
    ang = np.concatenate([inv_freq[:, None] * row[None, :], inv_freq[:, None] * col[None, :]], axis=0)
    return jnp.asarray(np.cos(ang), F32), jnp.asarray(np.sin(ang), F32)


def _filter_positions(l):
    t = np.linspace(0.0, 1.0, l)
    w = 2.0 * math.pi * np.arange(l, dtype=np.float64) / l
    bands = np.linspace(1e-4, HY_BANDS - 1, HY_BANDS)[None, :]
    z = np.concatenate([t[:, None], np.cos(bands * w[:, None]), -np.sin(bands * w[:, None])], axis=-1)
    idx = np.concatenate([np.arange(l), np.zeros((1,), np.int64), np.arange(l - 1, 0, -1)])
    return jnp.asarray(z[idx], F32), jnp.asarray(t[idx][:, None], F32)


def kernel(x, c, ctx, c_ctx, ada_w, ada_b, norm1_g, norm2_g, attn_wqkv, attn_wo, attn_q_gain, attn_k_gain, attn_sink, hy_w_in, hy_b_in, hy_conv_w, hy_conv_b, hy_f_w1, hy_f_b1, hy_f_freq1, hy_f_w2, hy_f_b2, hy_f_freq2, hy_f_wout, hy_decay, hy_skip, hy_w_out, hy_b_out, ffn_w1, ffn_w3, ffn_w2):
    b, l, d = x.shape
    tile = min(512, l)
    ff = ffn_w1.shape[-1]
    ff_chunk = ff // 2 if (ff // 2) % LANES == 0 else ff

    pad = (-(b + 1)) % 8
    cond = jnp.concatenate([c, c_ctx[None, :], jnp.zeros((pad, d), F32)], axis=0)
    mod = _ada(cond, ada_w, ada_b)

    def chunks(i, rows):
        m = mod[i, rows][:, None, :]
        return [m[..., k * d:(k + 1) * d] for k in range(6)]

    row = lambda a: a.reshape(1, -1)
    col = lambda a: a.reshape(-1, 1)

    sh1, sc1, g1, sh2, sc2, g2 = chunks(0, slice(0, b))
    csh1, csc1 = [jnp.broadcast_to(m, (b, 1, d)) for m in chunks(0, slice(b, b + 1))[:2]]
    wqkv_t = attn_wqkv[0].T.astype(BF16)
    cos_t, sin_t = _rope_tables_t(l)
    qg, kg = col(attn_q_gain[0]), col(attn_k_gain[0])
    q_t, k_t, v_t = _qkv(x, sh1, sc1, row(norm1_g[0]), wqkv_t, qg, kg, cos_t, sin_t, n_q=N_HEADS, tile=tile)
    kx_t, vx_t = _qkv(ctx, csh1, csc1, row(norm1_g[0]), wqkv_t[N_HEADS * HEAD_DIM:], qg, kg, None, None,
                      n_q=0, tile=ctx.shape[1])
    o_t = _attention(attn_sink[0], q_t, k_t, v_t, kx_t, vx_t)
    x = _proj_ffn(x, o_t, g1, sh2, sc2, g2, attn_wo[0].astype(BF16), jnp.zeros((1, d), F32), row(norm2_g[0]),
                  ffn_w1[0].astype(BF16), ffn_w3[0].astype(BF16), ffn_w2[0].astype(BF16),
                  tile=tile, ff_chunk=ff_chunk)

    sh1, sc1, g1, sh2, sc2, g2 = chunks(1, slice(0, b))
    data_consts, filt_consts, n1, nh = _dft_tables(l)
    zz, t_col = _filter_positions(l)
    taps, norm = _filter_taps(zz, t_col, hy_f_w1[0], hy_f_b1[0], hy_f_freq1[0], hy_f_w2[0], hy_f_b2[0],
                              hy_f_freq2[0], hy_f_wout[0], row(hy_decay[0]), l=l, rows=min(1024, l))
    khat = _filter_fft(taps, col(norm), filt_consts, cb=32, n1=n1, group=4)
    z_t = _inproj(x, sh1, sc1, row(norm1_g[1]), hy_w_in[0].T.astype(BF16), col(hy_b_in[0]), tile=tile)
    gated = _longconv(z_t, khat, hy_conv_w[0], hy_conv_b[0], hy_skip[0], data_consts, cb=32, nh=nh, n1=n1, group=8)
    x = _proj_ffn(x, gated, g1, sh2, sc2, g2, hy_w_out[0].astype(BF16), row(hy_b_out[0]),
                  row(norm2_g[1]), ffn_w1[1].astype(BF16), ffn_w3[1].astype(BF16), ffn_w2[1].astype(BF16),
                  tile=tile, ff_chunk=ff_chunk)
    return x
```

```python
import functools
import math

import numpy as np
import jax
import jax.numpy as jnp
from jax import lax
from jax.experimental import pallas as pl
from jax.experimental.pallas import tpu as pltpu

N_HEADS = 16
N_KV_HEADS = 4
HEAD_DIM = 64
GROUP = N_HEADS // N_KV_HEADS
ROPE_HALF = HEAD_DIM // 2
AXIS_FREQS = ROPE_HALF // 2
BLOCK = 128
GRID_W = 64
ROPE_THETA = 10000.0
ATTN_SCALE = HEAD_DIM ** -0.5
LOG2E = math.log2(math.e)
HY_BANDS = 16
EPS = 1e-6
NEG = -1e30
LANES = 128
VMEM_LIMIT = 56 * 1024 * 1024

F32 = jnp.float32
BF16 = jnp.bfloat16
HI = lax.Precision.HIGHEST
NT = (((1,), (1,)), ((), ()))
TN = (((0,), (0,)), ((), ()))


def _params(*sem):
    return pltpu.CompilerParams(dimension_semantics=sem, vmem_limit_bytes=VMEM_LIMIT)


def _const_spec(shape):
    n = len(shape)
    return pl.BlockSpec(shape, lambda *_: (0,) * n, pipeline_mode=pl.Buffered(1))


def _norm_mod(x, g, sc, sh):
    y = x * lax.rsqrt(jnp.mean(x * x, axis=-1, keepdims=True) + EPS)
    return (y * g) * (1.0 + sc) + sh


def _ada_kernel(cond_ref, w_ref, b_ref, o_ref):
    cnd = cond_ref[...]
    s = (cnd * jax.nn.sigmoid(cnd)).astype(BF16)
    o_ref[0] = jnp.dot(s, w_ref[0].astype(BF16), preferred_element_type=F32) + b_ref[0]


def _ada(cond, ada_w, ada_b):
    depth, d, d6 = ada_w.shape
    r = cond.shape[0]
    cw = 1536
    return pl.pallas_call(
        _ada_kernel,
        out_shape=jax.ShapeDtypeStruct((depth, r, d6), F32),
        grid=(depth, d6 // cw),
        in_specs=[pl.BlockSpec((r, d), lambda i, j: (0, 0)),
                  pl.BlockSpec((1, d, cw), lambda i, j: (i, 0, j)),
                  pl.BlockSpec((1, 1, cw), lambda i, j: (i, 0, j))],
        out_specs=pl.BlockSpec((1, r, cw), lambda i, j: (i, 0, j)),
        compiler_params=_params("arbitrary", "arbitrary"),
        name="ada_mod",
    )(cond, ada_w, ada_b.reshape(depth, 1, d6))


def _head_norm(t, gain, n_heads):
    t3 = t.reshape(n_heads, HEAD_DIM, t.shape[-1])
    ms = jnp.mean(t3 * t3, axis=1, keepdims=True)
    return (t3 * lax.rsqrt(ms + EPS)) * gain[None]


def _rope(t3, cos, sin):
    x1, x2 = t3[:, :ROPE_HALF], t3[:, ROPE_HALF:]
    c, s = cos[None], sin[None]
    return jnp.concatenate([x1 * c - x2 * s, x1 * s + x2 * c], axis=1)


def _qkv_kernel(*refs, n_q, rope):
    if rope:
        x_ref, sh_ref, sc_ref, g_ref, w_ref, qg_ref, kg_ref, cos_ref, sin_ref = refs[:9]
        outs = refs[9:]
    else:
        x_ref, sh_ref, sc_ref, g_ref, w_ref, qg_ref, kg_ref = refs[:7]
        outs = refs[7:]
    h = _norm_mod(x_ref[0], g_ref[...], sc_ref[0], sh_ref[0]).astype(BF16)
    nq = n_q * HEAD_DIM
    nk = N_KV_HEADS * HEAD_DIM
    tt = h.shape[0]

    def proj(r0, r1):
        return lax.dot_general(w_ref[r0:r1, :], h, NT, preferred_element_type=F32)

    q_rows = 8 * HEAD_DIM
    slabs = [(r0, min(r0 + q_rows, nq)) for r0 in range(0, nq, q_rows)] + [(nq, nq + 2 * nk)]
    if n_q:
        q_ref, k_ref, v_ref = outs
    else:
        k_ref, v_ref = outs
    nxt = proj(*slabs[0])
    for i, (r0, r1) in enumerate(slabs):
        t = nxt
        if i + 1 < len(slabs):
            nxt = proj(*slabs[i + 1])
        if r0 < nq:
            q3 = _head_norm(t, qg_ref[...], (r1 - r0) // HEAD_DIM)
            if rope:
                q3 = _rope(q3, cos_ref[...], sin_ref[...])
            q_ref[0, r0:r1, :] = (q3 * (ATTN_SCALE * LOG2E)).reshape(r1 - r0, tt).astype(BF16)
        else:
            k3 = _head_norm(t[:nk], kg_ref[...], N_KV_HEADS)
            if rope:
                k3 = _rope(k3, cos_ref[...], sin_ref[...])
            k_ref[0] = k3.reshape(nk, tt).astype(BF16)
            v_ref[0] = t[nk:].astype(BF16)


def _qkv(x, sh, sc, g, w_t, qg, kg, cos_t, sin_t, *, n_q, tile):
    b, l, d = x.shape
    nq, nk = n_q * HEAD_DIM, N_KV_HEADS * HEAD_DIM
    rope = cos_t is not None
    vec = pl.BlockSpec((1, 1, d), lambda i, j: (i, 0, 0))
    in_specs = [pl.BlockSpec((1, tile, d), lambda i, j: (i, j, 0)), vec, vec,
                _const_spec((1, d)), _const_spec(w_t.shape),
                _const_spec((HEAD_DIM, 1)), _const_spec((HEAD_DIM, 1))]
    args = [x, sh, sc, g, w_t, qg, kg]
    if rope:
        in_specs += [pl.BlockSpec((ROPE_HALF, tile), lambda i, j: (0, j))] * 2
        args += [cos_t, sin_t]
    out_shape, out_specs = [], []
    for rows in ([nq] if n_q else []) + [nk, nk]:
        out_shape.append(jax.ShapeDtypeStruct((b, rows, l), BF16))
        out_specs.append(pl.BlockSpec((1, rows, tile), lambda i, j: (i, 0, j)))
    return pl.pallas_call(
        functools.partial(_qkv_kernel, n_q=n_q, rope=rope),
        out_shape=out_shape, grid=(b, l // tile), in_specs=in_specs, out_specs=out_specs,
        compiler_params=_params("arbitrary", "arbitrary"),
        name="qkv_proj" if n_q else "ctx_kv_proj",
    )(*args)


HEADS_PER_TILE = GROUP


def _scores(q_ref, cols, kt_ref, r0, biases, kh, heads):
    q4 = jnp.concatenate([q_ref[0, h * HEAD_DIM:(h + 1) * HEAD_DIM, cols] for h in heads], axis=1)
    zeros = jnp.zeros_like(q4)
    qz = jnp.concatenate([q4, zeros] if kh % 2 == 0 else [zeros, q4], axis=0)
    lanes = slice((kh // 2) * LANES, (kh // 2 + 1) * LANES)
    s_loc = jnp.dot(kt_ref[r0:r0 + 3 * BLOCK, lanes], qz, preferred_element_type=F32)
    s_ctx = jnp.dot(kt_ref[4 * BLOCK:, lanes], qz, preferred_element_type=F32)
    return [s_loc[:BLOCK] + biases[0], s_loc[BLOCK:2 * BLOCK], s_loc[2 * BLOCK:] + biases[1], s_ctx]


def _softmax_pv(sink_ref, o_ref, cols, vparts, kh, heads, parts):
    rows = slice(kh * HEAD_DIM, (kh + 1) * HEAD_DIM)
    sink = jnp.concatenate([jnp.full((1, BLOCK), sink_ref[h] * LOG2E, F32) for h in heads], axis=1)
    m = sink
    for part in parts:
        m = jnp.maximum(m, jnp.max(part, axis=0, keepdims=True))
    denom = jnp.exp2(sink - m)
    ps = []
    for part in parts:
        p = jnp.exp2(part - m)
        denom = denom + jnp.sum(p, axis=0, keepdims=True)
        ps.append(p.astype(BF16))
    vcat = jnp.concatenate([vp(rows) for vp in vparts], axis=1)
    o4 = jnp.dot(vcat, jnp.concatenate(ps, axis=0), preferred_element_type=F32) / denom
    for g, h in enumerate(heads):
        o_ref[0, h * HEAD_DIM:(h + 1) * HEAD_DIM, cols] = o4[:, g * BLOCK:(g + 1) * BLOCK].astype(BF16)


def _attn_kernel(sink_ref, q_ref, kp_ref, km_ref, kn_ref, vp_ref, vm_ref, vn_ref, kx_ref, vx_ref, o_ref, kt_ref):
    j = pl.program_id(1)
    nj = pl.num_programs(1)
    wq = HEADS_PER_TILE * BLOCK
    key = lax.broadcasted_iota(jnp.int32, (BLOCK, wq), 0)
    qry = lax.broadcasted_iota(jnp.int32, (BLOCK, wq), 1) % BLOCK
    band_prev = jnp.where(key >= qry, 0.0, NEG)
    band_next = jnp.where(key <= qry, 0.0, NEG)
    lo, hi = slice(0, BLOCK), slice(BLOCK, 2 * BLOCK)
    kt_ref[0:BLOCK, :] = kp_ref[0].T
    kt_ref[BLOCK:3 * BLOCK, :] = km_ref[0].T
    kt_ref[3 * BLOCK:4 * BLOCK, :] = kn_ref[0].T
    kt_ref[4 * BLOCK:, :] = kx_ref[0].T
    ctx_v = lambda r: vx_ref[0, r]
    blocks = [
        (lo, 0, [lambda r: vp_ref[0, r], lambda r: vm_ref[0, r, lo], lambda r: vm_ref[0, r, hi], ctx_v],
         [jnp.where(j > 0, band_prev, NEG), band_next]),
        (hi, BLOCK, [lambda r: vm_ref[0, r, lo], lambda r: vm_ref[0, r, hi], lambda r: vn_ref[0, r], ctx_v],
         [band_prev, jnp.where(j < nj - 1, band_next, NEG)]),
    ]
    work = [(blk, kh, [kh * GROUP + t * HEADS_PER_TILE + g for g in range(HEADS_PER_TILE)])
            for blk in blocks for kh in range(N_KV_HEADS) for t in range(GROUP // HEADS_PER_TILE)]
    nxt = _scores(q_ref, work[0][0][0], kt_ref, work[0][0][1], work[0][0][3], work[0][1], work[0][2])
    for i, ((cols, r0, vparts, biases), kh, heads) in enumerate(work):
        parts = nxt
        if i + 1 < len(work):
            (c1, r1, _, b1), kh1, heads1 = work[i + 1]
            nxt = _scores(q_ref, c1, kt_ref, r1, b1, kh1, heads1)
        _softmax_pv(sink_ref, o_ref, cols, vparts, kh, heads, parts)


def _attention(sink, q_t, k_t, v_t, kx_t, vx_t):
    b, dq, l = q_t.shape
    nk = k_t.shape[1]
    lc = kx_t.shape[2]
    nb = l // BLOCK
    prev = pl.BlockSpec((1, nk, BLOCK), lambda i, j: (i, 0, jnp.maximum(2 * j - 1, 0)))
    mid = pl.BlockSpec((1, nk, 2 * BLOCK), lambda i, j: (i, 0, j))
    nxt = pl.BlockSpec((1, nk, BLOCK), lambda i, j: (i, 0, jnp.minimum(2 * j + 2, nb - 1)))
    ctx = pl.BlockSpec((1, nk, lc), lambda i, j: (i, 0, 0))
    return pl.pallas_call(
        _attn_kernel,
        out_shape=jax.ShapeDtypeStruct((b, dq, l), BF16),
        grid=(b, nb // 2),
        in_specs=[pl.BlockSpec(memory_space=pltpu.SMEM),
                  pl.BlockSpec((1, dq, 2 * BLOCK), lambda i, j: (i, 0, j)),
                  prev, mid, nxt, prev, mid, nxt, ctx, ctx],
        out_specs=pl.BlockSpec((1, dq, 2 * BLOCK), lambda i, j: (i, 0, j)),
        scratch_shapes=[pltpu.VMEM((4 * BLOCK + lc, nk), BF16)],
        compiler_params=_params("arbitrary", "arbitrary"),
        name="band_attention",
    )(sink, q_t, k_t, k_t, k_t, v_t, v_t, v_t, kx_t, vx_t)


def _proj_ffn_kernel(x_ref, a_ref, g1_ref, sh2_ref, sc2_ref, g2_ref, wo_ref, bo_ref, n2_ref,
                     w1_ref, w3_ref, w2_ref, o_ref, *, ff_chunk):
    proj = lax.dot_general(a_ref[0], wo_ref[...], TN, preferred_element_type=F32)
    x1 = x_ref[0] + g1_ref[0] * (proj + bo_ref[...])
    h = _norm_mod(x1, n2_ref[...], sc2_ref[0], sh2_ref[0]).astype(BF16)
    def up(c0):
        return (jnp.dot(h, w1_ref[:, c0:c0 + ff_chunk], preferred_element_type=F32),
                jnp.dot(h, w3_ref[:, c0:c0 + ff_chunk], preferred_element_type=F32))

    starts = list(range(0, w1_ref.shape[1], ff_chunk))
    acc, nxt = None, up(starts[0])
    for i, c0 in enumerate(starts):
        a, b = nxt
        if i + 1 < len(starts):
            nxt = up(starts[i + 1])
        act = ((a * jax.nn.sigmoid(a)) * b).astype(BF16)
        part = jnp.dot(act, w2_ref[c0:c0 + ff_chunk, :], preferred_element_type=F32)
        acc = part if acc is None else acc + part
    o_ref[0] = x1 + g2_ref[0] * acc


def _proj_ffn(x, a_t, g1, sh2, sc2, g2, wo, bo, n2, w1, w3, w2, *, tile, ff_chunk):
    b, l, d = x.shape
    c = a_t.shape[1]
    vec = pl.BlockSpec((1, 1, d), lambda i, j: (i, 0, 0))
    return pl.pallas_call(
        functools.partial(_proj_ffn_kernel, ff_chunk=ff_chunk),
        out_shape=jax.ShapeDtypeStruct((b, l, d), F32),
        grid=(b, l // tile),
        in_specs=[pl.BlockSpec((1, tile, d), lambda i, j: (i, j, 0)),
                  pl.BlockSpec((1, c, tile), lambda i, j: (i, 0, j)),
                  vec, vec, vec, vec,
                  _const_spec(wo.shape), _const_spec((1, d)), _const_spec((1, d)),
                  _const_spec(w1.shape), _const_spec(w3.shape), _const_spec(w2.shape)],
        out_specs=pl.BlockSpec((1, tile, d), lambda i, j: (i, j, 0)),
        compiler_params=_params("arbitrary", "arbitrary"),
        name="proj_swiglu",
    )(x, a_t, g1, sh2, sc2, g2, wo, bo, n2, w1, w3, w2)


def _inproj_kernel(x_ref, sh_ref, sc_ref, g_ref, w_ref, b_ref, z_ref):
    h = _norm_mod(x_ref[0], g_ref[...], sc_ref[0], sh_ref[0]).astype(BF16)
    z = lax.dot_general(w_ref[...], h, NT, preferred_element_type=F32) + b_ref[...]
    z_ref[0] = z.astype(BF16)


def _inproj(x, sh, sc, g, w_t, b_col, *, tile):
    b, l, d = x.shape
    c3 = w_t.shape[0]
    vec = pl.BlockSpec((1, 1, d), lambda i, j: (i, 0, 0))
    return pl.pallas_call(
        _inproj_kernel,
        out_shape=jax.ShapeDtypeStruct((b, c3, l), BF16),
        grid=(b, l // tile),
        in_specs=[pl.BlockSpec((1, tile, d), lambda i, j: (i, j, 0)), vec, vec,
                  _const_spec((1, d)), _const_spec(w_t.shape), _const_spec((c3, 1))],
        out_specs=pl.BlockSpec((1, c3, tile), lambda i, j: (i, 0, j)),
        compiler_params=_params("arbitrary", "arbitrary"),
        name="hyena_inproj",
    )(x, sh, sc, g, w_t, b_col)


def _filter_taps_kernel(z_ref, t_ref, w1_ref, b1_ref, f1_ref, w2_ref, b2_ref, f2_ref, wo_ref, dec_ref,
                        k_ref, norm_ref, *, l):
    i = pl.program_id(0)
    dot = functools.partial(jnp.dot, precision=HI, preferred_element_type=F32)
    h = jnp.sin(f1_ref[...] * (dot(z_ref[...], w1_ref[...]) + b1_ref[...]))
    h = jnp.sin(f2_ref[...] * (dot(h, w2_ref[...]) + b2_ref[...]))
    win = jnp.exp(-t_ref[...] * jnp.abs(dec_ref[...]))
    rows = z_ref.shape[0]
    slot = i * rows + lax.broadcasted_iota(jnp.int32, (rows, 1), 0)
    k = jnp.where(slot == l, 0.0, dot(h, wo_ref[...]) * win)

    @pl.when(i == 0)
    def _():
        norm_ref[...] = jnp.zeros_like(norm_ref)

    norm_ref[...] += jnp.sum(jnp.abs(k), axis=0, keepdims=True)
    k_ref[...] = k.T


def _filter_taps(zz, t_col, w1, b1, f1, w2, b2, f2, wout, dec_row, *, l, rows):
    n2, e = zz.shape
    hid = w1.shape[1]
    c = dec_row.shape[1]
    nblk = n2 // rows
    row = lambda a: a.reshape(1, -1)
    return pl.pallas_call(
        functools.partial(_filter_taps_kernel, l=l),
        out_shape=[jax.ShapeDtypeStruct((c, n2), F32), jax.ShapeDtypeStruct((1, c), F32)],
        grid=(nblk,),
        in_specs=[pl.BlockSpec((rows, e), lambda i: (i, 0)), pl.BlockSpec((rows, 1), lambda i: (i, 0)),
                  _const_spec(w1.shape), _const_spec((1, hid)), _const_spec((1, hid)),
                  _const_spec(w2.shape), _const_spec((1, hid)), _const_spec((1, hid)),
                  pl.BlockSpec((hid, c), lambda i: (0, (2 * i) // nblk)), _const_spec((1, c))],
        out_specs=[pl.BlockSpec((c, rows), lambda i: (0, i)), pl.BlockSpec((1, c), lambda i: (0, 0))],
        compiler_params=_params("arbitrary"),
        name="hyena_filter_taps",
    )(zz, t_col, w1, row(b1), row(f1), w2, row(b2), row(f2), wout, dec_row)


def _split(x):
    hi = x.astype(BF16)
    return hi, (x - hi.astype(F32)).astype(BF16)


def _dot3(a_hi, a_lo, b):
    b_hi, b_lo = _split(b)
    d = functools.partial(jnp.dot, preferred_element_type=F32)
    return d(a_hi, b_hi) + (d(a_hi, b_lo) + d(a_lo, b_hi))


def _dot3r(a, b_hi, b_lo):
    a_hi, a_lo = _split(a)
    d = functools.partial(jnp.dot, preferred_element_type=F32)
    return d(a_hi, b_hi) + (d(a_hi, b_lo) + d(a_lo, b_hi))


def _filter_fft_kernel(k_ref, norm_ref, f1h_ref, f1l_ref, twr_ref, twi_ref, gh_ref, gl_ref, o_ref, s_ref,
                       *, cb, n1, pitch, group):
    k = k_ref[...] / norm_ref[...]
    for j in range(n1):
        s_ref[j * pitch:j * pitch + cb, :] = k[:, j * LANES:(j + 1) * LANES]
    tr, ti = twr_ref[...], twi_ref[...]

    def body(gi, carry):
        st = []
        for g in range(group):
            kc = s_ref[pl.ds(gi * group + g, n1, stride=pitch), :]
            ri = _dot3(f1h_ref[...], f1l_ref[...], kc)
            re, im = ri[:n1], ri[n1:]
            st.append(jnp.concatenate([re * tr - im * ti, re * ti + im * tr], axis=1))
        spec = _dot3r(jnp.concatenate(st, axis=0), gh_ref[...], gl_ref[...])
        for g in range(group):
            o_ref[gi * group + g] = spec[g * n1:(g + 1) * n1]
        return carry

    lax.fori_loop(0, cb // group, body, 0)


def _filter_fft(taps, norm_col, consts, *, cb, n1, group):
    c, n2 = taps.shape
    f1h, f1l, twr, twi, gh, gl = consts
    pitch = cb + 8
    return pl.pallas_call(
        functools.partial(_filter_fft_kernel, cb=cb, n1=n1, pitch=pitch, group=group),
        out_shape=jax.ShapeDtypeStruct((c, n1, 2 * LANES), F32),
        grid=(c // cb,),
        in_specs=[pl.BlockSpec((cb, n2), lambda i: (i, 0)), pl.BlockSpec((cb, 1), lambda i: (i, 0)),
                  _const_spec(f1h.shape), _const_spec(f1l.shape), _const_spec(twr.shape), _const_spec(twi.shape),
                  _const_spec(gh.shape), _const_spec(gl.shape)],
        out_specs=pl.BlockSpec((cb, n1, 2 * LANES), lambda i: (i, 0, 0)),
        scratch_shapes=[pltpu.VMEM((n1 * pitch, LANES), F32)],
        compiler_params=_params("arbitrary"),
        name="hyena_filter_fft",
    )(taps, norm_col, f1h, f1l, twr, twi, gh, gl)


def _conv3_tiles(ref, bi, w_ref, b_ref, nh):
    cb = ref.shape[1]
    lane = lax.broadcasted_iota(jnp.int32, (cb, LANES), 1)
    first, last = lane == 0, lane == LANES - 1
    w0, w1, w2 = [jnp.broadcast_to(w_ref[k], (cb, LANES)) for k in range(3)]
    bias = jnp.broadcast_to(b_ref[...], (cb, LANES))
    zero = jnp.zeros((cb, LANES), BF16)

    @functools.lru_cache(maxsize=None)
    def tile(j):
        z = ref[bi, :, j * LANES:(j + 1) * LANES]
        return z, pltpu.roll(z, 1, axis=1), pltpu.roll(z, LANES - 1, axis=1)

    def conv(j):
        z, fwd, bwd = tile(j)
        prev = jnp.where(first, tile(j - 1)[1] if j else zero, fwd)
        nxt = jnp.where(last, tile(j + 1)[2] if j + 1 < nh else zero, bwd)
        return w0 * prev.astype(F32) + w1 * z.astype(F32) + w2 * nxt.astype(F32) + bias

    return conv


def _longconv_kernel(skip_ref, x0_ref, x1_ref, v_ref, w0_ref, w1_ref, wv_ref, b0_ref, b1_ref, bv_ref, kh_ref,
                     a1_ref, twr_ref, twi_ref, g_ref, gc_ref, p_ref, o_ref, s_ref, so_ref,
                     *, cb, nh, n1, group, pitch):
    for bi in range(2):
        x0 = _conv3_tiles(x0_ref, bi, w0_ref, b0_ref, nh)
        x1 = _conv3_tiles(x1_ref, bi, w1_ref, b1_ref, nh)
        v = _conv3_tiles(v_ref, bi, wv_ref, bv_ref, nh)
        for j in range(nh):
            s_ref[bi, j * pitch:j * pitch + cb, :] = v(j) * x1(j)
            s_ref[2 + bi, j * pitch:j * pitch + cb, :] = x0(j)

    c_base = pl.program_id(0) * cb
    tr, ti = twr_ref[...], twi_ref[...]

    def body(gi, carry):
        parts = 2
        sub = group // parts
        chans = [[gi * group + h * sub + k for k in range(sub)] for h in range(parts)]
        us = [[jnp.concatenate([s_ref[0, pl.ds(ci, nh, stride=pitch), :],
                                s_ref[1, pl.ds(ci, nh, stride=pitch), :]], axis=0) for ci in cs]
              for cs in chans]
        first = [[jnp.dot(a1_ref[...], u.astype(BF16), preferred_element_type=F32) for u in uh]
                 for uh in us]
        spec = []
        for h in range(parts):
            st1 = [jnp.concatenate([ri[:n1] * tr - ri[n1:] * ti, ri[:n1] * ti + ri[n1:] * tr], axis=1)
                   for ri in first[h]]
            spec.append(jnp.dot(jnp.concatenate(st1, axis=0).astype(BF16), g_ref[...],
                                preferred_element_type=F32))
        back = []
        for h in range(parts):
            prods = []
            for k, ci in enumerate(chans[h]):
                xs = spec[h][k * n1:(k + 1) * n1]
                xr, xi = xs[:, :LANES], xs[:, LANES:]
                kk = kh_ref[ci]
                kr, ki = kk[:, :LANES], kk[:, LANES:]
                prods.append(jnp.concatenate([xr * kr - xi * ki, xr * ki + xi * kr], axis=1))
            back.append(jnp.dot(jnp.concatenate(prods, axis=0).astype(BF16), gc_ref[...],
                                preferred_element_type=F32))
        for h in range(parts):
            for k, ci in enumerate(chans[h]):
                bs = back[h][k * n1:(k + 1) * n1]
                br, bim = bs[:, :LANES], bs[:, LANES:]
                st = jnp.concatenate([br * tr + bim * ti, bim * tr - br * ti], axis=0)
                y = jnp.dot(p_ref[...], st.astype(BF16), preferred_element_type=F32)
                x0 = jnp.concatenate([s_ref[2, pl.ds(ci, nh, stride=pitch), :],
                                      s_ref[3, pl.ds(ci, nh, stride=pitch), :]], axis=0)
                gated = (y + us[h][k] * skip_ref[c_base + ci]) * x0
                so_ref[0, pl.ds(ci, nh, stride=pitch), :] = gated[:nh]
                so_ref[1, pl.ds(ci, nh, stride=pitch), :] = gated[nh:]
        return carry

    lax.fori_loop(0, cb // group, body, 0)
    for bi in range(2):
        for j in range(nh):
            o_ref[bi, :, j * LANES:(j + 1) * LANES] = so_ref[bi, j * pitch:j * pitch + cb, :].astype(BF16)


def _longconv(z_t, khat, conv_w, conv_b, skip, consts, *, cb, nh, n1, group):
    b, c3, l = z_t.shape
    c = c3 // 3
    a1, twr, twi, g, gc, p = consts
    ncb = c // cb
    pitch = cb + 8
    w3 = conv_w.reshape(3, c3, 1)
    b3 = conv_b.reshape(c3, 1)

    def zspec(part):
        return pl.BlockSpec((2, cb, l), lambda i, j, part=part: (j, part * ncb + i, 0))

    def wspec(part):
        return pl.BlockSpec((3, cb, 1), lambda i, j, part=part: (0, part * ncb + i, 0))

    def bspec(part):
        return pl.BlockSpec((cb, 1), lambda i, j, part=part: (part * ncb + i, 0))

    return pl.pallas_call(
        functools.partial(_longconv_kernel, cb=cb, nh=nh, n1=n1, group=group, pitch=pitch),
        out_shape=jax.ShapeDtypeStruct((b, c, l), BF16),
        grid=(ncb, b // 2),
        in_specs=[pl.BlockSpec(memory_space=pltpu.SMEM), zspec(0), zspec(1), zspec(2),
                  wspec(0), wspec(1), wspec(2), bspec(0), bspec(1), bspec(2),
                  pl.BlockSpec((cb, n1, 2 * LANES), lambda i, j: (i, 0, 0)),
                  _const_spec(a1.shape), _const_spec(twr.shape), _const_spec(twi.shape),
                  _const_spec(g.shape), _const_spec(gc.shape), _const_spec(p.shape)],
        out_specs=pl.BlockSpec((2, cb, l), lambda i, j: (j, i, 0)),
        scratch_shapes=[pltpu.VMEM((4, nh * pitch, LANES), F32), pltpu.VMEM((2, nh * pitch, LANES), F32)],
        compiler_params=_params("arbitrary", "arbitrary"),
        name="hyena_longconv",
    )(skip, z_t, z_t, z_t, w3, w3, w3, b3, b3, b3, khat, a1, twr, twi, g, gc, p)


def _dft_tables(l):
    n = 2 * l
    n1 = n // LANES
    nh = l // LANES
    f1 = np.exp(-2j * np.pi * np.outer(np.arange(n1), np.arange(n1)) / n1)
    f2 = np.exp(-2j * np.pi * np.outer(np.arange(LANES), np.arange(LANES)) / LANES)
    tw = np.exp(-2j * np.pi * np.outer(np.arange(n1), np.arange(LANES)) / n)
    fh = f1[:, :nh]
    a1 = np.block([[fh.real, -fh.imag], [fh.imag, fh.real]])
    g = np.block([[f2.real, f2.imag], [-f2.imag, f2.real]])
    gc = np.block([[f2.real, -f2.imag], [f2.imag, f2.real]])
    ci = np.conj(f1)[:nh, :]
    p = np.block([[ci.real, -ci.imag], [ci.imag, ci.real]]) / n
    f1full = np.concatenate([f1.real, f1.imag], axis=0)

    def bf(a):
        return jnp.asarray(a, F32).astype(BF16)

    def hilo(a):
        a32 = jnp.asarray(a, F32)
        hi = a32.astype(BF16)
        return hi, (a32 - hi.astype(F32)).astype(BF16)

    data = (bf(a1), jnp.asarray(tw.real, F32), jnp.asarray(tw.imag, F32), bf(g), bf(gc), bf(p))
    filt = hilo(f1full) + (jnp.asarray(tw.real, F32), jnp.asarray(tw.imag, F32)) + hilo(g)
    return data, filt, n1, nh


def _rope_tables_t(l):
    rows = l // GRID_W
    row = np.repeat(np.arange(rows, dtype=np.float64), GRID_W)
    col = np.tile(np.arange(GRID_W, dtype=np.float64), rows)
    inv_freq = ROPE_THETA ** (-np.arange(AXIS_FREQS, dtype=np.float64) / AXIS_FREQS)
    ang = np.concatenate([inv_freq[:, None] * row[None, :], inv_freq[:, None] * col[None, :]], axis=0)
    return jnp.asarray(np.cos(ang), F32), jnp.asarray(np.sin(ang), F32)


def _filter_positions(l):
    t = np.linspace(0.0, 1.0, l)
    w = 2.0 * math.pi * np.arange(l, dtype=np.float64) / l
    bands = np.linspace(1e-4, HY_BANDS - 1, HY_BANDS)[None, :]
    z = np.concatenate([t[:, None], np.cos(bands * w[:, None]), -np.sin(bands * w[:, None])], axis=-1)
    idx = np.concatenate([np.arange(l), np.zeros((1,), np.int64), np.arange(l - 1, 0, -1)])
    return jnp.asarray(z[idx], F32), jnp.asarray(t[idx][:, None], F32)


def kernel(x, c, ctx, c_ctx, ada_w, ada_b, norm1_g, norm2_g, attn_wqkv, attn_wo, attn_q_gain, attn_k_gain, attn_sink, hy_w_in, hy_b_in, hy_conv_w, hy_conv_b, hy_f_w1, hy_f_b1, hy_f_freq1, hy_f_w2, hy_f_b2, hy_f_freq2, hy_f_wout, hy_decay, hy_skip, hy_w_out, hy_b_out, ffn_w1, ffn_w3, ffn_w2):
    b, l, d = x.shape
    tile = min(512, l)
    ff = ffn_w1.shape[-1]
    ff_chunk = ff // 2 if (ff // 2) % LANES == 0 else ff

    pad = (-(b + 1)) % 8
    cond = jnp.concatenate([c, c_ctx[None, :], jnp.zeros((pad, d), F32)], axis=0)
    mod = _ada(cond, ada_w, ada_b)

    def chunks(i, rows):
        m = mod[i, rows][:, None, :]
        return [m[..., k * d:(k + 1) * d] for k in range(6)]

    row = lambda a: a.reshape(1, -1)
    col = lambda a: a.reshape(-1, 1)

    sh1, sc1, g1, sh2, sc2, g2 = chunks(0, slice(0, b))
    csh1, csc1 = [jnp.broadcast_to(m, (b, 1, d)) for m in chunks(0, slice(b, b + 1))[:2]]
    wqkv_t = attn_wqkv[0].T.astype(BF16)
    cos_t, sin_t = _rope_tables_t(l)
    qg, kg = col(attn_q_gain[0]), col(attn_k_gain[0])
    q_t, k_t, v_t = _qkv(x, sh1, sc1, row(norm1_g[0]), wqkv_t, qg, kg, cos_t, sin_t, n_q=N_HEADS, tile=tile)
    kx_t, vx_t = _qkv(ctx, csh1, csc1, row(norm1_g[0]), wqkv_t[N_HEADS * HEAD_DIM:], qg, kg, None, None,
                      n_q=0, tile=ctx.shape[1])
    o_t = _attention(attn_sink[0], q_t, k_t, v_t, kx_t, vx_t)
    x = _proj_ffn(x, o_t, g1, sh2, sc2, g2, attn_wo[0].astype(BF16), jnp.zeros((1, d), F32), row(norm2_g[0]),
                  ffn_w1[0].astype(BF16), ffn_w3[0].astype(BF16), ffn_w2[0].astype(BF16),
                  tile=tile, ff_chunk=ff_chunk)

    sh1, sc1, g1, sh2, sc2, g2 = chunks(1, slice(0, b))
    data_consts, filt_consts, n1, nh = _dft_tables(l)
    zz, t_col = _filter_positions(l)
    taps, norm = _filter_taps(zz, t_col, hy_f_w1[0], hy_f_b1[0], hy_f_freq1[0], hy_f_w2[0], hy_f_b2[0],
                              hy_f_freq2[0], hy_f_wout[0], row(hy_decay[0]), l=l, rows=min(1024, l))
    khat = _filter_fft(taps, col(norm), filt_consts, cb=32, n1=n1, group=4)
    z_t = _inproj(x, sh1, sc1, row(norm1_g[1]), hy_w_in[0].T.astype(BF16), col(hy_b_in[0]), tile=tile)
    gated = _longconv(z_t, khat, hy_conv_w[0], hy_conv_b[0], hy_skip[0], data_consts, cb=32, nh=nh, n1=n1, group=8)
    x = _proj_ffn(x, gated, g1, sh2, sc2, g2, hy_w_out[0].astype(BF16), row(hy_b_out[0]),
                  row(norm2_g[1]), ffn_w1[1].astype(BF16), ffn_w3[1].astype(BF16), ffn_w2[1].astype(BF16),
                  tile=tile, ff_chunk=ff_chunk)
    return x
```

```python
import functools
import math

import numpy as np
import jax
import jax.numpy as jnp
from jax import lax
from jax.experimental import pallas as pl
from jax.experimental.pallas import tpu as pltpu

N_HEADS = 16
N_KV_HEADS = 4
HEAD_DIM = 64
GROUP = N_HEADS // N_KV_HEADS
ROPE_HALF = HEAD_DIM // 2
AXIS_FREQS = ROPE_HALF // 2
BLOCK = 128
GRID_W = 64
ROPE_THETA = 10000.0
ATTN_SCALE = HEAD_DIM ** -0.5
LOG2E = math.log2(math.e)
HY_BANDS = 16
EPS = 1e-6
NEG = -1e30
LANES = 128
MXU_TILE = 256
VMEM_LIMIT = 56 * 1024 * 1024

F32 = jnp.float32
BF16 = jnp.bfloat16
HI = lax.Precision.HIGHEST
NT = (((1,), (1,)), ((), ()))
TN = (((0,), (0,)), ((), ()))


def _params(*sem):
    return pltpu.CompilerParams(dimension_semantics=sem, vmem_limit_bytes=VMEM_LIMIT)


def _const_spec(shape):
    n = len(shape)
    return pl.BlockSpec(shape, lambda *_: (0,) * n, pipeline_mode=pl.Buffered(1))


def _norm_mod(x, g, sc, sh):
    y = x * lax.rsqrt(jnp.mean(x * x, axis=-1, keepdims=True) + EPS)
    return (y * g) * (1.0 + sc) + sh


def _ada_kernel(cond_ref, w_ref, b_ref, o_ref):
    cnd = cond_ref[...]
    s = (cnd * jax.nn.sigmoid(cnd)).astype(BF16)
    o_ref[0] = jnp.dot(s, w_ref[0].astype(BF16), preferred_element_type=F32) + b_ref[0]


def _ada(cond, ada_w, ada_b):
    depth, d, d6 = ada_w.shape
    r = cond.shape[0]
    cw = 1536
    return pl.pallas_call(
        _ada_kernel,
        out_shape=jax.ShapeDtypeStruct((depth, r, d6), F32),
        grid=(depth, d6 // cw),
        in_specs=[pl.BlockSpec((r, d), lambda i, j: (0, 0)),
                  pl.BlockSpec((1, d, cw), lambda i, j: (i, 0, j)),
                  pl.BlockSpec((1, 1, cw), lambda i, j: (i, 0, j))],
        out_specs=pl.BlockSpec((1, r, cw), lambda i, j: (i, 0, j)),
        compiler_params=_params("arbitrary", "arbitrary"),
        name="ada_mod",
    )(cond, ada_w, ada_b.reshape(depth, 1, d6))


def _head_norm(t, gain, n_heads):
    t3 = t.reshape(n_heads, HEAD_DIM, t.shape[-1])
    ms = jnp.mean(t3 * t3, axis=1, keepdims=True)
    return (t3 * lax.rsqrt(ms + EPS)) * gain[None]


def _rope(t3, cos, sin):
    x1, x2 = t3[:, :ROPE_HALF], t3[:, ROPE_HALF:]
    c, s = cos[None], sin[None]
    return jnp.concatenate([x1 * c - x2 * s, x1 * s + x2 * c], axis=1)


def _qkv_kernel(*refs, n_q, rope):
    if rope:
        x_ref, sh_ref, sc_ref, g_ref, w_ref, qg_ref, kg_ref, cos_ref, sin_ref = refs[:9]
        outs = refs[9:]
    else:
        x_ref, sh_ref, sc_ref, g_ref, w_ref, qg_ref, kg_ref = refs[:7]
        outs = refs[7:]
    h = _norm_mod(x_ref[0], g_ref[...], sc_ref[0], sh_ref[0]).astype(BF16)
    nq = n_q * HEAD_DIM
    nk = N_KV_HEADS * HEAD_DIM
    tt = h.shape[0]

    def proj(r0, r1):
        return lax.dot_general(w_ref[r0:r1, :], h, NT, preferred_element_type=F32)

    q_rows = 8 * HEAD_DIM
    slabs = [(r0, min(r0 + q_rows, nq)) for r0 in range(0, nq, q_rows)] + [(nq, nq + 2 * nk)]
    if n_q:
        q_ref, k_ref, v_ref = outs
    else:
        k_ref, v_ref = outs
    nxt = proj(*slabs[0])
    for i, (r0, r1) in enumerate(slabs):
        t = nxt
        if i + 1 < len(slabs):
            nxt = proj(*slabs[i + 1])
        if r0 < nq:
            q3 = _head_norm(t, qg_ref[...], (r1 - r0) // HEAD_DIM)
            if rope:
                q3 = _rope(q3, cos_ref[...], sin_ref[...])
            q_ref[0, r0:r1, :] = (q3 * (ATTN_SCALE * LOG2E)).reshape(r1 - r0, tt).astype(BF16)
        else:
            k3 = _head_norm(t[:nk], kg_ref[...], N_KV_HEADS)
            if rope:
                k3 = _rope(k3, cos_ref[...], sin_ref[...])
            k_ref[0] = k3.reshape(nk, tt).astype(BF16)
            v_ref[0] = t[nk:].astype(BF16)


def _qkv(x, sh, sc, g, w_t, qg, kg, cos_t, sin_t, *, n_q, tile):
    b, l, d = x.shape
    nq, nk = n_q * HEAD_DIM, N_KV_HEADS * HEAD_DIM
    rope = cos_t is not None
    vec = pl.BlockSpec((1, 1, d), lambda i, j: (i, 0, 0))
    in_specs = [pl.BlockSpec((1, tile, d), lambda i, j: (i, j, 0)), vec, vec,
                _const_spec((1, d)), _const_spec(w_t.shape),
                _const_spec((HEAD_DIM, 1)), _const_spec((HEAD_DIM, 1))]
    args = [x, sh, sc, g, w_t, qg, kg]
    if rope:
        in_specs += [pl.BlockSpec((ROPE_HALF, tile), lambda i, j: (0, j))] * 2
        args += [cos_t, sin_t]
    out_shape, out_specs = [], []
    for rows in ([nq] if n_q else []) + [nk, nk]:
        out_shape.append(jax.ShapeDtypeStruct((b, rows, l), BF16))
        out_specs.append(pl.BlockSpec((1, rows, tile), lambda i, j: (i, 0, j)))
    return pl.pallas_call(
        functools.partial(_qkv_kernel, n_q=n_q, rope=rope),
        out_shape=out_shape, grid=(b, l // tile), in_specs=in_specs, out_specs=out_specs,
        compiler_params=_params("arbitrary", "arbitrary"),
        name="qkv_proj" if n_q else "ctx_kv_proj",
    )(*args)


HEADS_PER_TILE = GROUP


def _scores(q_ref, cols, kt_ref, r0, biases, kh, heads):
    q4 = jnp.concatenate([q_ref[0, h * HEAD_DIM:(h + 1) * HEAD_DIM, cols] for h in heads], axis=1)
    zeros = jnp.zeros_like(q4)
    qz = jnp.concatenate([q4, zeros] if kh % 2 == 0 else [zeros, q4], axis=0)
    lanes = slice((kh // 2) * LANES, (kh // 2 + 1) * LANES)
    s_loc = jnp.dot(kt_ref[r0:r0 + 3 * BLOCK, lanes], qz, preferred_element_type=F32)
    s_ctx = jnp.dot(kt_ref[4 * BLOCK:, lanes], qz, preferred_element_type=F32)
    return [s_loc[:BLOCK] + biases[0], s_loc[BLOCK:2 * BLOCK], s_loc[2 * BLOCK:] + biases[1], s_ctx]


def _softmax_pv(sink_ref, o_ref, cols, vparts, kh, heads, parts):
    rows = slice(kh * HEAD_DIM, (kh + 1) * HEAD_DIM)
    sink = jnp.concatenate([jnp.full((1, BLOCK), sink_ref[h] * LOG2E, F32) for h in heads], axis=1)
    m = sink
    for part in parts:
        m = jnp.maximum(m, jnp.max(part, axis=0, keepdims=True))
    denom = jnp.exp2(sink - m)
    ps = []
    for part in parts:
        p = jnp.exp2(part - m)
        denom = denom + jnp.sum(p, axis=0, keepdims=True)
        ps.append(p.astype(BF16))
    vcat = jnp.concatenate([vp(rows) for vp in vparts], axis=1)
    o4 = jnp.dot(vcat, jnp.concatenate(ps, axis=0), preferred_element_type=F32) / denom
    for g, h in enumerate(heads):
        o_ref[0, h * HEAD_DIM:(h + 1) * HEAD_DIM, cols] = o4[:, g * BLOCK:(g + 1) * BLOCK].astype(BF16)


def _attn_kernel(sink_ref, q_ref, kp_ref, km_ref, kn_ref, vp_ref, vm_ref, vn_ref, kx_ref, vx_ref, o_ref, kt_ref):
    j = pl.program_id(1)
    nj = pl.num_programs(1)
    wq = HEADS_PER_TILE * BLOCK
    key = lax.broadcasted_iota(jnp.int32, (BLOCK, wq), 0)
    qry = lax.broadcasted_iota(jnp.int32, (BLOCK, wq), 1) % BLOCK
    band_prev = jnp.where(key >= qry, 0.0, NEG)
    band_next = jnp.where(key <= qry, 0.0, NEG)
    lo, hi = slice(0, BLOCK), slice(BLOCK, 2 * BLOCK)
    kt_ref[0:BLOCK, :] = kp_ref[0].T
    kt_ref[BLOCK:3 * BLOCK, :] = km_ref[0].T
    kt_ref[3 * BLOCK:4 * BLOCK, :] = kn_ref[0].T
    kt_ref[4 * BLOCK:, :] = kx_ref[0].T
    ctx_v = lambda r: vx_ref[0, r]
    blocks = [
        (lo, 0, [lambda r: vp_ref[0, r], lambda r: vm_ref[0, r, lo], lambda r: vm_ref[0, r, hi], ctx_v],
         [jnp.where(j > 0, band_prev, NEG), band_next]),
        (hi, BLOCK, [lambda r: vm_ref[0, r, lo], lambda r: vm_ref[0, r, hi], lambda r: vn_ref[0, r], ctx_v],
         [band_prev, jnp.where(j < nj - 1, band_next, NEG)]),
    ]
    work = [(blk, kh, [kh * GROUP + t * HEADS_PER_TILE + g for g in range(HEADS_PER_TILE)])
            for blk in blocks for kh in range(N_KV_HEADS) for t in range(GROUP // HEADS_PER_TILE)]
    nxt = _scores(q_ref, work[0][0][0], kt_ref, work[0][0][1], work[0][0][3], work[0][1], work[0][2])
    for i, ((cols, r0, vparts, biases), kh, heads) in enumerate(work):
        parts = nxt
        if i + 1 < len(work):
            (c1, r1, _, b1), kh1, heads1 = work[i + 1]
            nxt = _scores(q_ref, c1, kt_ref, r1, b1, kh1, heads1)
        _softmax_pv(sink_ref, o_ref, cols, vparts, kh, heads, parts)


def _attention(sink, q_t, k_t, v_t, kx_t, vx_t):
    b, dq, l = q_t.shape
    nk = k_t.shape[1]
    lc = kx_t.shape[2]
    nb = l // BLOCK
    prev = pl.BlockSpec((1, nk, BLOCK), lambda i, j: (i, 0, jnp.maximum(2 * j - 1, 0)))
    mid = pl.BlockSpec((1, nk, 2 * BLOCK), lambda i, j: (i, 0, j))
    nxt = pl.BlockSpec((1, nk, BLOCK), lambda i, j: (i, 0, jnp.minimum(2 * j + 2, nb - 1)))
    ctx = pl.BlockSpec((1, nk, lc), lambda i, j: (i, 0, 0))
    return pl.pallas_call(
        _attn_kernel,
        out_shape=jax.ShapeDtypeStruct((b, dq, l), BF16),
        grid=(b, nb // 2),
        in_specs=[pl.BlockSpec(memory_space=pltpu.SMEM),
                  pl.BlockSpec((1, dq, 2 * BLOCK), lambda i, j: (i, 0, j)),
                  prev, mid, nxt, prev, mid, nxt, ctx, ctx],
        out_specs=pl.BlockSpec((1, dq, 2 * BLOCK), lambda i, j: (i, 0, j)),
        scratch_shapes=[pltpu.VMEM((4 * BLOCK + lc, nk), BF16)],
        compiler_params=_params("arbitrary", "arbitrary"),
        name="band_attention",
    )(sink, q_t, k_t, k_t, k_t, v_t, v_t, v_t, kx_t, vx_t)


def _proj_ffn_kernel(x_ref, a_ref, g1_ref, sh2_ref, sc2_ref, g2_ref, wo_ref, bo_ref, n2_ref,
                     w1_ref, w3_ref, w2_ref, o_ref, *, ff_chunk):
    proj = lax.dot_general(a_ref[0], wo_ref[...], TN, preferred_element_type=F32)
    x1 = x_ref[0] + g1_ref[0] * (proj + bo_ref[...])
    h = _norm_mod(x1, n2_ref[...], sc2_ref[0], sh2_ref[0]).astype(BF16)

    def up(c0, c1):
        return (jnp.dot(h, w1_ref[:, c0:c1], preferred_element_type=F32),
                jnp.dot(h, w3_ref[:, c0:c1], preferred_element_type=F32))

    ff = w1_ref.shape[1]
    spans = [(c0, min(c0 + ff_chunk, ff)) for c0 in range(0, ff, ff_chunk)]
    acc, nxt = None, up(*spans[0])
    for i, (c0, c1) in enumerate(spans):
        a, b = nxt
        if i + 1 < len(spans):
            nxt = up(*spans[i + 1])
        act = ((a * jax.nn.sigmoid(a)) * b).astype(BF16)
        part = jnp.dot(act, w2_ref[c0:c1, :], preferred_element_type=F32)
        acc = part if acc is None else acc + part
    o_ref[0] = x1 + g2_ref[0] * acc


def _proj_ffn(x, a_t, g1, sh2, sc2, g2, wo, bo, n2, w1, w3, w2, *, tile, ff_chunk):
    b, l, d = x.shape
    c = a_t.shape[1]
    vec = pl.BlockSpec((1, 1, d), lambda i, j: (i, 0, 0))
    return pl.pallas_call(
        functools.partial(_proj_ffn_kernel, ff_chunk=ff_chunk),
        out_shape=jax.ShapeDtypeStruct((b, l, d), F32),
        grid=(b, l // tile),
        in_specs=[pl.BlockSpec((1, tile, d), lambda i, j: (i, j, 0)),
                  pl.BlockSpec((1, c, tile), lambda i, j: (i, 0, j)),
                  vec, vec, vec, vec,
                  _const_spec(wo.shape), _const_spec((1, d)), _const_spec((1, d)),
                  _const_spec(w1.shape), _const_spec(w3.shape), _const_spec(w2.shape)],
        out_specs=pl.BlockSpec((1, tile, d), lambda i, j: (i, j, 0)),
        compiler_params=_params("arbitrary", "arbitrary"),
        name="proj_swiglu",
    )(x, a_t, g1, sh2, sc2, g2, wo, bo, n2, w1, w3, w2)


def _inproj_kernel(x_ref, sh_ref, sc_ref, g_ref, w_ref, b_ref, cw_ref, cb_ref, u_ref, x0_ref, zc_ref, lcol_ref,
                   *, slab):
    j = pl.program_id(1)
    nt = pl.num_programs(1) - 1
    d = u_ref.shape[1]
    tt = zc_ref.shape[1]
    nl = tt // LANES

    @pl.when(j == 0)
    def _():
        zc_ref[...] = jnp.zeros_like(zc_ref)
        lcol_ref[...] = jnp.zeros_like(lcol_ref)

    h = _norm_mod(x_ref[0], g_ref[...], sc_ref[0], sh_ref[0]).astype(BF16)
    sub = min(slab, LANES)
    lane = lax.broadcasted_iota(jnp.int32, (sub, LANES), 1)
    first, last = lane == 0, lane == LANES - 1
    has_next = j < nt
    zero = jnp.zeros((sub, LANES), BF16)

    def proj(c0):
        outs = []
        for p in range(3):
            rows = slice(p * d + c0, p * d + c0 + slab)
            z = lax.dot_general(w_ref[rows, :], h, NT, preferred_element_type=F32)
            bias = b_ref[rows, :]
            outs.append(jnp.concatenate([z[:, k * LANES:(k + 1) * LANES] + bias for k in range(nl)],
                                        axis=1).astype(BF16))
        return outs

    def conv_rows(p, r0, znew_first):
        rows = slice(p * d + r0, p * d + r0 + sub)
        w0, w1, w2 = [cw_ref[k, rows, :] for k in range(3)]
        bias = cb_ref[rows, :]
        z = [zc_ref[rows, k * LANES:(k + 1) * LANES] for k in range(nl)]
        right = jnp.where(has_next, znew_first, zero)
        fwd = [pltpu.roll(t, 1, axis=1) for t in [lcol_ref[rows, :]] + z]
        bwd = [pltpu.roll(t, LANES - 1, axis=1) for t in z + [right]]
        tiles = []
        for k in range(nl):
            prev = jnp.where(first, fwd[k], fwd[k + 1])
            nxt = jnp.where(last, bwd[k + 1], bwd[k])
            tiles.append(w0 * prev.astype(F32) + w1 * z[k].astype(F32) + w2 * nxt.astype(F32) + bias)
        lcol_ref[rows, :] = z[nl - 1]
        return tiles

    def finish(c0, znew):
        for s0 in range(0, slab, sub):
            r0 = c0 + s0
            x1 = conv_rows(1, r0, znew[1][s0:s0 + sub, :LANES])
            v = conv_rows(2, r0, znew[2][s0:s0 + sub, :LANES])
            for k in range(nl):
                u_ref[0, r0:r0 + sub, k * LANES:(k + 1) * LANES] = (v[k] * x1[k]).astype(BF16)
            x0 = conv_rows(0, r0, znew[0][s0:s0 + sub, :LANES])
            for k in range(nl):
                x0_ref[0, r0:r0 + sub, k * LANES:(k + 1) * LANES] = x0[k].astype(BF16)
        for p in range(3):
            zc_ref[p * d + c0:p * d + c0 + slab, :] = znew[p]

    starts = list(range(0, d, slab))
    nxt = proj(starts[0])
    for i, c0 in enumerate(starts):
        znew = nxt
        if i + 1 < len(starts):
            nxt = proj(starts[i + 1])
        finish(c0, znew)


def _inproj(x, sh, sc, g, w_t, b_vec, conv_w, conv_b, *, tile, slab):
    b, l, d = x.shape
    c3 = w_t.shape[0]
    c = c3 // 3
    nt = l // tile
    vec = pl.BlockSpec((1, 1, d), lambda i, j: (i, 0, 0))
    out = pl.BlockSpec((1, c, tile), lambda i, j: (i, 0, jnp.maximum(j - 1, 0)))
    wide = lambda a: jnp.broadcast_to(a[..., None], a.shape + (LANES,))
    return pl.pallas_call(
        functools.partial(_inproj_kernel, slab=slab),
        out_shape=[jax.ShapeDtypeStruct((b, c, l), BF16), jax.ShapeDtypeStruct((b, c, l), BF16)],
        grid=(b, nt + 1),
        in_specs=[pl.BlockSpec((1, tile, d), lambda i, j: (i, jnp.minimum(j, nt - 1), 0)), vec, vec,
                  _const_spec((1, d)), _const_spec(w_t.shape), _const_spec((c3, LANES)),
                  _const_spec((3, c3, LANES)), _const_spec((c3, LANES))],
        out_specs=[out, out],
        scratch_shapes=[pltpu.VMEM((c3, tile), BF16), pltpu.VMEM((c3, LANES), BF16)],
        compiler_params=_params("arbitrary", "arbitrary"),
        name="hyena_inproj_conv",
    )(x, sh, sc, g, w_t, wide(b_vec), wide(conv_w), wide(conv_b))


def _filter_taps_kernel(z_ref, t_ref, w1_ref, b1_ref, f1_ref, w2_ref, b2_ref, f2_ref, wo_ref, dec_ref,
                        k_ref, norm_ref, *, l):
    i = pl.program_id(0)
    dot = functools.partial(jnp.dot, precision=HI, preferred_element_type=F32)
    h = jnp.sin(f1_ref[...] * (dot(z_ref[...], w1_ref[...]) + b1_ref[...]))
    h = jnp.sin(f2_ref[...] * (dot(h, w2_ref[...]) + b2_ref[...]))
    win = jnp.exp(-t_ref[...] * jnp.abs(dec_ref[...]))
    rows = z_ref.shape[0]
    slot = i * rows + lax.broadcasted_iota(jnp.int32, (rows, 1), 0)
    k = jnp.where(slot == l, 0.0, dot(h, wo_ref[...]) * win)

    @pl.when(i == 0)
    def _():
        norm_ref[...] = jnp.zeros_like(norm_ref)

    norm_ref[...] += jnp.sum(jnp.abs(k), axis=0, keepdims=True)
    k_ref[...] = k.T


def _filter_taps(zz, t_col, w1, b1, f1, w2, b2, f2, wout, dec_row, *, l, rows):
    n2, e = zz.shape
    hid = w1.shape[1]
    c = dec_row.shape[1]
    nblk = n2 // rows
    row = lambda a: a.reshape(1, -1)
    return pl.pallas_call(
        functools.partial(_filter_taps_kernel, l=l),
        out_shape=[jax.ShapeDtypeStruct((c, n2), F32), jax.ShapeDtypeStruct((1, c), F32)],
        grid=(nblk,),
        in_specs=[pl.BlockSpec((rows, e), lambda i: (i, 0)), pl.BlockSpec((rows, 1), lambda i: (i, 0)),
                  _const_spec(w1.shape), _const_spec((1, hid)), _const_spec((1, hid)),
                  _const_spec(w2.shape), _const_spec((1, hid)), _const_spec((1, hid)),
                  pl.BlockSpec((hid, c), lambda i: (0, (2 * i) // nblk)), _const_spec((1, c))],
        out_specs=[pl.BlockSpec((c, rows), lambda i: (0, i)), pl.BlockSpec((1, c), lambda i: (0, 0))],
        compiler_params=_params("arbitrary"),
        name="hyena_filter_taps",
    )(zz, t_col, w1, row(b1), row(f1), w2, row(b2), row(f2), wout, dec_row)


def _split(x):
    hi = x.astype(BF16)
    return hi, (x - hi.astype(F32)).astype(BF16)


def _dot3(a_hi, a_lo, b):
    b_hi, b_lo = _split(b)
    d = functools.partial(jnp.dot, preferred_element_type=F32)
    return d(a_hi, b_hi) + (d(a_hi, b_lo) + d(a_lo, b_hi))


def _dot3r(a, b_hi, b_lo):
    a_hi, a_lo = _split(a)
    d = functools.partial(jnp.dot, preferred_element_type=F32)
    return d(a_hi, b_hi) + (d(a_hi, b_lo) + d(a_lo, b_hi))


def _filter_fft_kernel(k_ref, norm_ref, f1h_ref, f1l_ref, twr_ref, twi_ref, gh_ref, gl_ref, o_ref, s_ref,
                       *, cb, n1, pitch, group):
    k = k_ref[...] / norm_ref[...]
    for j in range(n1):
        s_ref[j * pitch:j * pitch + cb, :] = k[:, j * LANES:(j + 1) * LANES]
    tr, ti = twr_ref[...], twi_ref[...]

    def body(gi, carry):
        st = []
        for g in range(group):
            kc = s_ref[pl.ds(gi * group + g, n1, stride=pitch), :]
            ri = _dot3(f1h_ref[...], f1l_ref[...], kc)
            re, im = ri[:n1], ri[n1:]
            st.append(jnp.concatenate([re * tr - im * ti, re * ti + im * tr], axis=1))
        spec = _dot3r(jnp.concatenate(st, axis=0), gh_ref[...], gl_ref[...])
        for g in range(group):
            o_ref[gi * group + g] = spec[g * n1:(g + 1) * n1]
        return carry

    lax.fori_loop(0, cb // group, body, 0)


def _filter_fft(taps, norm_col, consts, *, cb, n1, group):
    c, n2 = taps.shape
    f1h, f1l, twr, twi, gh, gl = consts
    pitch = cb + 8
    return pl.pallas_call(
        functools.partial(_filter_fft_kernel, cb=cb, n1=n1, pitch=pitch, group=group),
        out_shape=jax.ShapeDtypeStruct((c, n1, 2 * LANES), F32),
        grid=(c // cb,),
        in_specs=[pl.BlockSpec((cb, n2), lambda i: (i, 0)), pl.BlockSpec((cb, 1), lambda i: (i, 0)),
                  _const_spec(f1h.shape), _const_spec(f1l.shape), _const_spec(twr.shape), _const_spec(twi.shape),
                  _const_spec(gh.shape), _const_spec(gl.shape)],
        out_specs=pl.BlockSpec((cb, n1, 2 * LANES), lambda i: (i, 0, 0)),
        scratch_shapes=[pltpu.VMEM((n1 * pitch, LANES), F32)],
        compiler_params=_params("arbitrary"),
        name="hyena_filter_fft",
    )(taps, norm_col, f1h, f1l, twr, twi, gh, gl)


def _longconv_kernel(skip_ref, u_ref, x0_ref, kh_ref, a1_ref, twr_ref, twi_ref, g_ref, gc_ref, p_ref, o_ref,
                     s_ref, so_ref, *, cb, nh, n1, group, pitch):
    for bi in range(2):
        for j in range(nh):
            s_ref[bi, j * pitch:j * pitch + cb, :] = u_ref[bi, :, j * LANES:(j + 1) * LANES].astype(F32)
            s_ref[2 + bi, j * pitch:j * pitch + cb, :] = x0_ref[bi, :, j * LANES:(j + 1) * LANES].astype(F32)

    c_base = pl.program_id(0) * cb
    tr, ti = twr_ref[...], twi_ref[...]

    def body(gi, carry):
        parts = 2
        sub = group // parts
        chans = [[gi * group + h * sub + k for k in range(sub)] for h in range(parts)]
        us = [[jnp.concatenate([s_ref[0, pl.ds(ci, nh, stride=pitch), :],
                                s_ref[1, pl.ds(ci, nh, stride=pitch), :]], axis=0) for ci in cs]
              for cs in chans]
        first = [[jnp.dot(a1_ref[...], u.astype(BF16), preferred_element_type=F32) for u in uh]
                 for uh in us]
        spec = []
        for h in range(parts):
            st1 = [jnp.concatenate([ri[:n1] * tr - ri[n1:] * ti, ri[:n1] * ti + ri[n1:] * tr], axis=1)
                   for ri in first[h]]
            spec.append(jnp.dot(jnp.concatenate(st1, axis=0).astype(BF16), g_ref[...],
                                preferred_element_type=F32))
        back = []
        for h in range(parts):
            prods = []
            for k, ci in enumerate(chans[h]):
                xs = spec[h][k * n1:(k + 1) * n1]
                xr, xi = xs[:, :LANES], xs[:, LANES:]
                kk = kh_ref[ci]
                kr, ki = kk[:, :LANES], kk[:, LANES:]
                prods.append(jnp.concatenate([xr * kr - xi * ki, xr * ki + xi * kr], axis=1))
            back.append(jnp.dot(jnp.concatenate(prods, axis=0).astype(BF16), gc_ref[...],
                                preferred_element_type=F32))
        for h in range(parts):
            for k, ci in enumerate(chans[h]):
                bs = back[h][k * n1:(k + 1) * n1]
                br, bim = bs[:, :LANES], bs[:, LANES:]
                st = jnp.concatenate([br * tr + bim * ti, bim * tr - br * ti], axis=0)
                y = jnp.dot(p_ref[...], st.astype(BF16), preferred_element_type=F32)
                x0 = jnp.concatenate([s_ref[2, pl.ds(ci, nh, stride=pitch), :],
                                      s_ref[3, pl.ds(ci, nh, stride=pitch), :]], axis=0)
                gated = (y + us[h][k] * skip_ref[c_base + ci]) * x0
                so_ref[0, pl.ds(ci, nh, stride=pitch), :] = gated[:nh]
                so_ref[1, pl.ds(ci, nh, stride=pitch), :] = gated[nh:]
        return carry

    lax.fori_loop(0, cb // group, body, 0)
    for bi in range(2):
        for j in range(nh):
            o_ref[bi, :, j * LANES:(j + 1) * LANES] = so_ref[bi, j * pitch:j * pitch + cb, :].astype(BF16)


def _longconv(u_t, x0_t, khat, skip, consts, *, cb, nh, n1, group):
    b, c, l = u_t.shape
    a1, twr, twi, g, gc, p = consts
    pitch = cb + 8
    blk = pl.BlockSpec((2, cb, l), lambda i, j: (j, i, 0))
    return pl.pallas_call(
        functools.partial(_longconv_kernel, cb=cb, nh=nh, n1=n1, group=group, pitch=pitch),
        out_shape=jax.ShapeDtypeStruct((b, c, l), BF16),
        grid=(c // cb, b // 2),
        in_specs=[pl.BlockSpec(memory_space=pltpu.SMEM), blk, blk,
                  pl.BlockSpec((cb, n1, 2 * LANES), lambda i, j: (i, 0, 0)),
                  _const_spec(a1.shape), _const_spec(twr.shape), _const_spec(twi.shape),
                  _const_spec(g.shape), _const_spec(gc.shape), _const_spec(p.shape)],
        out_specs=blk,
        scratch_shapes=[pltpu.VMEM((4, nh * pitch, LANES), F32), pltpu.VMEM((2, nh * pitch, LANES), F32)],
        compiler_params=_params("arbitrary", "arbitrary"),
        name="hyena_longconv",
    )(skip, u_t, x0_t, khat, a1, twr, twi, g, gc, p)


def _dft_tables(l):
    n = 2 * l
    n1 = n // LANES
    nh = l // LANES
    f1 = np.exp(-2j * np.pi * np.outer(np.arange(n1), np.arange(n1)) / n1)
    f2 = np.exp(-2j * np.pi * np.outer(np.arange(LANES), np.arange(LANES)) / LANES)
    tw = np.exp(-2j * np.pi * np.outer(np.arange(n1), np.arange(LANES)) / n)
    fh = f1[:, :nh]
    a1 = np.block([[fh.real, -fh.imag], [fh.imag, fh.real]])
    g = np.block([[f2.real, f2.imag], [-f2.imag, f2.real]])
    gc = np.block([[f2.real, -f2.imag], [f2.imag, f2.real]])
    ci = np.conj(f1)[:nh, :]
    p = np.block([[ci.real, -ci.imag], [ci.imag, ci.real]]) / n
    f1full = np.concatenate([f1.real, f1.imag], axis=0)

    def bf(a):
        return jnp.asarray(a, F32).astype(BF16)

    def hilo(a):
        a32 = jnp.asarray(a, F32)
        hi = a32.astype(BF16)
        return hi, (a32 - hi.astype(F32)).astype(BF16)

    data = (bf(a1), jnp.asarray(tw.real, F32), jnp.asarray(tw.imag, F32), bf(g), bf(gc), bf(p))
    filt = hilo(f1full) + (jnp.asarray(tw.real, F32), jnp.asarray(tw.imag, F32)) + hilo(g)
    return data, filt, n1, nh


def _rope_tables_t(l):
    rows = l // GRID_W
    row = np.repeat(np.arange(rows, dtype=np.float64), GRID_W)
    col = np.tile(np.arange(GRID_W, dtype=np.float64), rows)
    inv_freq = ROPE_THETA ** (-np.arange(AXIS_FREQS, dtype=np.float64) / AXIS_FREQS)
    ang = np.concatenate([inv_freq[:, None] * row[None, :], inv_freq[:, None] * col[None, :]], axis=0)
    return jnp.asarray(np.cos(ang), F32), jnp.asarray(np.sin(ang), F32)


def _filter_positions(l):
    t = np.linspace(0.0, 1.0, l)
    w = 2.0 * math.pi * np.arange(l, dtype=np.float64) / l
    bands = np.linspace(1e-4, HY_BANDS - 1, HY_BANDS)[None, :]
    z = np.concatenate([t[:, None], np.cos(bands * w[:, None]), -np.sin(bands * w[:, None])], axis=-1)
    idx = np.concatenate([np.arange(l), np.zeros((1,), np.int64), np.arange(l - 1, 0, -1)])
    return jnp.asarray(z[idx], F32), jnp.asarray(t[idx][:, None], F32)


def kernel(x, c, ctx, c_ctx, ada_w, ada_b, norm1_g, norm2_g, attn_wqkv, attn_wo, attn_q_gain, attn_k_gain, attn_sink, hy_w_in, hy_b_in, hy_conv_w, hy_conv_b, hy_f_w1, hy_f_b1, hy_f_freq1, hy_f_w2, hy_f_b2, hy_f_freq2, hy_f_wout, hy_decay, hy_skip, hy_w_out, hy_b_out, ffn_w1, ffn_w3, ffn_w2):
    b, l, d = x.shape
    tile = min(512, l)
    ff = ffn_w1.shape[-1]
    ff_chunk = 3 * MXU_TILE

    pad = (-(b + 1)) % 8
    cond = jnp.concatenate([c, c_ctx[None, :], jnp.zeros((pad, d), F32)], axis=0)
    mod = _ada(cond, ada_w, ada_b)

    def chunks(i, rows):
        m = mod[i, rows][:, None, :]
        return [m[..., k * d:(k + 1) * d] for k in range(6)]

    row = lambda a: a.reshape(1, -1)
    col = lambda a: a.reshape(-1, 1)

    sh1, sc1, g1, sh2, sc2, g2 = chunks(0, slice(0, b))
    csh1, csc1 = [jnp.broadcast_to(m, (b, 1, d)) for m in chunks(0, slice(b, b + 1))[:2]]
    wqkv_t = attn_wqkv[0].T.astype(BF16)
    cos_t, sin_t = _rope_tables_t(l)
    qg, kg = col(attn_q_gain[0]), col(attn_k_gain[0])
    q_t, k_t, v_t = _qkv(x, sh1, sc1, row(norm1_g[0]), wqkv_t, qg, kg, cos_t, sin_t, n_q=N_HEADS, tile=tile)
    kx_t, vx_t = _qkv(ctx, csh1, csc1, row(norm1_g[0]), wqkv_t[N_HEADS * HEAD_DIM:], qg, kg, None, None,
                      n_q=0, tile=ctx.shape[1])
    o_t = _attention(attn_sink[0], q_t, k_t, v_t, kx_t, vx_t)
    x = _proj_ffn(x, o_t, g1, sh2, sc2, g2, attn_wo[0].astype(BF16), jnp.zeros((1, d), F32), row(norm2_g[0]),
                  ffn_w1[0].astype(BF16), ffn_w3[0].astype(BF16), ffn_w2[0].astype(BF16),
                  tile=tile, ff_chunk=ff_chunk)

    sh1, sc1, g1, sh2, sc2, g2 = chunks(1, slice(0, b))
    data_consts, filt_consts, n1, nh = _dft_tables(l)
    zz, t_col = _filter_positions(l)
    taps, norm = _filter_taps(zz, t_col, hy_f_w1[0], hy_f_b1[0], hy_f_freq1[0], hy_f_w2[0], hy_f_b2[0],
                              hy_f_freq2[0], hy_f_wout[0], row(hy_decay[0]), l=l, rows=min(1024, l))
    khat = _filter_fft(taps, col(norm), filt_consts, cb=32, n1=n1, group=4)
    u_t, x0_t = _inproj(x, sh1, sc1, row(norm1_g[1]), hy_w_in[0].T.astype(BF16), hy_b_in[0],
                        hy_conv_w[0], hy_conv_b[0], tile=tile, slab=min(1024, d))
    gated = _longconv(u_t, x0_t, khat, hy_skip[0], data_consts, cb=32, nh=nh, n1=n1, group=8)
    x = _proj_ffn(x, gated, g1, sh2, sc2, g2, hy_w_out[0].astype(BF16), row(hy_b_out[0]),
                  row(norm2_g[1]), ffn_w1[1].astype(BF16), ffn_w3[1].astype(BF16), ffn_w2[1].astype(BF16),
                  tile=tile, ff_chunk=ff_chunk)
    return x
```

```python
import functools
import math

import numpy as np
import jax
import jax.numpy as jnp
from jax import lax
from jax.experimental import pallas as pl
from jax.experimental.pallas import tpu as pltpu

N_HEADS = 16
N_KV_HEADS = 4
HEAD_DIM = 64
GROUP = N_HEADS // N_KV_HEADS
ROPE_HALF = HEAD_DIM // 2
AXIS_FREQS = ROPE_HALF // 2
BLOCK = 128
GRID_W = 64
ROPE_THETA = 10000.0
ATTN_SCALE = HEAD_DIM ** -0.5
LOG2E = math.log2(math.e)
HY_BANDS = 16
EPS = 1e-6
NEG = -1e30
LANES = 128
MXU_TILE = 256
VMEM_LIMIT = 56 * 1024 * 1024

F32 = jnp.float32
BF16 = jnp.bfloat16
NT = (((1,), (1,)), ((), ()))
TN = (((0,), (0,)), ((), ()))


def _params(*sem):
    return pltpu.CompilerParams(dimension_semantics=sem, vmem_limit_bytes=VMEM_LIMIT)


def _const_spec(shape):
    n = len(shape)
    return pl.BlockSpec(shape, lambda *_: (0,) * n, pipeline_mode=pl.Buffered(1))


def _norm_mod(x, g, sc, sh):
    y = x * lax.rsqrt(jnp.mean(x * x, axis=-1, keepdims=True) + EPS)
    return (y * g) * (1.0 + sc) + sh


def _ada_kernel(cond_ref, w_ref, b_ref, o_ref):
    cnd = cond_ref[...]
    s = (cnd * jax.nn.sigmoid(cnd)).astype(BF16)
    o_ref[0] = jnp.dot(s, w_ref[0].astype(BF16), preferred_element_type=F32) + b_ref[0]


def _ada(cond, ada_w, ada_b):
    depth, d, d6 = ada_w.shape
    r = cond.shape[0]
    cw = 1536
    return pl.pallas_call(
        _ada_kernel,
        out_shape=jax.ShapeDtypeStruct((depth, r, d6), F32),
        grid=(depth, d6 // cw),
        in_specs=[pl.BlockSpec((r, d), lambda i, j: (0, 0)),
                  pl.BlockSpec((1, d, cw), lambda i, j: (i, 0, j)),
                  pl.BlockSpec((1, 1, cw), lambda i, j: (i, 0, j))],
        out_specs=pl.BlockSpec((1, r, cw), lambda i, j: (i, 0, j)),
        compiler_params=_params("arbitrary", "arbitrary"),
        name="ada_mod",
    )(cond, ada_w, ada_b.reshape(depth, 1, d6))


def _head_norm(t, gain, n_heads):
    t3 = t.reshape(n_heads, HEAD_DIM, t.shape[-1])
    ms = jnp.mean(t3 * t3, axis=1, keepdims=True)
    return (t3 * lax.rsqrt(ms + EPS)) * gain[None]


def _rope(t3, cos, sin):
    x1, x2 = t3[:, :ROPE_HALF], t3[:, ROPE_HALF:]
    c, s = cos[None], sin[None]
    return jnp.concatenate([x1 * c - x2 * s, x1 * s + x2 * c], axis=1)


def _qkv_kernel(*refs, n_q, rope):
    if rope:
        x_ref, sh_ref, sc_ref, g_ref, w_ref, qg_ref, kg_ref, cos_ref, sin_ref = refs[:9]
        outs = refs[9:]
    else:
        x_ref, sh_ref, sc_ref, g_ref, w_ref, qg_ref, kg_ref = refs[:7]
        outs = refs[7:]
    h = _norm_mod(x_ref[0], g_ref[...], sc_ref[0], sh_ref[0]).astype(BF16)
    nq = n_q * HEAD_DIM
    nk = N_KV_HEADS * HEAD_DIM
    tt = h.shape[0]

    def proj(r0, r1):
        return lax.dot_general(w_ref[r0:r1, :], h, NT, preferred_element_type=F32)

    q_rows = 8 * HEAD_DIM
    slabs = [(r0, min(r0 + q_rows, nq)) for r0 in range(0, nq, q_rows)] + [(nq, nq + 2 * nk)]
    if n_q:
        q_ref, k_ref, v_ref = outs
    else:
        k_ref, v_ref = outs
    nxt = proj(*slabs[0])
    for i, (r0, r1) in enumerate(slabs):
        t = nxt
        if i + 1 < len(slabs):
            nxt = proj(*slabs[i + 1])
        if r0 < nq:
            q3 = _head_norm(t, qg_ref[...], (r1 - r0) // HEAD_DIM)
            if rope:
                q3 = _rope(q3, cos_ref[...], sin_ref[...])
            q_ref[0, r0:r1, :] = (q3 * (ATTN_SCALE * LOG2E)).reshape(r1 - r0, tt).astype(BF16)
        else:
            k3 = _head_norm(t[:nk], kg_ref[...], N_KV_HEADS)
            if rope:
                k3 = _rope(k3, cos_ref[...], sin_ref[...])
            k_ref[0] = k3.reshape(nk, tt).astype(BF16)
            v_ref[0] = t[nk:].astype(BF16)


def _qkv(x, sh, sc, g, w_t, qg, kg, cos_t, sin_t, *, n_q, tile):
    b, l, d = x.shape
    nq, nk = n_q * HEAD_DIM, N_KV_HEADS * HEAD_DIM
    rope = cos_t is not None
    vec = pl.BlockSpec((1, 1, d), lambda i, j: (i, 0, 0))
    in_specs = [pl.BlockSpec((1, tile, d), lambda i, j: (i, j, 0)), vec, vec,
                _const_spec((1, d)), _const_spec(w_t.shape),
                _const_spec((HEAD_DIM, 1)), _const_spec((HEAD_DIM, 1))]
    args = [x, sh, sc, g, w_t, qg, kg]
    if rope:
        in_specs += [pl.BlockSpec((ROPE_HALF, tile), lambda i, j: (0, j))] * 2
        args += [cos_t, sin_t]
    out_shape, out_specs = [], []
    for rows in ([nq] if n_q else []) + [nk, nk]:
        out_shape.append(jax.ShapeDtypeStruct((b, rows, l), BF16))
        out_specs.append(pl.BlockSpec((1, rows, tile), lambda i, j: (i, 0, j)))
    return pl.pallas_call(
        functools.partial(_qkv_kernel, n_q=n_q, rope=rope),
        out_shape=out_shape, grid=(b, l // tile), in_specs=in_specs, out_specs=out_specs,
        compiler_params=_params("arbitrary", "arbitrary"),
        name="qkv_proj" if n_q else "ctx_kv_proj",
    )(*args)


HEADS_PER_TILE = GROUP


def _scores(q_ref, cols, kt_ref, r0, biases, kh, heads):
    q4 = jnp.concatenate([q_ref[0, h * HEAD_DIM:(h + 1) * HEAD_DIM, cols] for h in heads], axis=1)
    zeros = jnp.zeros_like(q4)
    qz = jnp.concatenate([q4, zeros] if kh % 2 == 0 else [zeros, q4], axis=0)
    lanes = slice((kh // 2) * LANES, (kh // 2 + 1) * LANES)
    s_loc = jnp.dot(kt_ref[r0:r0 + 3 * BLOCK, lanes], qz, preferred_element_type=F32)
    s_ctx = jnp.dot(kt_ref[4 * BLOCK:, lanes], qz, preferred_element_type=F32)
    return [s_loc[:BLOCK] + biases[0], s_loc[BLOCK:2 * BLOCK], s_loc[2 * BLOCK:] + biases[1], s_ctx]


def _softmax_pv(sink_ref, o_ref, cols, vparts, kh, heads, parts):
    rows = slice(kh * HEAD_DIM, (kh + 1) * HEAD_DIM)
    sink = jnp.concatenate([jnp.full((1, BLOCK), sink_ref[h] * LOG2E, F32) for h in heads], axis=1)
    m = sink
    for part in parts:
        m = jnp.maximum(m, jnp.max(part, axis=0, keepdims=True))
    denom = jnp.exp2(sink - m)
    ps = []
    for part in parts:
        p = jnp.exp2(part - m)
        denom = denom + jnp.sum(p, axis=0, keepdims=True)
        ps.append(p.astype(BF16))
    vcat = jnp.concatenate([vp(rows) for vp in vparts], axis=1)
    o4 = jnp.dot(vcat, jnp.concatenate(ps, axis=0), preferred_element_type=F32) / denom
    for g, h in enumerate(heads):
        o_ref[0, h * HEAD_DIM:(h + 1) * HEAD_DIM, cols] = o4[:, g * BLOCK:(g + 1) * BLOCK].astype(BF16)


def _attn_kernel(sink_ref, q_ref, kp_ref, km_ref, kn_ref, vp_ref, vm_ref, vn_ref, kx_ref, vx_ref, o_ref, kt_ref):
    j = pl.program_id(1)
    nj = pl.num_programs(1)
    wq = HEADS_PER_TILE * BLOCK
    key = lax.broadcasted_iota(jnp.int32, (BLOCK, wq), 0)
    qry = lax.broadcasted_iota(jnp.int32, (BLOCK, wq), 1) % BLOCK
    band_prev = jnp.where(key >= qry, 0.0, NEG)
    band_next = jnp.where(key <= qry, 0.0, NEG)
    lo, hi = slice(0, BLOCK), slice(BLOCK, 2 * BLOCK)
    kt_ref[0:BLOCK, :] = kp_ref[0].T
    kt_ref[BLOCK:3 * BLOCK, :] = km_ref[0].T
    kt_ref[3 * BLOCK:4 * BLOCK, :] = kn_ref[0].T
    kt_ref[4 * BLOCK:, :] = kx_ref[0].T
    ctx_v = lambda r: vx_ref[0, r]
    blocks = [
        (lo, 0, [lambda r: vp_ref[0, r], lambda r: vm_ref[0, r, lo], lambda r: vm_ref[0, r, hi], ctx_v],
         [jnp.where(j > 0, band_prev, NEG), band_next]),
        (hi, BLOCK, [lambda r: vm_ref[0, r, lo], lambda r: vm_ref[0, r, hi], lambda r: vn_ref[0, r], ctx_v],
         [band_prev, jnp.where(j < nj - 1, band_next, NEG)]),
    ]
    work = [(blk, kh, [kh * GROUP + t * HEADS_PER_TILE + g for g in range(HEADS_PER_TILE)])
            for blk in blocks for kh in range(N_KV_HEADS) for t in range(GROUP // HEADS_PER_TILE)]
    nxt = _scores(q_ref, work[0][0][0], kt_ref, work[0][0][1], work[0][0][3], work[0][1], work[0][2])
    for i, ((cols, r0, vparts, biases), kh, heads) in enumerate(work):
        parts = nxt
        if i + 1 < len(work):
            (c1, r1, _, b1), kh1, heads1 = work[i + 1]
            nxt = _scores(q_ref, c1, kt_ref, r1, b1, kh1, heads1)
        _softmax_pv(sink_ref, o_ref, cols, vparts, kh, heads, parts)


def _attention(sink, q_t, k_t, v_t, kx_t, vx_t):
    b, dq, l = q_t.shape
    nk = k_t.shape[1]
    lc = kx_t.shape[2]
    nb = l // BLOCK
    prev = pl.BlockSpec((1, nk, BLOCK), lambda i, j: (i, 0, jnp.maximum(2 * j - 1, 0)))
    mid = pl.BlockSpec((1, nk, 2 * BLOCK), lambda i, j: (i, 0, j))
    nxt = pl.BlockSpec((1, nk, BLOCK), lambda i, j: (i, 0, jnp.minimum(2 * j + 2, nb - 1)))
    ctx = pl.BlockSpec((1, nk, lc), lambda i, j: (i, 0, 0))
    return pl.pallas_call(
        _attn_kernel,
        out_shape=jax.ShapeDtypeStruct((b, dq, l), BF16),
        grid=(b, nb // 2),
        in_specs=[pl.BlockSpec(memory_space=pltpu.SMEM),
                  pl.BlockSpec((1, dq, 2 * BLOCK), lambda i, j: (i, 0, j)),
                  prev, mid, nxt, prev, mid, nxt, ctx, ctx],
        out_specs=pl.BlockSpec((1, dq, 2 * BLOCK), lambda i, j: (i, 0, j)),
        scratch_shapes=[pltpu.VMEM((4 * BLOCK + lc, nk), BF16)],
        compiler_params=_params("arbitrary", "arbitrary"),
        name="band_attention",
    )(sink, q_t, k_t, k_t, k_t, v_t, v_t, v_t, kx_t, vx_t)


def _proj_ffn_kernel(x_ref, a_ref, g1_ref, sh2_ref, sc2_ref, g2_ref, wo_ref, bo_ref, n2_ref,
                     w1_ref, w3_ref, w2_ref, o_ref, *, ff_chunk):
    proj = lax.dot_general(a_ref[0], wo_ref[...], TN, preferred_element_type=F32)
    x1 = x_ref[0] + g1_ref[0] * (proj + bo_ref[...])
    h = _norm_mod(x1, n2_ref[...], sc2_ref[0], sh2_ref[0]).astype(BF16)

    def up(c0, c1):
        return (jnp.dot(h, w1_ref[:, c0:c1], preferred_element_type=F32),
                jnp.dot(h, w3_ref[:, c0:c1], preferred_element_type=F32))

    ff = w1_ref.shape[1]
    spans = [(c0, min(c0 + ff_chunk, ff)) for c0 in range(0, ff, ff_chunk)]
    acc, nxt = None, up(*spans[0])
    for i, (c0, c1) in enumerate(spans):
        a, b = nxt
        if i + 1 < len(spans):
            nxt = up(*spans[i + 1])
        act = ((a * jax.nn.sigmoid(a)) * b).astype(BF16)
        part = jnp.dot(act, w2_ref[c0:c1, :], preferred_element_type=F32)
        acc = part if acc is None else acc + part
    o_ref[0] = x1 + g2_ref[0] * acc


def _proj_ffn(x, a_t, g1, sh2, sc2, g2, wo, bo, n2, w1, w3, w2, *, tile, ff_chunk):
    b, l, d = x.shape
    c = a_t.shape[1]
    vec = pl.BlockSpec((1, 1, d), lambda i, j: (i, 0, 0))
    return pl.pallas_call(
        functools.partial(_proj_ffn_kernel, ff_chunk=ff_chunk),
        out_shape=jax.ShapeDtypeStruct((b, l, d), F32),
        grid=(b, l // tile),
        in_specs=[pl.BlockSpec((1, tile, d), lambda i, j: (i, j, 0)),
                  pl.BlockSpec((1, c, tile), lambda i, j: (i, 0, j)),
                  vec, vec, vec, vec,
                  _const_spec(wo.shape), _const_spec((1, d)), _const_spec((1, d)),
                  _const_spec(w1.shape), _const_spec(w3.shape), _const_spec(w2.shape)],
        out_specs=pl.BlockSpec((1, tile, d), lambda i, j: (i, j, 0)),
        compiler_params=_params("arbitrary", "arbitrary"),
        name="proj_swiglu",
    )(x, a_t, g1, sh2, sc2, g2, wo, bo, n2, w1, w3, w2)


def _inproj_kernel(x_ref, sh_ref, sc_ref, g_ref, w_ref, b_ref, cw_ref, cb_ref, u_ref, x0_ref, zc_ref, lcol_ref,
                   *, slab):
    j = pl.program_id(1)
    nt = pl.num_programs(1) - 1
    d = u_ref.shape[1]
    tt = zc_ref.shape[1]
    nl = tt // LANES

    @pl.when(j == 0)
    def _():
        zc_ref[...] = jnp.zeros_like(zc_ref)
        lcol_ref[...] = jnp.zeros_like(lcol_ref)

    h = _norm_mod(x_ref[0], g_ref[...], sc_ref[0], sh_ref[0]).astype(BF16)
    sub = min(slab, LANES)
    lane = lax.broadcasted_iota(jnp.int32, (sub, LANES), 1)
    first, last = lane == 0, lane == LANES - 1
    has_next = j < nt
    zero = jnp.zeros((sub, LANES), BF16)

    def proj(c0):
        outs = []
        for p in range(3):
            rows = slice(p * d + c0, p * d + c0 + slab)
            z = lax.dot_general(w_ref[rows, :], h, NT, preferred_element_type=F32)
            bias = b_ref[rows, :]
            outs.append(jnp.concatenate([z[:, k * LANES:(k + 1) * LANES] + bias for k in range(nl)],
                                        axis=1).astype(BF16))
        return outs

    def conv_rows(p, r0, znew_first):
        rows = slice(p * d + r0, p * d + r0 + sub)
        w0, w1, w2 = [cw_ref[k, rows, :] for k in range(3)]
        bias = cb_ref[rows, :]
        z = [zc_ref[rows, k * LANES:(k + 1) * LANES] for k in range(nl)]
        right = jnp.where(has_next, znew_first, zero)
        fwd = [pltpu.roll(t, 1, axis=1) for t in [lcol_ref[rows, :]] + z]
        bwd = [pltpu.roll(t, LANES - 1, axis=1) for t in z + [right]]
        tiles = []
        for k in range(nl):
            prev = jnp.where(first, fwd[k], fwd[k + 1])
            nxt = jnp.where(last, bwd[k + 1], bwd[k])
            tiles.append(w0 * prev.astype(F32) + w1 * z[k].astype(F32) + w2 * nxt.astype(F32) + bias)
        lcol_ref[rows, :] = z[nl - 1]
        return tiles

    def finish(c0, znew):
        for s0 in range(0, slab, sub):
            r0 = c0 + s0
            x1 = conv_rows(1, r0, znew[1][s0:s0 + sub, :LANES])
            v = conv_rows(2, r0, znew[2][s0:s0 + sub, :LANES])
            for k in range(nl):
                u_ref[0, r0:r0 + sub, k * LANES:(k + 1) * LANES] = (v[k] * x1[k]).astype(BF16)
            x0 = conv_rows(0, r0, znew[0][s0:s0 + sub, :LANES])
            for k in range(nl):
                x0_ref[0, r0:r0 + sub, k * LANES:(k + 1) * LANES] = x0[k].astype(BF16)
        for p in range(3):
            zc_ref[p * d + c0:p * d + c0 + slab, :] = znew[p]

    starts = list(range(0, d, slab))
    nxt = proj(starts[0])
    for i, c0 in enumerate(starts):
        znew = nxt
        if i + 1 < len(starts):
            nxt = proj(starts[i + 1])
        finish(c0, znew)


def _inproj(x, sh, sc, g, w_t, b_vec, conv_w, conv_b, *, tile, slab):
    b, l, d = x.shape
    c3 = w_t.shape[0]
    c = c3 // 3
    nt = l // tile
    vec = pl.BlockSpec((1, 1, d), lambda i, j: (i, 0, 0))
    out = pl.BlockSpec((1, c, tile), lambda i, j: (i, 0, jnp.maximum(j - 1, 0)))
    wide = lambda a: jnp.broadcast_to(a[..., None], a.shape + (LANES,))
    return pl.pallas_call(
        functools.partial(_inproj_kernel, slab=slab),
        out_shape=[jax.ShapeDtypeStruct((b, c, l), BF16), jax.ShapeDtypeStruct((b, c, l), BF16)],
        grid=(b, nt + 1),
        in_specs=[pl.BlockSpec((1, tile, d), lambda i, j: (i, jnp.minimum(j, nt - 1), 0)), vec, vec,
                  _const_spec((1, d)), _const_spec(w_t.shape), _const_spec((c3, LANES)),
                  _const_spec((3, c3, LANES)), _const_spec((c3, LANES))],
        out_specs=[out, out],
        scratch_shapes=[pltpu.VMEM((c3, tile), BF16), pltpu.VMEM((c3, LANES), BF16)],
        compiler_params=_params("arbitrary", "arbitrary"),
        name="hyena_inproj_conv",
    )(x, sh, sc, g, w_t, wide(b_vec), wide(conv_w), wide(conv_b))


def _split(x):
    hi = x.astype(BF16)
    return hi, (x - hi.astype(F32)).astype(BF16)


def _dot3(a, b):
    a_hi, a_lo = _split(a)
    b_hi, b_lo = _split(b)
    d = functools.partial(jnp.dot, preferred_element_type=F32)
    return d(a_hi, b_hi) + (d(a_hi, b_lo) + d(a_lo, b_hi))


def _filter_taps_kernel(z_ref, t_ref, w1_ref, b1_ref, f1_ref, w2_ref, b2_ref, f2_ref, wo_ref, dec_ref,
                        k_ref, norm_ref, *, l):
    i = pl.program_id(0)
    h = jnp.sin(f1_ref[...] * (_dot3(w1_ref[...], z_ref[...]) + b1_ref[...]))
    h = jnp.sin(f2_ref[...] * (_dot3(w2_ref[...], h) + b2_ref[...]))
    win = jnp.exp(-t_ref[...] * jnp.abs(dec_ref[...]))
    n = z_ref.shape[1]
    slot = i * n + lax.broadcasted_iota(jnp.int32, (1, n), 1)
    k = jnp.where(slot == l, 0.0, _dot3(wo_ref[...], h) * win)

    @pl.when(i == 0)
    def _():
        norm_ref[...] = jnp.zeros_like(norm_ref)

    part = jnp.abs(k[:, :LANES])
    for j in range(1, n // LANES):
        part = part + jnp.abs(k[:, j * LANES:(j + 1) * LANES])
    norm_ref[...] += part
    k_ref[...] = k


def _filter_taps(zz_t, t_row, w1, b1, f1, w2, b2, f2, wout, dec, *, l, cols):
    e, n2 = zz_t.shape
    hid = w1.shape[1]
    c = dec.shape[0]
    nblk = n2 // cols
    col = lambda a: a.reshape(-1, 1)
    return pl.pallas_call(
        functools.partial(_filter_taps_kernel, l=l),
        out_shape=[jax.ShapeDtypeStruct((c, n2), F32), jax.ShapeDtypeStruct((c, LANES), F32)],
        grid=(nblk,),
        in_specs=[pl.BlockSpec((e, cols), lambda i: (0, i)), pl.BlockSpec((1, cols), lambda i: (0, i)),
                  _const_spec((hid, e)), _const_spec((hid, 1)), _const_spec((hid, 1)),
                  _const_spec((hid, hid)), _const_spec((hid, 1)), _const_spec((hid, 1)),
                  pl.BlockSpec((c, hid), lambda i: ((2 * i) // nblk, 0)), _const_spec((c, 1))],
        out_specs=[pl.BlockSpec((c, cols), lambda i: (0, i)), pl.BlockSpec((c, LANES), lambda i: (0, 0))],
        compiler_params=_params("arbitrary"),
        name="hyena_filter_taps",
    )(zz_t, t_row, w1.T, col(b1), col(f1), w2.T, col(b2), col(f2), wout.T, col(dec))


def _filter_fft_kernel(k_ref, norm_ref, f1_ref, twr_ref, twi_ref, g_ref, o_ref, s_ref, *, cb, n1, pitch, group):
    k = k_ref[...] / jnp.sum(norm_ref[...], axis=1, keepdims=True)
    for j in range(n1):
        s_ref[j * pitch:j * pitch + cb, :] = k[:, j * LANES:(j + 1) * LANES]
    tr, ti = twr_ref[...], twi_ref[...]

    def body(gi, carry):
        st = []
        for g in range(group):
            kc = s_ref[pl.ds(gi * group + g, n1, stride=pitch), :]
            ri = jnp.dot(f1_ref[...], kc.astype(BF16), preferred_element_type=F32)
            re, im = ri[:n1], ri[n1:]
            st.append(jnp.concatenate([re * tr - im * ti, re * ti + im * tr], axis=1))
        spec = jnp.dot(jnp.concatenate(st, axis=0).astype(BF16), g_ref[...], preferred_element_type=F32)
        for g in range(group):
            o_ref[gi * group + g] = spec[g * n1:(g + 1) * n1]
        return carry

    lax.fori_loop(0, cb // group, body, 0)


def _filter_fft(taps, norm, consts, *, cb, n1, group):
    c, n2 = taps.shape
    f1, twr, twi, g = consts
    pitch = cb + 8
    return pl.pallas_call(
        functools.partial(_filter_fft_kernel, cb=cb, n1=n1, pitch=pitch, group=group),
        out_shape=jax.ShapeDtypeStruct((c, n1, 2 * LANES), F32),
        grid=(c // cb,),
        in_specs=[pl.BlockSpec((cb, n2), lambda i: (i, 0)), pl.BlockSpec((cb, LANES), lambda i: (i, 0)),
                  _const_spec(f1.shape), _const_spec(twr.shape), _const_spec(twi.shape), _const_spec(g.shape)],
        out_specs=pl.BlockSpec((cb, n1, 2 * LANES), lambda i: (i, 0, 0)),
        scratch_shapes=[pltpu.VMEM((n1 * pitch, LANES), F32)],
        compiler_params=_params("arbitrary"),
        name="hyena_filter_fft",
    )(taps, norm, f1, twr, twi, g)


def _longconv_kernel(skip_ref, u_ref, x0_ref, kh_ref, a1_ref, twr_ref, twi_ref, g_ref, gc_ref, p_ref, o_ref,
                     s_ref, so_ref, *, cb, nh, n1, group, pitch):
    for bi in range(2):
        for j in range(nh):
            s_ref[bi, j * pitch:j * pitch + cb, :] = u_ref[bi, :, j * LANES:(j + 1) * LANES].astype(F32)
            s_ref[2 + bi, j * pitch:j * pitch + cb, :] = x0_ref[bi, :, j * LANES:(j + 1) * LANES].astype(F32)

    c_base = pl.program_id(0) * cb
    tr, ti = twr_ref[...], twi_ref[...]

    def body(gi, carry):
        parts = 2
        sub = group // parts
        chans = [[gi * group + h * sub + k for k in range(sub)] for h in range(parts)]
        us = [[jnp.concatenate([s_ref[0, pl.ds(ci, nh, stride=pitch), :],
                                s_ref[1, pl.ds(ci, nh, stride=pitch), :]], axis=0) for ci in cs]
              for cs in chans]
        first = [[jnp.dot(a1_ref[...], u.astype(BF16), preferred_element_type=F32) for u in uh]
                 for uh in us]
        spec = []
        for h in range(parts):
            st1 = [jnp.concatenate([ri[:n1] * tr - ri[n1:] * ti, ri[:n1] * ti + ri[n1:] * tr], axis=1)
                   for ri in first[h]]
            spec.append(jnp.dot(jnp.concatenate(st1, axis=0).astype(BF16), g_ref[...],
                                preferred_element_type=F32))
        back = []
        for h in range(parts):
            prods = []
            for k, ci in enumerate(chans[h]):
                xs = spec[h][k * n1:(k + 1) * n1]
                xr, xi = xs[:, :LANES], xs[:, LANES:]
                kk = kh_ref[ci]
                kr, ki = kk[:, :LANES], kk[:, LANES:]
                prods.append(jnp.concatenate([xr * kr - xi * ki, xr * ki + xi * kr], axis=1))
            back.append(jnp.dot(jnp.concatenate(prods, axis=0).astype(BF16), gc_ref[...],
                                preferred_element_type=F32))
        for h in range(parts):
            for k, ci in enumerate(chans[h]):
                bs = back[h][k * n1:(k + 1) * n1]
                br, bim = bs[:, :LANES], bs[:, LANES:]
                st = jnp.concatenate([br * tr + bim * ti, bim * tr - br * ti], axis=0)
                y = jnp.dot(p_ref[...], st.astype(BF16), preferred_element_type=F32)
                x0 = jnp.concatenate([s_ref[2, pl.ds(ci, nh, stride=pitch), :],
                                      s_ref[3, pl.ds(ci, nh, stride=pitch), :]], axis=0)
                gated = (y + us[h][k] * skip_ref[c_base + ci]) * x0
                so_ref[0, pl.ds(ci, nh, stride=pitch), :] = gated[:nh]
                so_ref[1, pl.ds(ci, nh, stride=pitch), :] = gated[nh:]
        return carry

    lax.fori_loop(0, cb // group, body, 0)
    for bi in range(2):
        for j in range(nh):
            o_ref[bi, :, j * LANES:(j + 1) * LANES] = so_ref[bi, j * pitch:j * pitch + cb, :].astype(BF16)


def _longconv(u_t, x0_t, khat, skip, consts, *, cb, nh, n1, group):
    b, c, l = u_t.shape
    a1, twr, twi, g, gc, p = consts
    pitch = cb + 8
    blk = pl.BlockSpec((2, cb, l), lambda i, j: (j, i, 0))
    return pl.pallas_call(
        functools.partial(_longconv_kernel, cb=cb, nh=nh, n1=n1, group=group, pitch=pitch),
        out_shape=jax.ShapeDtypeStruct((b, c, l), BF16),
        grid=(c // cb, b // 2),
        in_specs=[pl.BlockSpec(memory_space=pltpu.SMEM), blk, blk,
                  pl.BlockSpec((cb, n1, 2 * LANES), lambda i, j: (i, 0, 0)),
                  _const_spec(a1.shape), _const_spec(twr.shape), _const_spec(twi.shape),
                  _const_spec(g.shape), _const_spec(gc.shape), _const_spec(p.shape)],
        out_specs=blk,
        scratch_shapes=[pltpu.VMEM((4, nh * pitch, LANES), F32), pltpu.VMEM((2, nh * pitch, LANES), F32)],
        compiler_params=_params("arbitrary", "arbitrary"),
        name="hyena_longconv",
    )(skip, u_t, x0_t, khat, a1, twr, twi, g, gc, p)


def _dft_tables(l):
    n = 2 * l
    n1 = n // LANES
    nh = l // LANES
    f1 = np.exp(-2j * np.pi * np.outer(np.arange(n1), np.arange(n1)) / n1)
    f2 = np.exp(-2j * np.pi * np.outer(np.arange(LANES), np.arange(LANES)) / LANES)
    tw = np.exp(-2j * np.pi * np.outer(np.arange(n1), np.arange(LANES)) / n)
    fh = f1[:, :nh]
    a1 = np.block([[fh.real, -fh.imag], [fh.imag, fh.real]])
    g = np.block([[f2.real, f2.imag], [-f2.imag, f2.real]])
    gc = np.block([[f2.real, -f2.imag], [f2.imag, f2.real]])
    ci = np.conj(f1)[:nh, :]
    p = np.block([[ci.real, -ci.imag], [ci.imag, ci.real]]) / n
    f1full = np.concatenate([f1.real, f1.imag], axis=0)

    def bf(a):
        return jnp.asarray(a, F32).astype(BF16)

    data = (bf(a1), jnp.asarray(tw.real, F32), jnp.asarray(tw.imag, F32), bf(g), bf(gc), bf(p))
    filt = (bf(f1full), jnp.asarray(tw.real, F32), jnp.asarray(tw.imag, F32), bf(g))
    return data, filt, n1, nh


def _rope_tables_t(l):
    rows = l // GRID_W
    row = np.repeat(np.arange(rows, dtype=np.float64), GRID_W)
    col = np.tile(np.arange(GRID_W, dtype=np.float64), rows)
    inv_freq = ROPE_THETA ** (-np.arange(AXIS_FREQS, dtype=np.float64) / AXIS_FREQS)
    ang = np.concatenate([inv_freq[:, None] * row[None, :], inv_freq[:, None] * col[None, :]], axis=0)
    return jnp.asarray(np.cos(ang), F32), jnp.asarray(np.sin(ang), F32)


def _filter_positions(l):
    t = np.linspace(0.0, 1.0, l)
    w = 2.0 * math.pi * np.arange(l, dtype=np.float64) / l
    bands = np.linspace(1e-4, HY_BANDS - 1, HY_BANDS)[None, :]
    z = np.concatenate([t[:, None], np.cos(bands * w[:, None]), -np.sin(bands * w[:, None])], axis=-1)
    idx = np.concatenate([np.arange(l), np.zeros((1,), np.int64), np.arange(l - 1, 0, -1)])
    return jnp.asarray(z[idx].T, F32), jnp.asarray(t[idx][None, :], F32)


def kernel(x, c, ctx, c_ctx, ada_w, ada_b, norm1_g, norm2_g, attn_wqkv, attn_wo, attn_q_gain, attn_k_gain, attn_sink, hy_w_in, hy_b_in, hy_conv_w, hy_conv_b, hy_f_w1, hy_f_b1, hy_f_freq1, hy_f_w2, hy_f_b2, hy_f_freq2, hy_f_wout, hy_decay, hy_skip, hy_w_out, hy_b_out, ffn_w1, ffn_w3, ffn_w2):
    b, l, d = x.shape
    assert b % 2 == 0, "batch pairs share one complex long-conv transform"
    tile = min(512, l)
    ff = ffn_w1.shape[-1]
    ff_chunk = 3 * MXU_TILE

    pad = (-(b + 1)) % 8
    cond = jnp.concatenate([c, c_ctx[None, :], jnp.zeros((pad, d), F32)], axis=0)
    mod = _ada(cond, ada_w, ada_b)

    def chunks(i, rows):
        m = mod[i, rows][:, None, :]
        return [m[..., k * d:(k + 1) * d] for k in range(6)]

    row = lambda a: a.reshape(1, -1)
    col = lambda a: a.reshape(-1, 1)

    sh1, sc1, g1, sh2, sc2, g2 = chunks(0, slice(0, b))
    csh1, csc1 = [jnp.broadcast_to(m, (b, 1, d)) for m in chunks(0, slice(b, b + 1))[:2]]
    wqkv_t = attn_wqkv[0].T.astype(BF16)
    cos_t, sin_t = _rope_tables_t(l)
    qg, kg = col(attn_q_gain[0]), col(attn_k_gain[0])
    q_t, k_t, v_t = _qkv(x, sh1, sc1, row(norm1_g[0]), wqkv_t, qg, kg, cos_t, sin_t, n_q=N_HEADS, tile=tile)
    kx_t, vx_t = _qkv(ctx, csh1, csc1, row(norm1_g[0]), wqkv_t[N_HEADS * HEAD_DIM:], qg, kg, None, None,
                      n_q=0, tile=ctx.shape[1])
    o_t = _attention(attn_sink[0], q_t, k_t, v_t, kx_t, vx_t)
    x = _proj_ffn(x, o_t, g1, sh2, sc2, g2, attn_wo[0].astype(BF16), jnp.zeros((1, d), F32), row(norm2_g[0]),
                  ffn_w1[0].astype(BF16), ffn_w3[0].astype(BF16), ffn_w2[0].astype(BF16),
                  tile=tile, ff_chunk=ff_chunk)

    sh1, sc1, g1, sh2, sc2, g2 = chunks(1, slice(0, b))
    data_consts, filt_consts, n1, nh = _dft_tables(l)
    zz_t, t_row = _filter_positions(l)
    taps, norm = _filter_taps(zz_t, t_row, hy_f_w1[0], hy_f_b1[0], hy_f_freq1[0], hy_f_w2[0], hy_f_b2[0],
                              hy_f_freq2[0], hy_f_wout[0], hy_decay[0], l=l, cols=min(1024, l))
    khat = _filter_fft(taps, norm, filt_consts, cb=32, n1=n1, group=4)
    u_t, x0_t = _inproj(x, sh1, sc1, row(norm1_g[1]), hy_w_in[0].T.astype(BF16), hy_b_in[0],
                        hy_conv_w[0], hy_conv_b[0], tile=tile, slab=min(1024, d))
    gated = _longconv(u_t, x0_t, khat, hy_skip[0], data_consts, cb=32, nh=nh, n1=n1, group=8)
    x = _proj_ffn(x, gated, g1, sh2, sc2, g2, hy_w_out[0].astype(BF16), row(hy_b_out[0]),
                  row(norm2_g[1]), ffn_w1[1].astype(BF16), ffn_w3[1].astype(BF16), ffn_w2[1].astype(BF16),
                  tile=tile, ff_chunk=ff_chunk)
    return x
```

```python
import functools
import math

import numpy as np
import jax
import jax.numpy as jnp
from jax import lax
from jax.experimental import pallas as pl
from jax.experimental.pallas import tpu as pltpu

N_HEADS = 16
N_KV_HEADS = 4
HEAD_DIM = 64
GROUP = N_HEADS // N_KV_HEADS
ROPE_HALF = HEAD_DIM // 2
AXIS_FREQS = ROPE_HALF // 2
BLOCK = 128
GRID_W = 64
ROPE_THETA = 10000.0
ATTN_SCALE = HEAD_DIM ** -0.5
LOG2E = math.log2(math.e)
HY_BANDS = 16
EPS = 1e-6
NEG = -1e30
LANES = 128
MXU_TILE = 256
VMEM_LIMIT = 56 * 1024 * 1024

F32 = jnp.float32
BF16 = jnp.bfloat16
NT = (((1,), (1,)), ((), ()))
TN = (((0,), (0,)), ((), ()))


def _params(*sem):
    return pltpu.CompilerParams(dimension_semantics=sem, vmem_limit_bytes=VMEM_LIMIT)


def _const_spec(shape):
    n = len(shape)
    return pl.BlockSpec(shape, lambda *_: (0,) * n, pipeline_mode=pl.Buffered(1))


def _norm_mod(x, g, sc, sh):
    y = x * lax.rsqrt(jnp.mean(x * x, axis=-1, keepdims=True) + EPS)
    return (y * g) * (1.0 + sc) + sh


def _ada_kernel(cond_ref, w_ref, b_ref, o_ref):
    cnd = cond_ref[...]
    s = (cnd * jax.nn.sigmoid(cnd)).astype(BF16)
    o_ref[0] = jnp.dot(s, w_ref[0].astype(BF16), preferred_element_type=F32) + b_ref[0]


def _ada(cond, ada_w, ada_b):
    depth, d, d6 = ada_w.shape
    r = cond.shape[0]
    cw = 1536
    return pl.pallas_call(
        _ada_kernel,
        out_shape=jax.ShapeDtypeStruct((depth, r, d6), F32),
        grid=(depth, d6 // cw),
        in_specs=[pl.BlockSpec((r, d), lambda i, j: (0, 0)),
                  pl.BlockSpec((1, d, cw), lambda i, j: (i, 0, j)),
                  pl.BlockSpec((1, 1, cw), lambda i, j: (i, 0, j))],
        out_specs=pl.BlockSpec((1, r, cw), lambda i, j: (i, 0, j)),
        compiler_params=_params("arbitrary", "arbitrary"),
        name="ada_mod",
    )(cond, ada_w, ada_b.reshape(depth, 1, d6))


def _head_norm(t, gain, n_heads):
    t3 = t.reshape(n_heads, HEAD_DIM, t.shape[-1])
    ms = jnp.mean(t3 * t3, axis=1, keepdims=True)
    return (t3 * lax.rsqrt(ms + EPS)) * gain[None]


def _rope(t3, cos, sin):
    x1, x2 = t3[:, :ROPE_HALF], t3[:, ROPE_HALF:]
    c, s = cos[None], sin[None]
    return jnp.concatenate([x1 * c - x2 * s, x1 * s + x2 * c], axis=1)


def _qkv_kernel(*refs, n_q, rope):
    if rope:
        x_ref, sh_ref, sc_ref, g_ref, w_ref, qg_ref, kg_ref, cos_ref, sin_ref = refs[:9]
        outs = refs[9:]
    else:
        x_ref, sh_ref, sc_ref, g_ref, w_ref, qg_ref, kg_ref = refs[:7]
        outs = refs[7:]
    h = _norm_mod(x_ref[0], g_ref[...], sc_ref[0], sh_ref[0]).astype(BF16)
    nq = n_q * HEAD_DIM
    nk = N_KV_HEADS * HEAD_DIM
    tt = h.shape[0]

    def proj(r0, r1):
        return lax.dot_general(w_ref[r0:r1, :], h, NT, preferred_element_type=F32)

    q_rows = 8 * HEAD_DIM
    slabs = [(r0, min(r0 + q_rows, nq)) for r0 in range(0, nq, q_rows)] + [(nq, nq + 2 * nk)]
    if n_q:
        q_ref, k_ref, v_ref = outs
    else:
        k_ref, v_ref = outs
    nxt = proj(*slabs[0])
    for i, (r0, r1) in enumerate(slabs):
        t = nxt
        if i + 1 < len(slabs):
            nxt = proj(*slabs[i + 1])
        if r0 < nq:
            q3 = _head_norm(t, qg_ref[...], (r1 - r0) // HEAD_DIM)
            if rope:
                q3 = _rope(q3, cos_ref[...], sin_ref[...])
            q_ref[0, r0:r1, :] = (q3 * (ATTN_SCALE * LOG2E)).reshape(r1 - r0, tt).astype(BF16)
        else:
            k3 = _head_norm(t[:nk], kg_ref[...], N_KV_HEADS)
            if rope:
                k3 = _rope(k3, cos_ref[...], sin_ref[...])
            k_ref[0] = k3.reshape(nk, tt).astype(BF16)
            v_ref[0] = t[nk:].astype(BF16)


def _qkv(x, sh, sc, g, w_t, qg, kg, cos_t, sin_t, *, n_q, tile):
    b, l, d = x.shape
    nq, nk = n_q * HEAD_DIM, N_KV_HEADS * HEAD_DIM
    rope = cos_t is not None
    vec = pl.BlockSpec((1, 1, d), lambda i, j: (i, 0, 0))
    in_specs = [pl.BlockSpec((1, tile, d), lambda i, j: (i, j, 0)), vec, vec,
                _const_spec((1, d)), _const_spec(w_t.shape),
                _const_spec((HEAD_DIM, 1)), _const_spec((HEAD_DIM, 1))]
    args = [x, sh, sc, g, w_t, qg, kg]
    if rope:
        in_specs += [pl.BlockSpec((ROPE_HALF, tile), lambda i, j: (0, j))] * 2
        args += [cos_t, sin_t]
    out_shape, out_specs = [], []
    for rows in ([nq] if n_q else []) + [nk, nk]:
        out_shape.append(jax.ShapeDtypeStruct((b, rows, l), BF16))
        out_specs.append(pl.BlockSpec((1, rows, tile), lambda i, j: (i, 0, j)))
    return pl.pallas_call(
        functools.partial(_qkv_kernel, n_q=n_q, rope=rope),
        out_shape=out_shape, grid=(b, l // tile), in_specs=in_specs, out_specs=out_specs,
        compiler_params=_params("arbitrary", "arbitrary"),
        name="qkv_proj" if n_q else "ctx_kv_proj",
    )(*args)


HEADS_PER_TILE = GROUP


def _scores(q_ref, cols, kt_ref, r0, ctx0, biases, kh, heads):
    q4 = jnp.concatenate([q_ref[0, h * HEAD_DIM:(h + 1) * HEAD_DIM, cols] for h in heads], axis=1)
    zeros = jnp.zeros_like(q4)
    qz = jnp.concatenate([q4, zeros] if kh % 2 == 0 else [zeros, q4], axis=0)
    lanes = slice((kh // 2) * LANES, (kh // 2 + 1) * LANES)
    s_loc = jnp.dot(kt_ref[r0:r0 + 3 * BLOCK, lanes], qz, preferred_element_type=F32)
    s_ctx = jnp.dot(kt_ref[ctx0:, lanes], qz, preferred_element_type=F32)
    return [s_loc[:BLOCK] + biases[0], s_loc[BLOCK:2 * BLOCK], s_loc[2 * BLOCK:] + biases[1], s_ctx]


def _softmax_pv(sink_ref, o_ref, cols, vparts, kh, heads, parts, ones_rows):
    rows = slice(kh * HEAD_DIM, (kh + 1) * HEAD_DIM)
    sink = jnp.concatenate([jnp.full((1, BLOCK), sink_ref[h] * LOG2E, F32) for h in heads], axis=1)
    m = sink
    for part in parts:
        m = jnp.maximum(m, jnp.max(part, axis=0, keepdims=True))
    p = jnp.concatenate([jnp.exp2(part - m).astype(BF16) for part in parts], axis=0)
    vaug = jnp.concatenate([vp(rows) for vp in vparts], axis=1)
    vaug = jnp.concatenate([vaug, ones_rows], axis=0)
    o_aug = jnp.dot(vaug, p, preferred_element_type=F32)
    denom = jnp.exp2(sink - m) + o_aug[HEAD_DIM:HEAD_DIM + 1]
    o4 = o_aug[:HEAD_DIM] / denom
    for g, h in enumerate(heads):
        o_ref[0, h * HEAD_DIM:(h + 1) * HEAD_DIM, cols] = o4[:, g * BLOCK:(g + 1) * BLOCK].astype(BF16)


def _attn_kernel(sink_ref, q_ref, kp_ref, km_ref, kn_ref, vp_ref, vm_ref, vn_ref, kx_ref, vx_ref, o_ref, kt_ref,
                 *, qb):
    j = pl.program_id(1)
    nj = pl.num_programs(1)
    wq = HEADS_PER_TILE * BLOCK
    key = lax.broadcasted_iota(jnp.int32, (BLOCK, wq), 0)
    qry = lax.broadcasted_iota(jnp.int32, (BLOCK, wq), 1) % BLOCK
    band_prev = jnp.where(key >= qry, 0.0, NEG)
    band_next = jnp.where(key <= qry, 0.0, NEG)
    blk = lambda i: slice(i * BLOCK, (i + 1) * BLOCK)
    kt_ref[0:BLOCK, :] = kp_ref[0].T
    kt_ref[BLOCK:(qb + 1) * BLOCK, :] = km_ref[0].T
    kt_ref[(qb + 1) * BLOCK:(qb + 2) * BLOCK, :] = kn_ref[0].T
    ctx0 = (qb + 2) * BLOCK
    kt_ref[ctx0:, :] = kx_ref[0].T
    lc = kx_ref.shape[2]
    ones_rows = jnp.where(lax.broadcasted_iota(jnp.int32, (16, 3 * BLOCK + lc), 0) == 0, 1.0, 0.0).astype(BF16)

    def local_v(i):
        refs = [(lambda r: vp_ref[0, r]) if i == 0 else (lambda r, i=i: vm_ref[0, r, blk(i - 1)]),
                lambda r, i=i: vm_ref[0, r, blk(i)],
                (lambda r: vn_ref[0, r]) if i == qb - 1 else (lambda r, i=i: vm_ref[0, r, blk(i + 1)])]
        return refs + [lambda r: vx_ref[0, r]]

    work = []
    for i in range(qb):
        b_prev = jnp.where(j > 0, band_prev, NEG) if i == 0 else band_prev
        b_next = jnp.where(j < nj - 1, band_next, NEG) if i == qb - 1 else band_next
        for kh in range(N_KV_HEADS):
            for t in range(GROUP // HEADS_PER_TILE):
                heads = [kh * GROUP + t * HEADS_PER_TILE + g for g in range(HEADS_PER_TILE)]
                work.append((blk(i), i * BLOCK, local_v(i), (b_prev, b_next), kh, heads))
    score = lambda w: _scores(q_ref, w[0], kt_ref, w[1], ctx0, w[3], w[4], w[5])
    nxt = score(work[0])
    for n, w in enumerate(work):
        parts = nxt
        if n + 1 < len(work):
            nxt = score(work[n + 1])
        _softmax_pv(sink_ref, o_ref, w[0], w[2], w[4], w[5], parts, ones_rows)


def _attention(sink, q_t, k_t, v_t, kx_t, vx_t, *, qb):
    b, dq, l = q_t.shape
    nk = k_t.shape[1]
    lc = kx_t.shape[2]
    nb = l // BLOCK
    prev = pl.BlockSpec((1, nk, BLOCK), lambda i, j: (i, 0, jnp.maximum(qb * j - 1, 0)))
    mid = pl.BlockSpec((1, nk, qb * BLOCK), lambda i, j: (i, 0, j))
    nxt = pl.BlockSpec((1, nk, BLOCK), lambda i, j: (i, 0, jnp.minimum(qb * j + qb, nb - 1)))
    ctx = pl.BlockSpec((1, nk, lc), lambda i, j: (i, 0, 0))
    return pl.pallas_call(
        functools.partial(_attn_kernel, qb=qb),
        out_shape=jax.ShapeDtypeStruct((b, dq, l), BF16),
        grid=(b, nb // qb),
        in_specs=[pl.BlockSpec(memory_space=pltpu.SMEM),
                  pl.BlockSpec((1, dq, qb * BLOCK), lambda i, j: (i, 0, j)),
                  prev, mid, nxt, prev, mid, nxt, ctx, ctx],
        out_specs=pl.BlockSpec((1, dq, qb * BLOCK), lambda i, j: (i, 0, j)),
        scratch_shapes=[pltpu.VMEM(((qb + 2) * BLOCK + lc, nk), BF16)],
        compiler_params=_params("arbitrary", "arbitrary"),
        name="band_attention",
    )(sink, q_t, k_t, k_t, k_t, v_t, v_t, v_t, kx_t, vx_t)


def _proj_ffn_kernel(x_ref, a_ref, g1_ref, sh2_ref, sc2_ref, g2_ref, wo_ref, bo_ref, n2_ref,
                     w1_ref, w3_ref, w2_ref, o_ref, *, ff_chunk):
    proj = lax.dot_general(a_ref[0], wo_ref[...], TN, preferred_element_type=F32)
    x1 = x_ref[0] + g1_ref[0] * (proj + bo_ref[...])
    h = _norm_mod(x1, n2_ref[...], sc2_ref[0], sh2_ref[0]).astype(BF16)

    def up(c0, c1):
        return (jnp.dot(h, w1_ref[:, c0:c1], preferred_element_type=F32),
                jnp.dot(h, w3_ref[:, c0:c1], preferred_element_type=F32))

    ff = w1_ref.shape[1]
    spans = [(c0, min(c0 + ff_chunk, ff)) for c0 in range(0, ff, ff_chunk)]
    acc, nxt = None, up(*spans[0])
    for i, (c0, c1) in enumerate(spans):
        a, b = nxt
        if i + 1 < len(spans):
            nxt = up(*spans[i + 1])
        act = ((a * jax.nn.sigmoid(a)) * b).astype(BF16)
        part = jnp.dot(act, w2_ref[c0:c1, :], preferred_element_type=F32)
        acc = part if acc is None else acc + part
    o_ref[0] = x1 + g2_ref[0] * acc


def _proj_ffn(x, a_t, g1, sh2, sc2, g2, wo, bo, n2, w1, w3, w2, *, tile, ff_chunk):
    b, l, d = x.shape
    c = a_t.shape[1]
    vec = pl.BlockSpec((1, 1, d), lambda i, j: (i, 0, 0))
    return pl.pallas_call(
        functools.partial(_proj_ffn_kernel, ff_chunk=ff_chunk),
        out_shape=jax.ShapeDtypeStruct((b, l, d), F32),
        grid=(b, l // tile),
        in_specs=[pl.BlockSpec((1, tile, d), lambda i, j: (i, j, 0)),
                  pl.BlockSpec((1, c, tile), lambda i, j: (i, 0, j)),
                  vec, vec, vec, vec,
                  _const_spec(wo.shape), _const_spec((1, d)), _const_spec((1, d)),
                  _const_spec(w1.shape), _const_spec(w3.shape), _const_spec(w2.shape)],
        out_specs=pl.BlockSpec((1, tile, d), lambda i, j: (i, j, 0)),
        compiler_params=_params("arbitrary", "arbitrary"),
        name="proj_swiglu",
    )(x, a_t, g1, sh2, sc2, g2, wo, bo, n2, w1, w3, w2)


def _inproj_kernel(x_ref, sh_ref, sc_ref, g_ref, w_ref, b_ref, cw_ref, cb_ref, u_ref, x0_ref, zc_ref, lcol_ref,
                   *, slab):
    j = pl.program_id(1)
    nt = pl.num_programs(1) - 1
    d = u_ref.shape[1]
    tt = zc_ref.shape[1]
    nl = tt // LANES

    @pl.when(j == 0)
    def _():
        zc_ref[...] = jnp.zeros_like(zc_ref)
        lcol_ref[...] = jnp.zeros_like(lcol_ref)

    h = _norm_mod(x_ref[0], g_ref[...], sc_ref[0], sh_ref[0]).astype(BF16)
    sub = min(slab, LANES)
    lane = lax.broadcasted_iota(jnp.int32, (sub, LANES), 1)
    first, last = lane == 0, lane == LANES - 1
    has_next = j < nt
    zero = jnp.zeros((sub, LANES), BF16)

    def proj(c0):
        outs = []
        for p in range(3):
            rows = slice(p * d + c0, p * d + c0 + slab)
            z = lax.dot_general(w_ref[rows, :], h, NT, preferred_element_type=F32)
            bias = b_ref[rows, :]
            outs.append(jnp.concatenate([z[:, k * LANES:(k + 1) * LANES] + bias for k in range(nl)],
                                        axis=1).astype(BF16))
        return outs

    def conv_rows(p, r0, znew_first):
        rows = slice(p * d + r0, p * d + r0 + sub)
        w0, w1, w2 = [cw_ref[k, rows, :] for k in range(3)]
        bias = cb_ref[rows, :]
        z = [zc_ref[rows, k * LANES:(k + 1) * LANES] for k in range(nl)]
        right = jnp.where(has_next, znew_first, zero)
        fwd = [pltpu.roll(t, 1, axis=1) for t in [lcol_ref[rows, :]] + z]
        bwd = [pltpu.roll(t, LANES - 1, axis=1) for t in z + [right]]
        tiles = []
        for k in range(nl):
            prev = jnp.where(first, fwd[k], fwd[k + 1])
            nxt = jnp.where(last, bwd[k + 1], bwd[k])
            tiles.append(w0 * prev.astype(F32) + w1 * z[k].astype(F32) + w2 * nxt.astype(F32) + bias)
        lcol_ref[rows, :] = z[nl - 1]
        return tiles

    def finish(c0, znew):
        for s0 in range(0, slab, sub):
            r0 = c0 + s0
            x1 = conv_rows(1, r0, znew[1][s0:s0 + sub, :LANES])
            v = conv_rows(2, r0, znew[2][s0:s0 + sub, :LANES])
            for k in range(nl):
                u_ref[0, r0:r0 + sub, k * LANES:(k + 1) * LANES] = (v[k] * x1[k]).astype(BF16)
            x0 = conv_rows(0, r0, znew[0][s0:s0 + sub, :LANES])
            for k in range(nl):
                x0_ref[0, r0:r0 + sub, k * LANES:(k + 1) * LANES] = x0[k].astype(BF16)
        for p in range(3):
            zc_ref[p * d + c0:p * d + c0 + slab, :] = znew[p]

    starts = list(range(0, d, slab))
    nxt = proj(starts[0])
    for i, c0 in enumerate(starts):
        znew = nxt
        if i + 1 < len(starts):
            nxt = proj(starts[i + 1])
        finish(c0, znew)


def _inproj(x, sh, sc, g, w_t, b_vec, conv_w, conv_b, *, tile, slab):
    b, l, d = x.shape
    c3 = w_t.shape[0]
    c = c3 // 3
    nt = l // tile
    vec = pl.BlockSpec((1, 1, d), lambda i, j: (i, 0, 0))
    out = pl.BlockSpec((1, c, tile), lambda i, j: (i, 0, jnp.maximum(j - 1, 0)))
    wide = lambda a: jnp.broadcast_to(a[..., None], a.shape + (LANES,))
    return pl.pallas_call(
        functools.partial(_inproj_kernel, slab=slab),
        out_shape=[jax.ShapeDtypeStruct((b, c, l), BF16), jax.ShapeDtypeStruct((b, c, l), BF16)],
        grid=(b, nt + 1),
        in_specs=[pl.BlockSpec((1, tile, d), lambda i, j: (i, jnp.minimum(j, nt - 1), 0)), vec, vec,
                  _const_spec((1, d)), _const_spec(w_t.shape), _const_spec((c3, LANES)),
                  _const_spec((3, c3, LANES)), _const_spec((c3, LANES))],
        out_specs=[out, out],
        scratch_shapes=[pltpu.VMEM((c3, tile), BF16), pltpu.VMEM((c3, LANES), BF16)],
        compiler_params=_params("arbitrary", "arbitrary"),
        name="hyena_inproj_conv",
    )(x, sh, sc, g, w_t, wide(b_vec), wide(conv_w), wide(conv_b))


def _split(x):
    hi = x.astype(BF16)
    return hi, (x - hi.astype(F32)).astype(BF16)


def _dot3(a, b):
    a_hi, a_lo = _split(a)
    b_hi, b_lo = _split(b)
    d = functools.partial(jnp.dot, preferred_element_type=F32)
    return d(a_hi, b_hi) + (d(a_hi, b_lo) + d(a_lo, b_hi))


def _filter_taps_kernel(z_ref, t_ref, w1_ref, b1_ref, f1_ref, w2_ref, b2_ref, f2_ref, wo_ref, dec_ref,
                        k_ref, norm_ref, *, l):
    i = pl.program_id(0)
    h = jnp.sin(f1_ref[...] * (_dot3(w1_ref[...], z_ref[...]) + b1_ref[...]))
    h = jnp.sin(f2_ref[...] * (_dot3(w2_ref[...], h) + b2_ref[...]))
    win = jnp.exp(-t_ref[...] * jnp.abs(dec_ref[...]))
    n = z_ref.shape[1]
    slot = i * n + lax.broadcasted_iota(jnp.int32, (1, n), 1)
    k = jnp.where(slot == l, 0.0, _dot3(wo_ref[...], h) * win)

    @pl.when(i == 0)
    def _():
        norm_ref[...] = jnp.zeros_like(norm_ref)

    part = jnp.abs(k[:, :LANES])
    for j in range(1, n // LANES):
        part = part + jnp.abs(k[:, j * LANES:(j + 1) * LANES])
    norm_ref[...] += part
    k_ref[...] = k


def _filter_taps(zz_t, t_row, w1, b1, f1, w2, b2, f2, wout, dec, *, l, cols):
    e, n2 = zz_t.shape
    hid = w1.shape[1]
    c = dec.shape[0]
    nblk = n2 // cols
    col = lambda a: a.reshape(-1, 1)
    return pl.pallas_call(
        functools.partial(_filter_taps_kernel, l=l),
        out_shape=[jax.ShapeDtypeStruct((c, n2), F32), jax.ShapeDtypeStruct((c, LANES), F32)],
        grid=(nblk,),
        in_specs=[pl.BlockSpec((e, cols), lambda i: (0, i)), pl.BlockSpec((1, cols), lambda i: (0, i)),
                  _const_spec((hid, e)), _const_spec((hid, 1)), _const_spec((hid, 1)),
                  _const_spec((hid, hid)), _const_spec((hid, 1)), _const_spec((hid, 1)),
                  pl.BlockSpec((c, hid), lambda i: ((2 * i) // nblk, 0)), _const_spec((c, 1))],
        out_specs=[pl.BlockSpec((c, cols), lambda i: (0, i)), pl.BlockSpec((c, LANES), lambda i: (0, 0))],
        compiler_params=_params("arbitrary"),
        name="hyena_filter_taps",
    )(zz_t, t_row, w1.T, col(b1), col(f1), w2.T, col(b2), col(f2), wout.T, col(dec))


def _filter_fft_kernel(k_ref, norm_ref, f1_ref, twr_ref, twi_ref, g_ref, o_ref, s_ref, *, cb, n1, pitch, group):
    k = k_ref[...] / jnp.sum(norm_ref[...], axis=1, keepdims=True)
    for j in range(n1):
        s_ref[j * pitch:j * pitch + cb, :] = k[:, j * LANES:(j + 1) * LANES]
    tr, ti = twr_ref[...], twi_ref[...]

    def body(gi, carry):
        st = []
        for g in range(group):
            kc = s_ref[pl.ds(gi * group + g, n1, stride=pitch), :]
            ri = jnp.dot(f1_ref[...], kc.astype(BF16), preferred_element_type=F32)
            re, im = ri[:n1], ri[n1:]
            st.append(jnp.concatenate([re * tr - im * ti, re * ti + im * tr], axis=1))
        spec = jnp.dot(jnp.concatenate(st, axis=0).astype(BF16), g_ref[...], preferred_element_type=F32)
        for g in range(group):
            o_ref[gi * group + g] = spec[g * n1:(g + 1) * n1]
        return carry

    lax.fori_loop(0, cb // group, body, 0)


def _filter_fft(taps, norm, consts, *, cb, n1, group):
    c, n2 = taps.shape
    f1, twr, twi, g = consts
    pitch = cb + 8
    return pl.pallas_call(
        functools.partial(_filter_fft_kernel, cb=cb, n1=n1, pitch=pitch, group=group),
        out_shape=jax.ShapeDtypeStruct((c, n1, 2 * LANES), F32),
        grid=(c // cb,),
        in_specs=[pl.BlockSpec((cb, n2), lambda i: (i, 0)), pl.BlockSpec((cb, LANES), lambda i: (i, 0)),
                  _const_spec(f1.shape), _const_spec(twr.shape), _const_spec(twi.shape), _const_spec(g.shape)],
        out_specs=pl.BlockSpec((cb, n1, 2 * LANES), lambda i: (i, 0, 0)),
        scratch_shapes=[pltpu.VMEM((n1 * pitch, LANES), F32)],
        compiler_params=_params("arbitrary"),
        name="hyena_filter_fft",
    )(taps, norm, f1, twr, twi, g)


def _longconv_kernel(skip_ref, u_ref, x0_ref, kh_ref, a1_ref, twr_ref, twi_ref, g_ref, gc_ref, p_ref, o_ref,
                     s_ref, so_ref, *, cb, nh, n1, group, pitch):
    for bi in range(2):
        for j in range(nh):
            s_ref[bi, j * pitch:j * pitch + cb, :] = u_ref[bi, :, j * LANES:(j + 1) * LANES].astype(F32)
            s_ref[2 + bi, j * pitch:j * pitch + cb, :] = x0_ref[bi, :, j * LANES:(j + 1) * LANES].astype(F32)

    c_base = pl.program_id(0) * cb
    tr, ti = twr_ref[...], twi_ref[...]

    def body(gi, carry):
        parts = 2
        sub = group // parts
        chans = [[gi * group + h * sub + k for k in range(sub)] for h in range(parts)]
        us = [[jnp.concatenate([s_ref[0, pl.ds(ci, nh, stride=pitch), :],
                                s_ref[1, pl.ds(ci, nh, stride=pitch), :]], axis=0) for ci in cs]
              for cs in chans]
        first = [[jnp.dot(a1_ref[...], u.astype(BF16), preferred_element_type=F32) for u in uh]
                 for uh in us]
        spec = []
        for h in range(parts):
            st1 = [jnp.concatenate([ri[:n1] * tr - ri[n1:] * ti, ri[:n1] * ti + ri[n1:] * tr], axis=1)
                   for ri in first[h]]
            spec.append(jnp.dot(jnp.concatenate(st1, axis=0).astype(BF16), g_ref[...],
                                preferred_element_type=F32))
        back = []
        for h in range(parts):
            prods = []
            for k, ci in enumerate(chans[h]):
                xs = spec[h][k * n1:(k + 1) * n1]
                xr, xi = xs[:, :LANES], xs[:, LANES:]
                kk = kh_ref[ci]
                kr, ki = kk[:, :LANES], kk[:, LANES:]
                prods.append(jnp.concatenate([xr * kr - xi * ki, xr * ki + xi * kr], axis=1))
            back.append(jnp.dot(jnp.concatenate(prods, axis=0).astype(BF16), gc_ref[...],
                                preferred_element_type=F32))
        for h in range(parts):
            for k, ci in enumerate(chans[h]):
                bs = back[h][k * n1:(k + 1) * n1]
                br, bim = bs[:, :LANES], bs[:, LANES:]
                st = jnp.concatenate([br * tr + bim * ti, bim * tr - br * ti], axis=0)
                y = jnp.dot(p_ref[...], st.astype(BF16), preferred_element_type=F32)
                x0 = jnp.concatenate([s_ref[2, pl.ds(ci, nh, stride=pitch), :],
                                      s_ref[3, pl.ds(ci, nh, stride=pitch), :]], axis=0)
                gated = (y + us[h][k] * skip_ref[c_base + ci]) * x0
                so_ref[0, pl.ds(ci, nh, stride=pitch), :] = gated[:nh]
                so_ref[1, pl.ds(ci, nh, stride=pitch), :] = gated[nh:]
        return carry

    lax.fori_loop(0, cb // group, body, 0)
    for bi in range(2):
        for j in range(nh):
            o_ref[bi, :, j * LANES:(j + 1) * LANES] = so_ref[bi, j * pitch:j * pitch + cb, :].astype(BF16)


def _longconv(u_t, x0_t, khat, skip, consts, *, cb, nh, n1, group):
    b, c, l = u_t.shape
    a1, twr, twi, g, gc, p = consts
    pitch = cb + 8
    blk = pl.BlockSpec((2, cb, l), lambda i, j: (j, i, 0))
    return pl.pallas_call(
        functools.partial(_longconv_kernel, cb=cb, nh=nh, n1=n1, group=group, pitch=pitch),
        out_shape=jax.ShapeDtypeStruct((b, c, l), BF16),
        grid=(c // cb, b // 2),
        in_specs=[pl.BlockSpec(memory_space=pltpu.SMEM), blk, blk,
                  pl.BlockSpec((cb, n1, 2 * LANES), lambda i, j: (i, 0, 0)),
                  _const_spec(a1.shape), _const_spec(twr.shape), _const_spec(twi.shape),
                  _const_spec(g.shape), _const_spec(gc.shape), _const_spec(p.shape)],
        out_specs=blk,
        scratch_shapes=[pltpu.VMEM((4, nh * pitch, LANES), F32), pltpu.VMEM((2, nh * pitch, LANES), F32)],
        compiler_params=_params("arbitrary", "arbitrary"),
        name="hyena_longconv",
    )(skip, u_t, x0_t, khat, a1, twr, twi, g, gc, p)


def _dft_tables(l):
    n = 2 * l
    n1 = n // LANES
    nh = l // LANES
    f1 = np.exp(-2j * np.pi * np.outer(np.arange(n1), np.arange(n1)) / n1)
    f2 = np.exp(-2j * np.pi * np.outer(np.arange(LANES), np.arange(LANES)) / LANES)
    tw = np.exp(-2j * np.pi * np.outer(np.arange(n1), np.arange(LANES)) / n)
    fh = f1[:, :nh]
    a1 = np.block([[fh.real, -fh.imag], [fh.imag, fh.real]])
    g = np.block([[f2.real, f2.imag], [-f2.imag, f2.real]])
    gc = np.block([[f2.real, -f2.imag], [f2.imag, f2.real]])
    ci = np.conj(f1)[:nh, :]
    p = np.block([[ci.real, -ci.imag], [ci.imag, ci.real]]) / n
    f1full = np.concatenate([f1.real, f1.imag], axis=0)

    def bf(a):
        return jnp.asarray(a, F32).astype(BF16)

    data = (bf(a1), jnp.asarray(tw.real, F32), jnp.asarray(tw.imag, F32), bf(g), bf(gc), bf(p))
    filt = (bf(f1full), jnp.asarray(tw.real, F32), jnp.asarray(tw.imag, F32), bf(g))
    return data, filt, n1, nh


def _rope_tables_t(l):
    rows = l // GRID_W
    row = np.repeat(np.arange(rows, dtype=np.float64), GRID_W)
    col = np.tile(np.arange(GRID_W, dtype=np.float64), rows)
    inv_freq = ROPE_THETA ** (-np.arange(AXIS_FREQS, dtype=np.float64) / AXIS_FREQS)
    ang = np.concatenate([inv_freq[:, None] * row[None, :], inv_freq[:, None] * col[None, :]], axis=0)
    return jnp.asarray(np.cos(ang), F32), jnp.asarray(np.sin(ang), F32)


def _filter_positions(l):
    t = np.linspace(0.0, 1.0, l)
    w = 2.0 * math.pi * np.arange(l, dtype=np.float64) / l
    bands = np.linspace(1e-4, HY_BANDS - 1, HY_BANDS)[None, :]
    z = np.concatenate([t[:, None], np.cos(bands * w[:, None]), -np.sin(bands * w[:, None])], axis=-1)
    idx = np.concatenate([np.arange(l), np.zeros((1,), np.int64), np.arange(l - 1, 0, -1)])
    return jnp.asarray(z[idx].T, F32), jnp.asarray(t[idx][None, :], F32)


def kernel(x, c, ctx, c_ctx, ada_w, ada_b, norm1_g, norm2_g, attn_wqkv, attn_wo, attn_q_gain, attn_k_gain, attn_sink, hy_w_in, hy_b_in, hy_conv_w, hy_conv_b, hy_f_w1, hy_f_b1, hy_f_freq1, hy_f_w2, hy_f_b2, hy_f_freq2, hy_f_wout, hy_decay, hy_skip, hy_w_out, hy_b_out, ffn_w1, ffn_w3, ffn_w2):
    b, l, d = x.shape
    assert b % 2 == 0, "batch pairs share one complex long-conv transform"
    tile = min(512, l)
    ff = ffn_w1.shape[-1]
    ff_chunk = 3 * MXU_TILE

    pad = (-(b + 1)) % 8
    cond = jnp.concatenate([c, c_ctx[None, :], jnp.zeros((pad, d), F32)], axis=0)
    mod = _ada(cond, ada_w, ada_b)

    def chunks(i, rows):
        m = mod[i, rows][:, None, :]
        return [m[..., k * d:(k + 1) * d] for k in range(6)]

    row = lambda a: a.reshape(1, -1)
    col = lambda a: a.reshape(-1, 1)

    sh1, sc1, g1, sh2, sc2, g2 = chunks(0, slice(0, b))
    csh1, csc1 = [jnp.broadcast_to(m, (b, 1, d)) for m in chunks(0, slice(b, b + 1))[:2]]
    wqkv_t = attn_wqkv[0].T.astype(BF16)
    cos_t, sin_t = _rope_tables_t(l)
    qg, kg = col(attn_q_gain[0]), col(attn_k_gain[0])
    q_t, k_t, v_t = _qkv(x, sh1, sc1, row(norm1_g[0]), wqkv_t, qg, kg, cos_t, sin_t, n_q=N_HEADS, tile=tile)
    kx_t, vx_t = _qkv(ctx, csh1, csc1, row(norm1_g[0]), wqkv_t[N_HEADS * HEAD_DIM:], qg, kg, None, None,
                      n_q=0, tile=ctx.shape[1])
    o_t = _attention(attn_sink[0], q_t, k_t, v_t, kx_t, vx_t, qb=4)
    x = _proj_ffn(x, o_t, g1, sh2, sc2, g2, attn_wo[0].astype(BF16), jnp.zeros((1, d), F32), row(norm2_g[0]),
                  ffn_w1[0].astype(BF16), ffn_w3[0].astype(BF16), ffn_w2[0].astype(BF16),
                  tile=tile, ff_chunk=ff_chunk)

    sh1, sc1, g1, sh2, sc2, g2 = chunks(1, slice(0, b))
    data_consts, filt_consts, n1, nh = _dft_tables(l)
    zz_t, t_row = _filter_positions(l)
    taps, norm = _filter_taps(zz_t, t_row, hy_f_w1[0], hy_f_b1[0], hy_f_freq1[0], hy_f_w2[0], hy_f_b2[0],
                              hy_f_freq2[0], hy_f_wout[0], hy_decay[0], l=l, cols=min(1024, l))
    khat = _filter_fft(taps, norm, filt_consts, cb=32, n1=n1, group=4)
    u_t, x0_t = _inproj(x, sh1, sc1, row(norm1_g[1]), hy_w_in[0].T.astype(BF16), hy_b_in[0],
                        hy_conv_w[0], hy_conv_b[0], tile=tile, slab=min(1024, d))
    gated = _longconv(u_t, x0_t, khat, hy_skip[0], data_consts, cb=32, nh=nh, n1=n1, group=8)
    x = _proj_ffn(x, gated, g1, sh2, sc2, g2, hy_w_out[0].astype(BF16), row(hy_b_out[0]),
                  row(norm2_g[1]), ffn_w1[1].astype(BF16), ffn_w3[1].astype(BF16), ffn_w2[1].astype(BF16),
                  tile=tile, ff_chunk=ff_chunk)
    return x
```

```python
import functools
import math

import numpy as np
import jax
import jax.numpy as jnp
from jax import lax
from jax.experimental import pallas as pl
from jax.experimental.pallas import tpu as pltpu

N_HEADS = 16
N_KV_HEADS = 4
HEAD_DIM = 64
GROUP = N_HEADS // N_KV_HEADS
ROPE_HALF = HEAD_DIM // 2
AXIS_FREQS = ROPE_HALF // 2
BLOCK = 128
GRID_W = 64
ROPE_THETA = 10000.0
ATTN_SCALE = HEAD_DIM ** -0.5
LOG2E = math.log2(math.e)
HY_BANDS = 16
EPS = 1e-6
NEG = -1e30
LANES = 128
MXU_TILE = 256
VMEM_LIMIT = 56 * 1024 * 1024

F32 = jnp.float32
BF16 = jnp.bfloat16
NT = (((1,), (1,)), ((), ()))
TN = (((0,), (0,)), ((), ()))


def _params(*sem):
    return pltpu.CompilerParams(dimension_semantics=sem, vmem_limit_bytes=VMEM_LIMIT)


def _const_spec(shape):
    n = len(shape)
    return pl.BlockSpec(shape, lambda *_: (0,) * n, pipeline_mode=pl.Buffered(1))


def _norm_mod(x, g, sc, sh):
    y = x * lax.rsqrt(jnp.mean(x * x, axis=-1, keepdims=True) + EPS)
    return (y * g) * (1.0 + sc) + sh


def _ada_kernel(cond_ref, w_ref, b_ref, o_ref):
    cnd = cond_ref[...]
    s = (cnd * jax.nn.sigmoid(cnd)).astype(BF16)
    o_ref[0] = jnp.dot(s, w_ref[0].astype(BF16), preferred_element_type=F32) + b_ref[0]


def _ada(cond, ada_w, ada_b):
    depth, d, d6 = ada_w.shape
    r = cond.shape[0]
    cw = 1536
    return pl.pallas_call(
        _ada_kernel,
        out_shape=jax.ShapeDtypeStruct((depth, r, d6), F32),
        grid=(depth, d6 // cw),
        in_specs=[pl.BlockSpec((r, d), lambda i, j: (0, 0)),
                  pl.BlockSpec((1, d, cw), lambda i, j: (i, 0, j)),
                  pl.BlockSpec((1, 1, cw), lambda i, j: (i, 0, j))],
        out_specs=pl.BlockSpec((1, r, cw), lambda i, j: (i, 0, j)),
        compiler_params=_params("arbitrary", "arbitrary"),
        name="ada_mod",
    )(cond, ada_w, ada_b.reshape(depth, 1, d6))


def _head_norm(t, gain, n_heads):
    t3 = t.reshape(n_heads, HEAD_DIM, t.shape[-1])
    ms = jnp.mean(t3 * t3, axis=1, keepdims=True)
    return (t3 * lax.rsqrt(ms + EPS)) * gain[None]


def _rope(t3, cos, sin):
    x1, x2 = t3[:, :ROPE_HALF], t3[:, ROPE_HALF:]
    c, s = cos[None], sin[None]
    return jnp.concatenate([x1 * c - x2 * s, x1 * s + x2 * c], axis=1)


def _qkv_kernel(*refs, n_q, rope):
    if rope:
        x_ref, sh_ref, sc_ref, g_ref, w_ref, qg_ref, kg_ref, cos_ref, sin_ref = refs[:9]
        outs = refs[9:]
    else:
        x_ref, sh_ref, sc_ref, g_ref, w_ref, qg_ref, kg_ref = refs[:7]
        outs = refs[7:]
    h = _norm_mod(x_ref[0], g_ref[...], sc_ref[0], sh_ref[0]).astype(BF16)
    nq = n_q * HEAD_DIM
    nk = N_KV_HEADS * HEAD_DIM
    tt = h.shape[0]

    def proj(r0, r1):
        return lax.dot_general(w_ref[r0:r1, :], h, NT, preferred_element_type=F32)

    q_rows = 8 * HEAD_DIM
    slabs = [(r0, min(r0 + q_rows, nq)) for r0 in range(0, nq, q_rows)] + [(nq, nq + 2 * nk)]
    if n_q:
        q_ref, k_ref, v_ref = outs
    else:
        k_ref, v_ref = outs
    nxt = proj(*slabs[0])
    for i, (r0, r1) in enumerate(slabs):
        t = nxt
        if i + 1 < len(slabs):
            nxt = proj(*slabs[i + 1])
        if r0 < nq:
            q3 = _head_norm(t, qg_ref[...], (r1 - r0) // HEAD_DIM)
            if rope:
                q3 = _rope(q3, cos_ref[...], sin_ref[...])
            q_ref[0, r0:r1, :] = (q3 * (ATTN_SCALE * LOG2E)).reshape(r1 - r0, tt).astype(BF16)
        else:
            k3 = _head_norm(t[:nk], kg_ref[...], N_KV_HEADS)
            if rope:
                k3 = _rope(k3, cos_ref[...], sin_ref[...])
            k_ref[0] = k3.reshape(nk, tt).astype(BF16)
            v_ref[0] = t[nk:].astype(BF16)


def _qkv(x, sh, sc, g, w_t, qg, kg, cos_t, sin_t, *, n_q, tile):
    b, l, d = x.shape
    nq, nk = n_q * HEAD_DIM, N_KV_HEADS * HEAD_DIM
    rope = cos_t is not None
    vec = pl.BlockSpec((1, 1, d), lambda i, j: (i, 0, 0))
    in_specs = [pl.BlockSpec((1, tile, d), lambda i, j: (i, j, 0)), vec, vec,
                _const_spec((1, d)), _const_spec(w_t.shape),
                _const_spec((HEAD_DIM, 1)), _const_spec((HEAD_DIM, 1))]
    args = [x, sh, sc, g, w_t, qg, kg]
    if rope:
        in_specs += [pl.BlockSpec((ROPE_HALF, tile), lambda i, j: (0, j))] * 2
        args += [cos_t, sin_t]
    out_shape, out_specs = [], []
    for rows in ([nq] if n_q else []) + [nk, nk]:
        out_shape.append(jax.ShapeDtypeStruct((b, rows, l), BF16))
        out_specs.append(pl.BlockSpec((1, rows, tile), lambda i, j: (i, 0, j)))
    return pl.pallas_call(
        functools.partial(_qkv_kernel, n_q=n_q, rope=rope),
        out_shape=out_shape, grid=(b, l // tile), in_specs=in_specs, out_specs=out_specs,
        compiler_params=_params("arbitrary", "arbitrary"),
        name="qkv_proj" if n_q else "ctx_kv_proj",
    )(*args)


HEADS_PER_TILE = GROUP


def _scores(q_ref, cols, kt_ref, r0, ctx0, biases, kh, heads):
    q4 = jnp.concatenate([q_ref[0, h * HEAD_DIM:(h + 1) * HEAD_DIM, cols] for h in heads], axis=1)
    zeros = jnp.zeros_like(q4)
    qz = jnp.concatenate([q4, zeros] if kh % 2 == 0 else [zeros, q4], axis=0)
    lanes = slice((kh // 2) * LANES, (kh // 2 + 1) * LANES)
    s_loc = jnp.dot(kt_ref[r0:r0 + 3 * BLOCK, lanes], qz, preferred_element_type=F32)
    s_ctx = jnp.dot(kt_ref[ctx0:, lanes], qz, preferred_element_type=F32)
    return [s_loc[:BLOCK] + biases[0], s_loc[BLOCK:2 * BLOCK], s_loc[2 * BLOCK:] + biases[1], s_ctx]


def _softmax_pv(sink_ref, o_ref, cols, vparts, kh, heads, parts, ones_rows):
    rows = slice(kh * HEAD_DIM, (kh + 1) * HEAD_DIM)
    sink = jnp.concatenate([jnp.full((1, BLOCK), sink_ref[h] * LOG2E, F32) for h in heads], axis=1)
    m = sink
    for part in parts:
        m = jnp.maximum(m, jnp.max(part, axis=0, keepdims=True))
    p = jnp.concatenate([jnp.exp2(part - m).astype(BF16) for part in parts], axis=0)
    vaug = jnp.concatenate([vp(rows) for vp in vparts], axis=1)
    vaug = jnp.concatenate([vaug, ones_rows], axis=0)
    o_aug = jnp.dot(vaug, p, preferred_element_type=F32)
    denom = jnp.exp2(sink - m) + o_aug[HEAD_DIM:HEAD_DIM + 1]
    o4 = o_aug[:HEAD_DIM] / denom
    for g, h in enumerate(heads):
        o_ref[0, h * HEAD_DIM:(h + 1) * HEAD_DIM, cols] = o4[:, g * BLOCK:(g + 1) * BLOCK].astype(BF16)


def _attn_kernel(sink_ref, q_ref, kp_ref, km_ref, kn_ref, vp_ref, vm_ref, vn_ref, kx_ref, vx_ref, o_ref, kt_ref,
                 *, qb):
    j = pl.program_id(1)
    nj = pl.num_programs(1)
    wq = HEADS_PER_TILE * BLOCK
    key = lax.broadcasted_iota(jnp.int32, (BLOCK, wq), 0)
    qry = lax.broadcasted_iota(jnp.int32, (BLOCK, wq), 1) % BLOCK
    band_prev = jnp.where(key >= qry, 0.0, NEG)
    band_next = jnp.where(key <= qry, 0.0, NEG)
    blk = lambda i: slice(i * BLOCK, (i + 1) * BLOCK)
    kt_ref[0:BLOCK, :] = kp_ref[0].T
    kt_ref[BLOCK:(qb + 1) * BLOCK, :] = km_ref[0].T
    kt_ref[(qb + 1) * BLOCK:(qb + 2) * BLOCK, :] = kn_ref[0].T
    ctx0 = (qb + 2) * BLOCK
    kt_ref[ctx0:, :] = kx_ref[0].T
    lc = kx_ref.shape[2]
    ones_rows = jnp.where(lax.broadcasted_iota(jnp.int32, (16, 3 * BLOCK + lc), 0) == 0, 1.0, 0.0).astype(BF16)

    def local_v(i):
        refs = [(lambda r: vp_ref[0, r]) if i == 0 else (lambda r, i=i: vm_ref[0, r, blk(i - 1)]),
                lambda r, i=i: vm_ref[0, r, blk(i)],
                (lambda r: vn_ref[0, r]) if i == qb - 1 else (lambda r, i=i: vm_ref[0, r, blk(i + 1)])]
        return refs + [lambda r: vx_ref[0, r]]

    work = []
    for i in range(qb):
        b_prev = jnp.where(j > 0, band_prev, NEG) if i == 0 else band_prev
        b_next = jnp.where(j < nj - 1, band_next, NEG) if i == qb - 1 else band_next
        for kh in range(N_KV_HEADS):
            for t in range(GROUP // HEADS_PER_TILE):
                heads = [kh * GROUP + t * HEADS_PER_TILE + g for g in range(HEADS_PER_TILE)]
                work.append((blk(i), i * BLOCK, local_v(i), (b_prev, b_next), kh, heads))
    score = lambda w: _scores(q_ref, w[0], kt_ref, w[1], ctx0, w[3], w[4], w[5])
    nxt = score(work[0])
    for n, w in enumerate(work):
        parts = nxt
        if n + 1 < len(work):
            nxt = score(work[n + 1])
        _softmax_pv(sink_ref, o_ref, w[0], w[2], w[4], w[5], parts, ones_rows)


def _attention(sink, q_t, k_t, v_t, kx_t, vx_t, *, qb):
    b, dq, l = q_t.shape
    nk = k_t.shape[1]
    lc = kx_t.shape[2]
    nb = l // BLOCK
    prev = pl.BlockSpec((1, nk, BLOCK), lambda i, j: (i, 0, jnp.maximum(qb * j - 1, 0)))
    mid = pl.BlockSpec((1, nk, qb * BLOCK), lambda i, j: (i, 0, j))
    nxt = pl.BlockSpec((1, nk, BLOCK), lambda i, j: (i, 0, jnp.minimum(qb * j + qb, nb - 1)))
    ctx = pl.BlockSpec((1, nk, lc), lambda i, j: (i, 0, 0))
    return pl.pallas_call(
        functools.partial(_attn_kernel, qb=qb),
        out_shape=jax.ShapeDtypeStruct((b, dq, l), BF16),
        grid=(b, nb // qb),
        in_specs=[pl.BlockSpec(memory_space=pltpu.SMEM),
                  pl.BlockSpec((1, dq, qb * BLOCK), lambda i, j: (i, 0, j)),
                  prev, mid, nxt, prev, mid, nxt, ctx, ctx],
        out_specs=pl.BlockSpec((1, dq, qb * BLOCK), lambda i, j: (i, 0, j)),
        scratch_shapes=[pltpu.VMEM(((qb + 2) * BLOCK + lc, nk), BF16)],
        compiler_params=_params("arbitrary", "arbitrary"),
        name="band_attention",
    )(sink, q_t, k_t, k_t, k_t, v_t, v_t, v_t, kx_t, vx_t)


def _proj_ffn_kernel(x_ref, a_ref, g1_ref, sh2_ref, sc2_ref, g2_ref, wo_ref, bo_ref, n2_ref,
                     w1_ref, w3_ref, w2_ref, o_ref, *, ff_chunk):
    proj = lax.dot_general(a_ref[0], wo_ref[...], TN, preferred_element_type=F32)
    x1 = x_ref[0] + g1_ref[0] * (proj + bo_ref[...])
    h = _norm_mod(x1, n2_ref[...], sc2_ref[0], sh2_ref[0]).astype(BF16)

    def up(c0, c1):
        return (jnp.dot(h, w1_ref[:, c0:c1], preferred_element_type=F32),
                jnp.dot(h, w3_ref[:, c0:c1], preferred_element_type=F32))

    ff = w1_ref.shape[1]
    spans = [(c0, min(c0 + ff_chunk, ff)) for c0 in range(0, ff, ff_chunk)]
    acc, nxt = None, up(*spans[0])
    for i, (c0, c1) in enumerate(spans):
        a, b = nxt
        if i + 1 < len(spans):
            nxt = up(*spans[i + 1])
        act = ((a * jax.nn.sigmoid(a)) * b).astype(BF16)
        part = jnp.dot(act, w2_ref[c0:c1, :], preferred_element_type=F32)
        acc = part if acc is None else acc + part
    o_ref[0] = x1 + g2_ref[0] * acc


def _proj_ffn(x, a_t, g1, sh2, sc2, g2, wo, bo, n2, w1, w3, w2, *, tile, ff_chunk):
    b, l, d = x.shape
    c = a_t.shape[1]
    vec = pl.BlockSpec((1, 1, d), lambda i, j: (i, 0, 0))
    return pl.pallas_call(
        functools.partial(_proj_ffn_kernel, ff_chunk=ff_chunk),
        out_shape=jax.ShapeDtypeStruct((b, l, d), F32),
        grid=(b, l // tile),
        in_specs=[pl.BlockSpec((1, tile, d), lambda i, j: (i, j, 0)),
                  pl.BlockSpec((1, c, tile), lambda i, j: (i, 0, j)),
                  vec, vec, vec, vec,
                  _const_spec(wo.shape), _const_spec((1, d)), _const_spec((1, d)),
                  _const_spec(w1.shape), _const_spec(w3.shape), _const_spec(w2.shape)],
        out_specs=pl.BlockSpec((1, tile, d), lambda i, j: (i, j, 0)),
        compiler_params=_params("arbitrary", "arbitrary"),
        name="proj_swiglu",
    )(x, a_t, g1, sh2, sc2, g2, wo, bo, n2, w1, w3, w2)


def _inproj_kernel(x_ref, sh_ref, sc_ref, g_ref, w_ref, b_ref, cw_ref, cb_ref, u_ref, x0_ref, zc_ref, lcol_ref,
                   *, slab):
    j = pl.program_id(1)
    nt = pl.num_programs(1) - 1
    d = u_ref.shape[1]
    tt = zc_ref.shape[1]
    nl = tt // LANES

    @pl.when(j == 0)
    def _():
        zc_ref[...] = jnp.zeros_like(zc_ref)
        lcol_ref[...] = jnp.zeros_like(lcol_ref)

    h = _norm_mod(x_ref[0], g_ref[...], sc_ref[0], sh_ref[0]).astype(BF16)
    sub = min(slab, LANES)
    lane = lax.broadcasted_iota(jnp.int32, (sub, LANES), 1)
    first, last = lane == 0, lane == LANES - 1
    has_next = j < nt
    zero = jnp.zeros((sub, LANES), BF16)

    def proj(c0):
        outs = []
        for p in range(3):
            rows = slice(p * d + c0, p * d + c0 + slab)
            z = lax.dot_general(w_ref[rows, :], h, NT, preferred_element_type=F32)
            bias = b_ref[rows, :]
            outs.append(jnp.concatenate([z[:, k * LANES:(k + 1) * LANES] + bias for k in range(nl)],
                                        axis=1).astype(BF16))
        return outs

    def conv_rows(p, r0, znew_first):
        rows = slice(p * d + r0, p * d + r0 + sub)
        w0, w1, w2 = [cw_ref[k, rows, :] for k in range(3)]
        bias = cb_ref[rows, :]
        z = [zc_ref[rows, k * LANES:(k + 1) * LANES] for k in range(nl)]
        right = jnp.where(has_next, znew_first, zero)
        fwd = [pltpu.roll(t, 1, axis=1) for t in [lcol_ref[rows, :]] + z]
        bwd = [pltpu.roll(t, LANES - 1, axis=1) for t in z + [right]]
        tiles = []
        for k in range(nl):
            prev = jnp.where(first, fwd[k], fwd[k + 1])
            nxt = jnp.where(last, bwd[k + 1], bwd[k])
            tiles.append(w0 * prev.astype(F32) + w1 * z[k].astype(F32) + w2 * nxt.astype(F32) + bias)
        lcol_ref[rows, :] = z[nl - 1]
        return tiles

    def finish(c0, znew):
        for s0 in range(0, slab, sub):
            r0 = c0 + s0
            x1 = conv_rows(1, r0, znew[1][s0:s0 + sub, :LANES])
            v = conv_rows(2, r0, znew[2][s0:s0 + sub, :LANES])
            for k in range(nl):
                u_ref[0, r0:r0 + sub, k * LANES:(k + 1) * LANES] = (v[k] * x1[k]).astype(BF16)
            x0 = conv_rows(0, r0, znew[0][s0:s0 + sub, :LANES])
            for k in range(nl):
                x0_ref[0, r0:r0 + sub, k * LANES:(k + 1) * LANES] = x0[k].astype(BF16)
        for p in range(3):
            zc_ref[p * d + c0:p * d + c0 + slab, :] = znew[p]

    starts = list(range(0, d, slab))
    nxt = proj(starts[0])
    for i, c0 in enumerate(starts):
        znew = nxt
        if i + 1 < len(starts):
            nxt = proj(starts[i + 1])
        finish(c0, znew)


def _inproj(x, sh, sc, g, w_t, b_vec, conv_w, conv_b, *, tile, slab):
    b, l, d = x.shape
    c3 = w_t.shape[0]
    c = c3 // 3
    nt = l // tile
    vec = pl.BlockSpec((1, 1, d), lambda i, j: (i, 0, 0))
    out = pl.BlockSpec((1, c, tile), lambda i, j: (i, 0, jnp.maximum(j - 1, 0)))
    wide = lambda a: jnp.broadcast_to(a[..., None], a.shape + (LANES,))
    return pl.pallas_call(
        functools.partial(_inproj_kernel, slab=slab),
        out_shape=[jax.ShapeDtypeStruct((b, c, l), BF16), jax.ShapeDtypeStruct((b, c, l), BF16)],
        grid=(b, nt + 1),
        in_specs=[pl.BlockSpec((1, tile, d), lambda i, j: (i, jnp.minimum(j, nt - 1), 0)), vec, vec,
                  _const_spec((1, d)), _const_spec(w_t.shape), _const_spec((c3, LANES)),
                  _const_spec((3, c3, LANES)), _const_spec((c3, LANES))],
        out_specs=[out, out],
        scratch_shapes=[pltpu.VMEM((c3, tile), BF16), pltpu.VMEM((c3, LANES), BF16)],
        compiler_params=_params("arbitrary", "arbitrary"),
        name="hyena_inproj_conv",
    )(x, sh, sc, g, w_t, wide(b_vec), wide(conv_w), wide(conv_b))


def _split(x):
    hi = x.astype(BF16)
    return hi, (x - hi.astype(F32)).astype(BF16)


def _dot3(a, b):
    a_hi, a_lo = _split(a)
    b_hi, b_lo = _split(b)
    d = functools.partial(jnp.dot, preferred_element_type=F32)
    return d(a_hi, b_hi) + (d(a_hi, b_lo) + d(a_lo, b_hi))


def _filter_taps_kernel(z_ref, t_ref, w1_ref, b1_ref, f1_ref, w2_ref, b2_ref, f2_ref, wo_ref, dec_ref,
                        k_ref, norm_ref, *, l):
    i = pl.program_id(0)
    h = jnp.sin(f1_ref[...] * (_dot3(w1_ref[...], z_ref[...]) + b1_ref[...]))
    h = jnp.sin(f2_ref[...] * (_dot3(w2_ref[...], h) + b2_ref[...]))
    win = jnp.exp(-t_ref[...] * jnp.abs(dec_ref[...]))
    n = z_ref.shape[1]
    slot = i * n + lax.broadcasted_iota(jnp.int32, (1, n), 1)
    k = jnp.where(slot == l, 0.0, _dot3(wo_ref[...], h) * win)

    @pl.when(i == 0)
    def _():
        norm_ref[...] = jnp.zeros_like(norm_ref)

    part = jnp.abs(k[:, :LANES])
    for j in range(1, n // LANES):
        part = part + jnp.abs(k[:, j * LANES:(j + 1) * LANES])
    norm_ref[...] += part
    k_ref[...] = k


def _filter_taps(zz_t, t_row, w1, b1, f1, w2, b2, f2, wout, dec, *, l, cols):
    e, n2 = zz_t.shape
    hid = w1.shape[1]
    c = dec.shape[0]
    nblk = n2 // cols
    col = lambda a: a.reshape(-1, 1)
    return pl.pallas_call(
        functools.partial(_filter_taps_kernel, l=l),
        out_shape=[jax.ShapeDtypeStruct((c, n2), F32), jax.ShapeDtypeStruct((c, LANES), F32)],
        grid=(nblk,),
        in_specs=[pl.BlockSpec((e, cols), lambda i: (0, i)), pl.BlockSpec((1, cols), lambda i: (0, i)),
                  _const_spec((hid, e)), _const_spec((hid, 1)), _const_spec((hid, 1)),
                  _const_spec((hid, hid)), _const_spec((hid, 1)), _const_spec((hid, 1)),
                  pl.BlockSpec((c, hid), lambda i: ((2 * i) // nblk, 0)), _const_spec((c, 1))],
        out_specs=[pl.BlockSpec((c, cols), lambda i: (0, i)), pl.BlockSpec((c, LANES), lambda i: (0, 0))],
        compiler_params=_params("arbitrary"),
        name="hyena_filter_taps",
    )(zz_t, t_row, w1.T, col(b1), col(f1), w2.T, col(b2), col(f2), wout.T, col(dec))


def _filter_fft_kernel(k_ref, norm_ref, f1_ref, twr_ref, twi_ref, g_ref, o_ref, s_ref, *, cb, n1, pitch, group):
    k = k_ref[...] / jnp.sum(norm_ref[...], axis=1, keepdims=True)
    for j in range(n1):
        s_ref[j * pitch:j * pitch + cb, :] = k[:, j * LANES:(j + 1) * LANES]
    tr, ti = twr_ref[...], twi_ref[...]

    def body(gi, carry):
        st = []
        for g in range(group):
            kc = s_ref[pl.ds(gi * group + g, n1, stride=pitch), :]
            ri = jnp.dot(f1_ref[...], kc.astype(BF16), preferred_element_type=F32)
            re, im = ri[:n1], ri[n1:]
            st.append(jnp.concatenate([re * tr - im * ti, re * ti + im * tr], axis=1))
        spec = jnp.dot(jnp.concatenate(st, axis=0).astype(BF16), g_ref[...], preferred_element_type=F32)
        for g in range(group):
            o_ref[gi * group + g] = spec[g * n1:(g + 1) * n1]
        return carry

    lax.fori_loop(0, cb // group, body, 0)


def _filter_fft(taps, norm, consts, *, cb, n1, group):
    c, n2 = taps.shape
    f1, twr, twi, g = consts
    pitch = cb + 8
    return pl.pallas_call(
        functools.partial(_filter_fft_kernel, cb=cb, n1=n1, pitch=pitch, group=group),
        out_shape=jax.ShapeDtypeStruct((c, n1, 2 * LANES), F32),
        grid=(c // cb,),
        in_specs=[pl.BlockSpec((cb, n2), lambda i: (i, 0)), pl.BlockSpec((cb, LANES), lambda i: (i, 0)),
                  _const_spec(f1.shape), _const_spec(twr.shape), _const_spec(twi.shape), _const_spec(g.shape)],
        out_specs=pl.BlockSpec((cb, n1, 2 * LANES), lambda i: (i, 0, 0)),
        scratch_shapes=[pltpu.VMEM((n1 * pitch, LANES), F32)],
        compiler_params=_params("arbitrary"),
        name="hyena_filter_fft",
    )(taps, norm, f1, twr, twi, g)


def _longconv_kernel(skip_ref, u_ref, x0_ref, kh_ref, a1_ref, twr_ref, twi_ref, g_ref, gc_ref, p_ref, o_ref,
                     s_ref, so_ref, *, cb, nh, n1, group, pitch):
    for bi in range(2):
        for j in range(nh):
            s_ref[bi, j * pitch:j * pitch + cb, :] = u_ref[bi, :, j * LANES:(j + 1) * LANES].astype(F32)
            s_ref[2 + bi, j * pitch:j * pitch + cb, :] = x0_ref[bi, :, j * LANES:(j + 1) * LANES].astype(F32)

    c_base = pl.program_id(0) * cb
    tr, ti = twr_ref[...], twi_ref[...]

    def body(gi, carry):
        parts = 2
        sub = group // parts
        chans = [[gi * group + h * sub + k for k in range(sub)] for h in range(parts)]
        us = [[jnp.concatenate([s_ref[0, pl.ds(ci, nh, stride=pitch), :],
                                s_ref[1, pl.ds(ci, nh, stride=pitch), :]], axis=0) for ci in cs]
              for cs in chans]
        first = [[jnp.dot(a1_ref[...], u.astype(BF16), preferred_element_type=F32) for u in uh]
                 for uh in us]
        spec = []
        for h in range(parts):
            st1 = [jnp.concatenate([ri[:n1] * tr - ri[n1:] * ti, ri[:n1] * ti + ri[n1:] * tr], axis=1)
                   for ri in first[h]]
            spec.append(jnp.dot(jnp.concatenate(st1, axis=0).astype(BF16), g_ref[...],
                                preferred_element_type=F32))
        back = []
        for h in range(parts):
            prods = []
            for k, ci in enumerate(chans[h]):
                xs = spec[h][k * n1:(k + 1) * n1]
                xr, xi = xs[:, :LANES], xs[:, LANES:]
                kk = kh_ref[ci]
                kr, ki = kk[:, :LANES], kk[:, LANES:]
                prods.append(jnp.concatenate([xr * kr - xi * ki, xr * ki + xi * kr], axis=1))
            back.append(jnp.dot(jnp.concatenate(prods, axis=0).astype(BF16), gc_ref[...],
                                preferred_element_type=F32))
        for h in range(parts):
            for k, ci in enumerate(chans[h]):
                bs = back[h][k * n1:(k + 1) * n1]
                br, bim = bs[:, :LANES], bs[:, LANES:]
                st = jnp.concatenate([br * tr + bim * ti, bim * tr - br * ti], axis=0)
                y = jnp.dot(p_ref[...], st.astype(BF16), preferred_element_type=F32)
                x0 = jnp.concatenate([s_ref[2, pl.ds(ci, nh, stride=pitch), :],
                                      s_ref[3, pl.ds(ci, nh, stride=pitch), :]], axis=0)
                gated = (y + us[h][k] * skip_ref[c_base + ci]) * x0
                so_ref[0, pl.ds(ci, nh, stride=pitch), :] = gated[:nh]
                so_ref[1, pl.ds(ci, nh, stride=pitch), :] = gated[nh:]
        return carry

    lax.fori_loop(0, cb // group, body, 0)
    for bi in range(2):
        for j in range(nh):
            o_ref[bi, :, j * LANES:(j + 1) * LANES] = so_ref[bi, j * pitch:j * pitch + cb, :].astype(BF16)


def _longconv(u_t, x0_t, khat, skip, consts, *, cb, nh, n1, group):
    b, c, l = u_t.shape
    a1, twr, twi, g, gc, p = consts
    pitch = cb + 8
    blk = pl.BlockSpec((2, cb, l), lambda i, j: (j, i, 0))
    return pl.pallas_call(
        functools.partial(_longconv_kernel, cb=cb, nh=nh, n1=n1, group=group, pitch=pitch),
        out_shape=jax.ShapeDtypeStruct((b, c, l), BF16),
        grid=(c // cb, b // 2),
        in_specs=[pl.BlockSpec(memory_space=pltpu.SMEM), blk, blk,
                  pl.BlockSpec((cb, n1, 2 * LANES), lambda i, j: (i, 0, 0)),
                  _const_spec(a1.shape), _const_spec(twr.shape), _const_spec(twi.shape),
                  _const_spec(g.shape), _const_spec(gc.shape), _const_spec(p.shape)],
        out_specs=blk,
        scratch_shapes=[pltpu.VMEM((4, nh * pitch, LANES), F32), pltpu.VMEM((2, nh * pitch, LANES), F32)],
        compiler_params=_params("arbitrary", "arbitrary"),
        name="hyena_longconv",
    )(skip, u_t, x0_t, khat, a1, twr, twi, g, gc, p)


def _dft_tables(l):
    n = 2 * l
    n1 = n // LANES
    nh = l // LANES
    f1 = np.exp(-2j * np.pi * np.outer(np.arange(n1), np.arange(n1)) / n1)
    f2 = np.exp(-2j * np.pi * np.outer(np.arange(LANES), np.arange(LANES)) / LANES)
    tw = np.exp(-2j * np.pi * np.outer(np.arange(n1), np.arange(LANES)) / n)
    fh = f1[:, :nh]
    a1 = np.block([[fh.real, -fh.imag], [fh.imag, fh.real]])
    g = np.block([[f2.real, f2.imag], [-f2.imag, f2.real]])
    gc = np.block([[f2.real, -f2.imag], [f2.imag, f2.real]])
    ci = np.conj(f1)[:nh, :]
    p = np.block([[ci.real, -ci.imag], [ci.imag, ci.real]]) / n
    f1full = np.concatenate([f1.real, f1.imag], axis=0)

    def bf(a):
        return jnp.asarray(a, F32).astype(BF16)

    data = (bf(a1), jnp.asarray(tw.real, F32), jnp.asarray(tw.imag, F32), bf(g), bf(gc), bf(p))
    filt = (bf(f1full), jnp.asarray(tw.real, F32), jnp.asarray(tw.imag, F32), bf(g))
    return data, filt, n1, nh


def _rope_tables_t(l):
    rows = l // GRID_W
    row = np.repeat(np.arange(rows, dtype=np.float64), GRID_W)
    col = np.tile(np.arange(GRID_W, dtype=np.float64), rows)
    inv_freq = ROPE_THETA ** (-np.arange(AXIS_FREQS, dtype=np.float64) / AXIS_FREQS)
    ang = np.concatenate([inv_freq[:, None] * row[None, :], inv_freq[:, None] * col[None, :]], axis=0)
    return jnp.asarray(np.cos(ang), F32), jnp.asarray(np.sin(ang), F32)


def _filter_positions(l):
    t = np.linspace(0.0, 1.0, l)
    w = 2.0 * math.pi * np.arange(l, dtype=np.float64) / l
    bands = np.linspace(1e-4, HY_BANDS - 1, HY_BANDS)[None, :]
    z = np.concatenate([t[:, None], np.cos(bands * w[:, None]), -np.sin(bands * w[:, None])], axis=-1)
    idx = np.concatenate([np.arange(l), np.zeros((1,), np.int64), np.arange(l - 1, 0, -1)])
    return jnp.asarray(z[idx].T, F32), jnp.asarray(t[idx][None, :], F32)


def kernel(x, c, ctx, c_ctx, ada_w, ada_b, norm1_g, norm2_g, attn_wqkv, attn_wo, attn_q_gain, attn_k_gain, attn_sink, hy_w_in, hy_b_in, hy_conv_w, hy_conv_b, hy_f_w1, hy_f_b1, hy_f_freq1, hy_f_w2, hy_f_b2, hy_f_freq2, hy_f_wout, hy_decay, hy_skip, hy_w_out, hy_b_out, ffn_w1, ffn_w3, ffn_w2):
    b, l, d = x.shape
    assert b % 2 == 0, "batch pairs share one complex long-conv transform"
    tile = min(512, l)
    ff = ffn_w1.shape[-1]
    ff_chunk = 1 * MXU_TILE

    pad = (-(b + 1)) % 8
    cond = jnp.concatenate([c, c_ctx[None, :], jnp.zeros((pad, d), F32)], axis=0)
    mod = _ada(cond, ada_w, ada_b)

    def chunks(i, rows):
        m = mod[i, rows][:, None, :]
        return [m[..., k * d:(k + 1) * d] for k in range(6)]

    row = lambda a: a.reshape(1, -1)
    col = lambda a: a.reshape(-1, 1)

    sh1, sc1, g1, sh2, sc2, g2 = chunks(0, slice(0, b))
    csh1, csc1 = [jnp.broadcast_to(m, (b, 1, d)) for m in chunks(0, slice(b, b + 1))[:2]]
    wqkv_t = attn_wqkv[0].T.astype(BF16)
    cos_t, sin_t = _rope_tables_t(l)
    qg, kg = col(attn_q_gain[0]), col(attn_k_gain[0])
    q_t, k_t, v_t = _qkv(x, sh1, sc1, row(norm1_g[0]), wqkv_t, qg, kg, cos_t, sin_t, n_q=N_HEADS, tile=tile)
    kx_t, vx_t = _qkv(ctx, csh1, csc1, row(norm1_g[0]), wqkv_t[N_HEADS * HEAD_DIM:], qg, kg, None, None,
                      n_q=0, tile=ctx.shape[1])
    o_t = _attention(attn_sink[0], q_t, k_t, v_t, kx_t, vx_t, qb=8)
    x = _proj_ffn(x, o_t, g1, sh2, sc2, g2, attn_wo[0].astype(BF16), jnp.zeros((1, d), F32), row(norm2_g[0]),
                  ffn_w1[0].astype(BF16), ffn_w3[0].astype(BF16), ffn_w2[0].astype(BF16),
                  tile=min(2 * tile, l), ff_chunk=ff_chunk)

    sh1, sc1, g1, sh2, sc2, g2 = chunks(1, slice(0, b))
    data_consts, filt_consts, n1, nh = _dft_tables(l)
    zz_t, t_row = _filter_positions(l)
    taps, norm = _filter_taps(zz_t, t_row, hy_f_w1[0], hy_f_b1[0], hy_f_freq1[0], hy_f_w2[0], hy_f_b2[0],
                              hy_f_freq2[0], hy_f_wout[0], hy_decay[0], l=l, cols=min(1024, l))
    khat = _filter_fft(taps, norm, filt_consts, cb=32, n1=n1, group=4)
    u_t, x0_t = _inproj(x, sh1, sc1, row(norm1_g[1]), hy_w_in[0].T.astype(BF16), hy_b_in[0],
                        hy_conv_w[0], hy_conv_b[0], tile=tile, slab=min(1024, d))
    gated = _longconv(u_t, x0_t, khat, hy_skip[0], data_consts, cb=32, nh=nh, n1=n1, group=8)
    x = _proj_ffn(x, gated, g1, sh2, sc2, g2, hy_w_out[0].astype(BF16), row(hy_b_out[0]),
                  row(norm2_g[1]), ffn_w1[1].astype(BF16), ffn_w3[1].astype(BF16), ffn_w2[1].astype(BF16),
                  tile=min(2 * tile, l), ff_chunk=ff_chunk)
    return x
```

```python
import functools
import math

import numpy as np
import jax
import jax.numpy as jnp
from jax import lax
from jax.experimental import pallas as pl
from jax.experimental.pallas import tpu as pltpu

N_HEADS = 16
N_KV_HEADS = 4
HEAD_DIM = 64
GROUP = N_HEADS // N_KV_HEADS
ROPE_HALF = HEAD_DIM // 2
AXIS_FREQS = ROPE_HALF // 2
BLOCK = 128
GRID_W = 64
ROPE_THETA = 10000.0
ATTN_SCALE = HEAD_DIM ** -0.5
LOG2E = math.log2(math.e)
HY_BANDS = 16
EPS = 1e-6
NEG = -1e30
LANES = 128
MXU_TILE = 256
VMEM_LIMIT = 56 * 1024 * 1024

F32 = jnp.float32
BF16 = jnp.bfloat16
NT = (((1,), (1,)), ((), ()))
TN = (((0,), (0,)), ((), ()))


def _params(*sem):
    return pltpu.CompilerParams(dimension_semantics=sem, vmem_limit_bytes=VMEM_LIMIT)


def _const_spec(shape):
    n = len(shape)
    return pl.BlockSpec(shape, lambda *_: (0,) * n, pipeline_mode=pl.Buffered(1))


def _norm_mod(x, g, sc, sh):
    y = x * lax.rsqrt(jnp.mean(x * x, axis=-1, keepdims=True) + EPS)
    return (y * g) * (1.0 + sc) + sh


def _ada_kernel(cond_ref, w_ref, b_ref, o_ref):
    cnd = cond_ref[...]
    s = (cnd * jax.nn.sigmoid(cnd)).astype(BF16)
    o_ref[0] = jnp.dot(s, w_ref[0].astype(BF16), preferred_element_type=F32) + b_ref[0]


def _ada(cond, ada_w, ada_b):
    depth, d, d6 = ada_w.shape
    r = cond.shape[0]
    cw = 1536
    return pl.pallas_call(
        _ada_kernel,
        out_shape=jax.ShapeDtypeStruct((depth, r, d6), F32),
        grid=(depth, d6 // cw),
        in_specs=[pl.BlockSpec((r, d), lambda i, j: (0, 0)),
                  pl.BlockSpec((1, d, cw), lambda i, j: (i, 0, j)),
                  pl.BlockSpec((1, 1, cw), lambda i, j: (i, 0, j))],
        out_specs=pl.BlockSpec((1, r, cw), lambda i, j: (i, 0, j)),
        compiler_params=_params("arbitrary", "arbitrary"),
        name="ada_mod",
    )(cond, ada_w, ada_b.reshape(depth, 1, d6))


def _head_norm(t, gain, n_heads):
    t3 = t.reshape(n_heads, HEAD_DIM, t.shape[-1])
    ms = jnp.mean(t3 * t3, axis=1, keepdims=True)
    return (t3 * lax.rsqrt(ms + EPS)) * gain[None]


def _rope(t3, cos, sin):
    x1, x2 = t3[:, :ROPE_HALF], t3[:, ROPE_HALF:]
    c, s = cos[None], sin[None]
    return jnp.concatenate([x1 * c - x2 * s, x1 * s + x2 * c], axis=1)


def _qkv_kernel(*refs, n_q, rope):
    if rope:
        x_ref, sh_ref, sc_ref, g_ref, w_ref, qg_ref, kg_ref, cos_ref, sin_ref = refs[:9]
        outs = refs[9:]
    else:
        x_ref, sh_ref, sc_ref, g_ref, w_ref, qg_ref, kg_ref = refs[:7]
        outs = refs[7:]
    h = _norm_mod(x_ref[0], g_ref[...], sc_ref[0], sh_ref[0]).astype(BF16)
    nq = n_q * HEAD_DIM
    nk = N_KV_HEADS * HEAD_DIM
    tt = h.shape[0]

    def proj(r0, r1):
        return lax.dot_general(w_ref[r0:r1, :], h, NT, preferred_element_type=F32)

    q_rows = 8 * HEAD_DIM
    slabs = [(r0, min(r0 + q_rows, nq)) for r0 in range(0, nq, q_rows)] + [(nq, nq + 2 * nk)]
    if n_q:
        q_ref, k_ref, v_ref = outs
    else:
        k_ref, v_ref = outs
    nxt = proj(*slabs[0])
    for i, (r0, r1) in enumerate(slabs):
        t = nxt
        if i + 1 < len(slabs):
            nxt = proj(*slabs[i + 1])
        if r0 < nq:
            q3 = _head_norm(t, qg_ref[...], (r1 - r0) // HEAD_DIM)
            if rope:
                q3 = _rope(q3, cos_ref[...], sin_ref[...])
            q_ref[0, r0:r1, :] = (q3 * (ATTN_SCALE * LOG2E)).reshape(r1 - r0, tt).astype(BF16)
        else:
            k3 = _head_norm(t[:nk], kg_ref[...], N_KV_HEADS)
            if rope:
                k3 = _rope(k3, cos_ref[...], sin_ref[...])
            k_ref[0] = k3.reshape(nk, tt).astype(BF16)
            v_ref[0] = t[nk:].astype(BF16)


def _qkv(x, sh, sc, g, w_t, qg, kg, cos_t, sin_t, *, n_q, tile):
    b, l, d = x.shape
    nq, nk = n_q * HEAD_DIM, N_KV_HEADS * HEAD_DIM
    rope = cos_t is not None
    vec = pl.BlockSpec((1, 1, d), lambda i, j: (i, 0, 0))
    in_specs = [pl.BlockSpec((1, tile, d), lambda i, j: (i, j, 0)), vec, vec,
                _const_spec((1, d)), _const_spec(w_t.shape),
                _const_spec((HEAD_DIM, 1)), _const_spec((HEAD_DIM, 1))]
    args = [x, sh, sc, g, w_t, qg, kg]
    if rope:
        in_specs += [pl.BlockSpec((ROPE_HALF, tile), lambda i, j: (0, j))] * 2
        args += [cos_t, sin_t]
    out_shape, out_specs = [], []
    for rows in ([nq] if n_q else []) + [nk, nk]:
        out_shape.append(jax.ShapeDtypeStruct((b, rows, l), BF16))
        out_specs.append(pl.BlockSpec((1, rows, tile), lambda i, j: (i, 0, j)))
    return pl.pallas_call(
        functools.partial(_qkv_kernel, n_q=n_q, rope=rope),
        out_shape=out_shape, grid=(b, l // tile), in_specs=in_specs, out_specs=out_specs,
        compiler_params=_params("arbitrary", "arbitrary"),
        name="qkv_proj" if n_q else "ctx_kv_proj",
    )(*args)


HEADS_PER_TILE = GROUP


def _scores(q_ref, cols, kt_ref, r0, ctx0, biases, kh, heads):
    q4 = jnp.concatenate([q_ref[0, h * HEAD_DIM:(h + 1) * HEAD_DIM, cols] for h in heads], axis=1)
    zeros = jnp.zeros_like(q4)
    qz = jnp.concatenate([q4, zeros] if kh % 2 == 0 else [zeros, q4], axis=0)
    lanes = slice((kh // 2) * LANES, (kh // 2 + 1) * LANES)
    s_loc = jnp.dot(kt_ref[r0:r0 + 3 * BLOCK, lanes], qz, preferred_element_type=F32)
    s_ctx = jnp.dot(kt_ref[ctx0:, lanes], qz, preferred_element_type=F32)
    return [s_loc[:BLOCK] + biases[0], s_loc[BLOCK:2 * BLOCK], s_loc[2 * BLOCK:] + biases[1], s_ctx]


def _softmax_pv(sink_ref, o_ref, cols, vparts, kh, heads, parts, ones_rows):
    rows = slice(kh * HEAD_DIM, (kh + 1) * HEAD_DIM)
    sink = jnp.concatenate([jnp.full((1, BLOCK), sink_ref[h] * LOG2E, F32) for h in heads], axis=1)
    m = sink
    for part in parts:
        m = jnp.maximum(m, jnp.max(part, axis=0, keepdims=True))
    p = jnp.concatenate([jnp.exp2(part - m).astype(BF16) for part in parts], axis=0)
    vaug = jnp.concatenate([vp(rows) for vp in vparts], axis=1)
    vaug = jnp.concatenate([vaug, ones_rows], axis=0)
    o_aug = jnp.dot(vaug, p, preferred_element_type=F32)
    denom = jnp.exp2(sink - m) + o_aug[HEAD_DIM:HEAD_DIM + 1]
    o4 = o_aug[:HEAD_DIM] / denom
    for g, h in enumerate(heads):
        o_ref[0, h * HEAD_DIM:(h + 1) * HEAD_DIM, cols] = o4[:, g * BLOCK:(g + 1) * BLOCK].astype(BF16)


def _attn_kernel(sink_ref, q_ref, kp_ref, km_ref, kn_ref, vp_ref, vm_ref, vn_ref, kx_ref, vx_ref, o_ref, kt_ref,
                 *, qb):
    j = pl.program_id(1)
    nj = pl.num_programs(1)
    wq = HEADS_PER_TILE * BLOCK
    key = lax.broadcasted_iota(jnp.int32, (BLOCK, wq), 0)
    qry = lax.broadcasted_iota(jnp.int32, (BLOCK, wq), 1) % BLOCK
    band_prev = jnp.where(key >= qry, 0.0, NEG)
    band_next = jnp.where(key <= qry, 0.0, NEG)
    blk = lambda i: slice(i * BLOCK, (i + 1) * BLOCK)
    kt_ref[0:BLOCK, :] = kp_ref[0].T
    kt_ref[BLOCK:(qb + 1) * BLOCK, :] = km_ref[0].T
    kt_ref[(qb + 1) * BLOCK:(qb + 2) * BLOCK, :] = kn_ref[0].T
    ctx0 = (qb + 2) * BLOCK
    kt_ref[ctx0:, :] = kx_ref[0].T
    lc = kx_ref.shape[2]
    ones_rows = jnp.where(lax.broadcasted_iota(jnp.int32, (16, 3 * BLOCK + lc), 0) == 0, 1.0, 0.0).astype(BF16)

    def local_v(i):
        refs = [(lambda r: vp_ref[0, r]) if i == 0 else (lambda r, i=i: vm_ref[0, r, blk(i - 1)]),
                lambda r, i=i: vm_ref[0, r, blk(i)],
                (lambda r: vn_ref[0, r]) if i == qb - 1 else (lambda r, i=i: vm_ref[0, r, blk(i + 1)])]
        return refs + [lambda r: vx_ref[0, r]]

    work = []
    for i in range(qb):
        b_prev = jnp.where(j > 0, band_prev, NEG) if i == 0 else band_prev
        b_next = jnp.where(j < nj - 1, band_next, NEG) if i == qb - 1 else band_next
        for kh in range(N_KV_HEADS):
            for t in range(GROUP // HEADS_PER_TILE):
                heads = [kh * GROUP + t * HEADS_PER_TILE + g for g in range(HEADS_PER_TILE)]
                work.append((blk(i), i * BLOCK, local_v(i), (b_prev, b_next), kh, heads))
    score = lambda w: _scores(q_ref, w[0], kt_ref, w[1], ctx0, w[3], w[4], w[5])
    nxt = score(work[0])
    for n, w in enumerate(work):
        parts = nxt
        if n + 1 < len(work):
            nxt = score(work[n + 1])
        _softmax_pv(sink_ref, o_ref, w[0], w[2], w[4], w[5], parts, ones_rows)


def _attention(sink, q_t, k_t, v_t, kx_t, vx_t, *, qb):
    b, dq, l = q_t.shape
    nk = k_t.shape[1]
    lc = kx_t.shape[2]
    nb = l // BLOCK
    prev = pl.BlockSpec((1, nk, BLOCK), lambda i, j: (i, 0, jnp.maximum(qb * j - 1, 0)))
    mid = pl.BlockSpec((1, nk, qb * BLOCK), lambda i, j: (i, 0, j))
    nxt = pl.BlockSpec((1, nk, BLOCK), lambda i, j: (i, 0, jnp.minimum(qb * j + qb, nb - 1)))
    ctx = pl.BlockSpec((1, nk, lc), lambda i, j: (i, 0, 0))
    return pl.pallas_call(
        functools.partial(_attn_kernel, qb=qb),
        out_shape=jax.ShapeDtypeStruct((b, dq, l), BF16),
        grid=(b, nb // qb),
        in_specs=[pl.BlockSpec(memory_space=pltpu.SMEM),
                  pl.BlockSpec((1, dq, qb * BLOCK), lambda i, j: (i, 0, j)),
                  prev, mid, nxt, prev, mid, nxt, ctx, ctx],
        out_specs=pl.BlockSpec((1, dq, qb * BLOCK), lambda i, j: (i, 0, j)),
        scratch_shapes=[pltpu.VMEM(((qb + 2) * BLOCK + lc, nk), BF16)],
        compiler_params=_params("arbitrary", "arbitrary"),
        name="band_attention",
    )(sink, q_t, k_t, k_t, k_t, v_t, v_t, v_t, kx_t, vx_t)


def _proj_ffn_kernel(x_ref, a_ref, g1_ref, sh2_ref, sc2_ref, g2_ref, wo_ref, bo_ref, n2_ref,
                     w1_ref, w3_ref, w2_ref, o_ref, *, ff_chunk):
    proj = lax.dot_general(a_ref[0], wo_ref[...], TN, preferred_element_type=F32)
    x1 = x_ref[0] + g1_ref[0] * (proj + bo_ref[...])
    h = _norm_mod(x1, n2_ref[...], sc2_ref[0], sh2_ref[0]).astype(BF16)

    def up(c0, c1):
        return (jnp.dot(h, w1_ref[:, c0:c1], preferred_element_type=F32),
                jnp.dot(h, w3_ref[:, c0:c1], preferred_element_type=F32))

    ff = w1_ref.shape[1]
    spans = [(c0, min(c0 + ff_chunk, ff)) for c0 in range(0, ff, ff_chunk)]
    acc, nxt = None, up(*spans[0])
    for i, (c0, c1) in enumerate(spans):
        a, b = nxt
        if i + 1 < len(spans):
            nxt = up(*spans[i + 1])
        act = ((a * jax.nn.sigmoid(a)) * b).astype(BF16)
        part = jnp.dot(act, w2_ref[c0:c1, :], preferred_element_type=F32)
        acc = part if acc is None else acc + part
    o_ref[0] = x1 + g2_ref[0] * acc


def _proj_ffn(x, a_t, g1, sh2, sc2, g2, wo, bo, n2, w1, w3, w2, *, tile, ff_chunk):
    b, l, d = x.shape
    c = a_t.shape[1]
    vec = pl.BlockSpec((1, 1, d), lambda i, j: (i, 0, 0))
    return pl.pallas_call(
        functools.partial(_proj_ffn_kernel, ff_chunk=ff_chunk),
        out_shape=jax.ShapeDtypeStruct((b, l, d), F32),
        grid=(b, l // tile),
        in_specs=[pl.BlockSpec((1, tile, d), lambda i, j: (i, j, 0)),
                  pl.BlockSpec((1, c, tile), lambda i, j: (i, 0, j)),
                  vec, vec, vec, vec,
                  _const_spec(wo.shape), _const_spec((1, d)), _const_spec((1, d)),
                  _const_spec(w1.shape), _const_spec(w3.shape), _const_spec(w2.shape)],
        out_specs=pl.BlockSpec((1, tile, d), lambda i, j: (i, j, 0)),
        compiler_params=_params("arbitrary", "arbitrary"),
        name="proj_swiglu",
    )(x, a_t, g1, sh2, sc2, g2, wo, bo, n2, w1, w3, w2)


def _inproj_kernel(x_ref, sh_ref, sc_ref, g_ref, w_ref, b_ref, cw_ref, cb_ref, u_ref, x0_ref, zc_ref, lcol_ref,
                   *, slab):
    j = pl.program_id(1)
    nt = pl.num_programs(1) - 1
    d = u_ref.shape[1]
    tt = zc_ref.shape[1]
    nl = tt // LANES

    @pl.when(j == 0)
    def _():
        zc_ref[...] = jnp.zeros_like(zc_ref)
        lcol_ref[...] = jnp.zeros_like(lcol_ref)

    h = _norm_mod(x_ref[0], g_ref[...], sc_ref[0], sh_ref[0]).astype(BF16)
    sub = min(slab, LANES)
    lane = lax.broadcasted_iota(jnp.int32, (sub, LANES), 1)
    first, last = lane == 0, lane == LANES - 1
    has_next = j < nt
    zero = jnp.zeros((sub, LANES), BF16)

    def proj(c0):
        outs = []
        for p in range(3):
            rows = slice(p * d + c0, p * d + c0 + slab)
            z = lax.dot_general(w_ref[rows, :], h, NT, preferred_element_type=F32)
            bias = b_ref[rows, :]
            outs.append(jnp.concatenate([z[:, k * LANES:(k + 1) * LANES] + bias for k in range(nl)],
                                        axis=1).astype(BF16))
        return outs

    def conv_rows(p, r0, znew_first):
        rows = slice(p * d + r0, p * d + r0 + sub)
        w0, w1, w2 = [cw_ref[k, rows, :] for k in range(3)]
        bias = cb_ref[rows, :]
        z = [zc_ref[rows, k * LANES:(k + 1) * LANES] for k in range(nl)]
        right = jnp.where(has_next, znew_first, zero)
        fwd = [pltpu.roll(t, 1, axis=1) for t in [lcol_ref[rows, :]] + z]
        bwd = [pltpu.roll(t, LANES - 1, axis=1) for t in z + [right]]
        tiles = []
        for k in range(nl):
            prev = jnp.where(first, fwd[k], fwd[k + 1])
            nxt = jnp.where(last, bwd[k + 1], bwd[k])
            tiles.append(w0 * prev.astype(F32) + w1 * z[k].astype(F32) + w2 * nxt.astype(F32) + bias)
        lcol_ref[rows, :] = z[nl - 1]
        return tiles

    def finish(c0, znew):
        for s0 in range(0, slab, sub):
            r0 = c0 + s0
            x1 = conv_rows(1, r0, znew[1][s0:s0 + sub, :LANES])
            v = conv_rows(2, r0, znew[2][s0:s0 + sub, :LANES])
            for k in range(nl):
                u_ref[0, r0:r0 + sub, k * LANES:(k + 1) * LANES] = (v[k] * x1[k]).astype(BF16)
            x0 = conv_rows(0, r0, znew[0][s0:s0 + sub, :LANES])
            for k in range(nl):
                x0_ref[0, r0:r0 + sub, k * LANES:(k + 1) * LANES] = x0[k].astype(BF16)
        for p in range(3):
            zc_ref[p * d + c0:p * d + c0 + slab, :] = znew[p]

    starts = list(range(0, d, slab))
    nxt = proj(starts[0])
    for i, c0 in enumerate(starts):
        znew = nxt
        if i + 1 < len(starts):
            nxt = proj(starts[i + 1])
        finish(c0, znew)


def _inproj(x, sh, sc, g, w_t, b_vec, conv_w, conv_b, *, tile, slab):
    b, l, d = x.shape
    c3 = w_t.shape[0]
    c = c3 // 3
    nt = l // tile
    vec = pl.BlockSpec((1, 1, d), lambda i, j: (i, 0, 0))
    out = pl.BlockSpec((1, c, tile), lambda i, j: (i, 0, jnp.maximum(j - 1, 0)))
    wide = lambda a: jnp.broadcast_to(a[..., None], a.shape + (LANES,))
    return pl.pallas_call(
        functools.partial(_inproj_kernel, slab=slab),
        out_shape=[jax.ShapeDtypeStruct((b, c, l), BF16), jax.ShapeDtypeStruct((b, c, l), BF16)],
        grid=(b, nt + 1),
        in_specs=[pl.BlockSpec((1, tile, d), lambda i, j: (i, jnp.minimum(j, nt - 1), 0)), vec, vec,
                  _const_spec((1, d)), _const_spec(w_t.shape), _const_spec((c3, LANES)),
                  _const_spec((3, c3, LANES)), _const_spec((c3, LANES))],
        out_specs=[out, out],
        scratch_shapes=[pltpu.VMEM((c3, tile), BF16), pltpu.VMEM((c3, LANES), BF16)],
        compiler_params=_params("arbitrary", "arbitrary"),
        name="hyena_inproj_conv",
    )(x, sh, sc, g, w_t, wide(b_vec), wide(conv_w), wide(conv_b))


def _split(x):
    hi = x.astype(BF16)
    return hi, (x - hi.astype(F32)).astype(BF16)


def _dot3(a, b):
    a_hi, a_lo = _split(a)
    b_hi, b_lo = _split(b)
    d = functools.partial(jnp.dot, preferred_element_type=F32)
    return d(a_hi, b_hi) + (d(a_hi, b_lo) + d(a_lo, b_hi))


def _filter_taps_kernel(z_ref, t_ref, w1_ref, b1_ref, f1_ref, w2_ref, b2_ref, f2_ref, wo_ref, dec_ref,
                        k_ref, norm_ref, *, l):
    i = pl.program_id(0)
    h = jnp.sin(f1_ref[...] * (_dot3(w1_ref[...], z_ref[...]) + b1_ref[...]))
    h = jnp.sin(f2_ref[...] * (_dot3(w2_ref[...], h) + b2_ref[...]))
    win = jnp.exp(-t_ref[...] * jnp.abs(dec_ref[...]))
    n = z_ref.shape[1]
    slot = i * n + lax.broadcasted_iota(jnp.int32, (1, n), 1)
    k = jnp.where(slot == l, 0.0, _dot3(wo_ref[...], h) * win)

    @pl.when(i == 0)
    def _():
        norm_ref[...] = jnp.zeros_like(norm_ref)

    part = jnp.abs(k[:, :LANES])
    for j in range(1, n // LANES):
        part = part + jnp.abs(k[:, j * LANES:(j + 1) * LANES])
    norm_ref[...] += part
    k_ref[...] = k


def _filter_taps(zz_t, t_row, w1, b1, f1, w2, b2, f2, wout, dec, *, l, cols):
    e, n2 = zz_t.shape
    hid = w1.shape[1]
    c = dec.shape[0]
    nblk = n2 // cols
    col = lambda a: a.reshape(-1, 1)
    return pl.pallas_call(
        functools.partial(_filter_taps_kernel, l=l),
        out_shape=[jax.ShapeDtypeStruct((c, n2), F32), jax.ShapeDtypeStruct((c, LANES), F32)],
        grid=(nblk,),
        in_specs=[pl.BlockSpec((e, cols), lambda i: (0, i)), pl.BlockSpec((1, cols), lambda i: (0, i)),
                  _const_spec((hid, e)), _const_spec((hid, 1)), _const_spec((hid, 1)),
                  _const_spec((hid, hid)), _const_spec((hid, 1)), _const_spec((hid, 1)),
                  pl.BlockSpec((c, hid), lambda i: ((2 * i) // nblk, 0)), _const_spec((c, 1))],
        out_specs=[pl.BlockSpec((c, cols), lambda i: (0, i)), pl.BlockSpec((c, LANES), lambda i: (0, 0))],
        compiler_params=_params("arbitrary"),
        name="hyena_filter_taps",
    )(zz_t, t_row, w1.T, col(b1), col(f1), w2.T, col(b2), col(f2), wout.T, col(dec))


def _filter_fft_kernel(k_ref, norm_ref, f1_ref, twr_ref, twi_ref, g_ref, o_ref, s_ref, *, cb, n1, pitch, group):
    k = k_ref[...] / jnp.sum(norm_ref[...], axis=1, keepdims=True)
    for j in range(n1):
        s_ref[j * pitch:j * pitch + cb, :] = k[:, j * LANES:(j + 1) * LANES]
    tr, ti = twr_ref[...], twi_ref[...]

    def body(gi, carry):
        st = []
        for g in range(group):
            kc = s_ref[pl.ds(gi * group + g, n1, stride=pitch), :]
            ri = jnp.dot(f1_ref[...], kc.astype(BF16), preferred_element_type=F32)
            re, im = ri[:n1], ri[n1:]
            st.append(jnp.concatenate([re * tr - im * ti, re * ti + im * tr], axis=1))
        spec = jnp.dot(jnp.concatenate(st, axis=0).astype(BF16), g_ref[...], preferred_element_type=F32)
        for g in range(group):
            o_ref[gi * group + g] = spec[g * n1:(g + 1) * n1]
        return carry

    lax.fori_loop(0, cb // group, body, 0)


def _filter_fft(taps, norm, consts, *, cb, n1, group):
    c, n2 = taps.shape
    f1, twr, twi, g = consts
    pitch = cb + 8
    return pl.pallas_call(
        functools.partial(_filter_fft_kernel, cb=cb, n1=n1, pitch=pitch, group=group),
        out_shape=jax.ShapeDtypeStruct((c, n1, 2 * LANES), F32),
        grid=(c // cb,),
        in_specs=[pl.BlockSpec((cb, n2), lambda i: (i, 0)), pl.BlockSpec((cb, LANES), lambda i: (i, 0)),
                  _const_spec(f1.shape), _const_spec(twr.shape), _const_spec(twi.shape), _const_spec(g.shape)],
        out_specs=pl.BlockSpec((cb, n1, 2 * LANES), lambda i: (i, 0, 0)),
        scratch_shapes=[pltpu.VMEM((n1 * pitch, LANES), F32)],
        compiler_params=_params("arbitrary"),
        name="hyena_filter_fft",
    )(taps, norm, f1, twr, twi, g)


def _longconv_kernel(skip_ref, u_ref, x0_ref, kh_ref, a1_ref, twr_ref, twi_ref, g_ref, gc_ref, p_ref, o_ref,
                     s_ref, so_ref, *, cb, nh, n1, group, pitch):
    for bi in range(2):
        for j in range(nh):
            s_ref[bi, j * pitch:j * pitch + cb, :] = u_ref[bi, :, j * LANES:(j + 1) * LANES].astype(F32)
            s_ref[2 + bi, j * pitch:j * pitch + cb, :] = x0_ref[bi, :, j * LANES:(j + 1) * LANES].astype(F32)

    c_base = pl.program_id(0) * cb
    tr, ti = twr_ref[...], twi_ref[...]

    def body(gi, carry):
        parts = 2
        sub = group // parts
        chans = [[gi * group + h * sub + k for k in range(sub)] for h in range(parts)]
        us = [[jnp.concatenate([s_ref[0, pl.ds(ci, nh, stride=pitch), :],
                                s_ref[1, pl.ds(ci, nh, stride=pitch), :]], axis=0) for ci in cs]
              for cs in chans]
        first = [[jnp.dot(a1_ref[...], u.astype(BF16), preferred_element_type=F32) for u in uh]
                 for uh in us]
        spec = []
        for h in range(parts):
            st1 = [jnp.concatenate([ri[:n1] * tr - ri[n1:] * ti, ri[:n1] * ti + ri[n1:] * tr], axis=1)
                   for ri in first[h]]
            spec.append(jnp.dot(jnp.concatenate(st1, axis=0).astype(BF16), g_ref[...],
                                preferred_element_type=F32))
        back = []
        for h in range(parts):
            prods = []
            for k, ci in enumerate(chans[h]):
                xs = spec[h][k * n1:(k + 1) * n1]
                xr, xi = xs[:, :LANES], xs[:, LANES:]
                kk = kh_ref[ci]
                kr, ki = kk[:, :LANES], kk[:, LANES:]
                prods.append(jnp.concatenate([xr * kr - xi * ki, xr * ki + xi * kr], axis=1))
            back.append(jnp.dot(jnp.concatenate(prods, axis=0).astype(BF16), gc_ref[...],
                                preferred_element_type=F32))
        for h in range(parts):
            for k, ci in enumerate(chans[h]):
                bs = back[h][k * n1:(k + 1) * n1]
                br, bim = bs[:, :LANES], bs[:, LANES:]
                st = jnp.concatenate([br * tr + bim * ti, bim * tr - br * ti], axis=0)
                y = jnp.dot(p_ref[...], st.astype(BF16), preferred_element_type=F32)
                x0 = jnp.concatenate([s_ref[2, pl.ds(ci, nh, stride=pitch), :],
                                      s_ref[3, pl.ds(ci, nh, stride=pitch), :]], axis=0)
                gated = (y + us[h][k] * skip_ref[c_base + ci]) * x0
                so_ref[0, pl.ds(ci, nh, stride=pitch), :] = gated[:nh]
                so_ref[1, pl.ds(ci, nh, stride=pitch), :] = gated[nh:]
        return carry

    lax.fori_loop(0, cb // group, body, 0)
    for bi in range(2):
        for j in range(nh):
            o_ref[bi, :, j * LANES:(j + 1) * LANES] = so_ref[bi, j * pitch:j * pitch + cb, :].astype(BF16)


def _longconv(u_t, x0_t, khat, skip, consts, *, cb, nh, n1, group):
    b, c, l = u_t.shape
    a1, twr, twi, g, gc, p = consts
    pitch = cb + 8
    blk = pl.BlockSpec((2, cb, l), lambda i, j: (j, i, 0))
    return pl.pallas_call(
        functools.partial(_longconv_kernel, cb=cb, nh=nh, n1=n1, group=group, pitch=pitch),
        out_shape=jax.ShapeDtypeStruct((b, c, l), BF16),
        grid=(c // cb, b // 2),
        in_specs=[pl.BlockSpec(memory_space=pltpu.SMEM), blk, blk,
                  pl.BlockSpec((cb, n1, 2 * LANES), lambda i, j: (i, 0, 0)),
                  _const_spec(a1.shape), _const_spec(twr.shape), _const_spec(twi.shape),
                  _const_spec(g.shape), _const_spec(gc.shape), _const_spec(p.shape)],
        out_specs=blk,
        scratch_shapes=[pltpu.VMEM((4, nh * pitch, LANES), F32), pltpu.VMEM((2, nh * pitch, LANES), F32)],
        compiler_params=_params("arbitrary", "arbitrary"),
        name="hyena_longconv",
    )(skip, u_t, x0_t, khat, a1, twr, twi, g, gc, p)


def _dft_tables(l):
    n = 2 * l
    n1 = n // LANES
    nh = l // LANES
    f1 = np.exp(-2j * np.pi * np.outer(np.arange(n1), np.arange(n1)) / n1)
    f2 = np.exp(-2j * np.pi * np.outer(np.arange(LANES), np.arange(LANES)) / LANES)
    tw = np.exp(-2j * np.pi * np.outer(np.arange(n1), np.arange(LANES)) / n)
    fh = f1[:, :nh]
    a1 = np.block([[fh.real, -fh.imag], [fh.imag, fh.real]])
    g = np.block([[f2.real, f2.imag], [-f2.imag, f2.real]])
    gc = np.block([[f2.real, -f2.imag], [f2.imag, f2.real]])
    ci = np.conj(f1)[:nh, :]
    p = np.block([[ci.real, -ci.imag], [ci.imag, ci.real]]) / n
    f1full = np.concatenate([f1.real, f1.imag], axis=0)

    def bf(a):
        return jnp.asarray(a, F32).astype(BF16)

    data = (bf(a1), jnp.asarray(tw.real, F32), jnp.asarray(tw.imag, F32), bf(g), bf(gc), bf(p))
    filt = (bf(f1full), jnp.asarray(tw.real, F32), jnp.asarray(tw.imag, F32), bf(g))
    return data, filt, n1, nh


def _rope_tables_t(l):
    rows = l // GRID_W
    row = np.repeat(np.arange(rows, dtype=np.float64), GRID_W)
    col = np.tile(np.arange(GRID_W, dtype=np.float64), rows)
    inv_freq = ROPE_THETA ** (-np.arange(AXIS_FREQS, dtype=np.float64) / AXIS_FREQS)
    ang = np.concatenate([inv_freq[:, None] * row[None, :], inv_freq[:, None] * col[None, :]], axis=0)
    return jnp.asarray(np.cos(ang), F32), jnp.asarray(np.sin(ang), F32)


def _filter_positions(l):
    t = np.linspace(0.0, 1.0, l)
    w = 2.0 * math.pi * np.arange(l, dtype=np.float64) / l
    bands = np.linspace(1e-4, HY_BANDS - 1, HY_BANDS)[None, :]
    z = np.concatenate([t[:, None], np.cos(bands * w[:, None]), -np.sin(bands * w[:, None])], axis=-1)
    idx = np.concatenate([np.arange(l), np.zeros((1,), np.int64), np.arange(l - 1, 0, -1)])
    return jnp.asarray(z[idx].T, F32), jnp.asarray(t[idx][None, :], F32)


def kernel(x, c, ctx, c_ctx, ada_w, ada_b, norm1_g, norm2_g, attn_wqkv, attn_wo, attn_q_gain, attn_k_gain, attn_sink, hy_w_in, hy_b_in, hy_conv_w, hy_conv_b, hy_f_w1, hy_f_b1, hy_f_freq1, hy_f_w2, hy_f_b2, hy_f_freq2, hy_f_wout, hy_decay, hy_skip, hy_w_out, hy_b_out, ffn_w1, ffn_w3, ffn_w2):
    b, l, d = x.shape
    assert b % 2 == 0, "batch pairs share one complex long-conv transform"
    tile = min(512, l)
    ff = ffn_w1.shape[-1]
    ff_chunk = 1 * MXU_TILE

    pad = (-(b + 1)) % 8
    cond = jnp.concatenate([c, c_ctx[None, :], jnp.zeros((pad, d), F32)], axis=0)
    mod = _ada(cond, ada_w, ada_b)

    def chunks(i, rows):
        m = mod[i, rows][:, None, :]
        return [m[..., k * d:(k + 1) * d] for k in range(6)]

    row = lambda a: a.reshape(1, -1)
    col = lambda a: a.reshape(-1, 1)

    sh1, sc1, g1, sh2, sc2, g2 = chunks(0, slice(0, b))
    csh1, csc1 = [jnp.broadcast_to(m, (b, 1, d)) for m in chunks(0, slice(b, b + 1))[:2]]
    wqkv_t = attn_wqkv[0].T.astype(BF16)
    cos_t, sin_t = _rope_tables_t(l)
    qg, kg = col(attn_q_gain[0]), col(attn_k_gain[0])
    q_t, k_t, v_t = _qkv(x, sh1, sc1, row(norm1_g[0]), wqkv_t, qg, kg, cos_t, sin_t, n_q=N_HEADS, tile=tile)
    kx_t, vx_t = _qkv(ctx, csh1, csc1, row(norm1_g[0]), wqkv_t[N_HEADS * HEAD_DIM:], qg, kg, None, None,
                      n_q=0, tile=ctx.shape[1])
    o_t = _attention(attn_sink[0], q_t, k_t, v_t, kx_t, vx_t, qb=4)
    x = _proj_ffn(x, o_t, g1, sh2, sc2, g2, attn_wo[0].astype(BF16), jnp.zeros((1, d), F32), row(norm2_g[0]),
                  ffn_w1[0].astype(BF16), ffn_w3[0].astype(BF16), ffn_w2[0].astype(BF16),
                  tile=min(2 * tile, l), ff_chunk=ff_chunk)

    sh1, sc1, g1, sh2, sc2, g2 = chunks(1, slice(0, b))
    data_consts, filt_consts, n1, nh = _dft_tables(l)
    zz_t, t_row = _filter_positions(l)
    taps, norm = _filter_taps(zz_t, t_row, hy_f_w1[0], hy_f_b1[0], hy_f_freq1[0], hy_f_w2[0], hy_f_b2[0],
                              hy_f_freq2[0], hy_f_wout[0], hy_decay[0], l=l, cols=min(1024, l))
    khat = _filter_fft(taps, norm, filt_consts, cb=32, n1=n1, group=4)
    u_t, x0_t = _inproj(x, sh1, sc1, row(norm1_g[1]), hy_w_in[0].T.astype(BF16), hy_b_in[0],
                        hy_conv_w[0], hy_conv_b[0], tile=tile, slab=min(1024, d))
    gated = _longconv(u_t, x0_t, khat, hy_skip[0], data_consts, cb=64, nh=nh, n1=n1, group=16)
    x = _proj_ffn(x, gated, g1, sh2, sc2, g2, hy_w_out[0].astype(BF16), row(hy_b_out[0]),
                  row(norm2_g[1]), ffn_w1[1].astype(BF16), ffn_w3[1].astype(BF16), ffn_w2[1].astype(BF16),
                  tile=min(2 * tile, l), ff_chunk=ff_chunk)
    return x
```

```python
import functools
import math

import numpy as np
import jax
import jax.numpy as jnp
from jax import lax
from jax.experimental import pallas as pl
from jax.experimental.pallas import tpu as pltpu

N_HEADS = 16
N_KV_HEADS = 4
HEAD_DIM = 64
GROUP = N_HEADS // N_KV_HEADS
ROPE_HALF = HEAD_DIM // 2
AXIS_FREQS = ROPE_HALF // 2
BLOCK = 128
GRID_W = 64
ROPE_THETA = 10000.0
ATTN_SCALE = HEAD_DIM ** -0.5
LOG2E = math.log2(math.e)
HY_BANDS = 16
EPS = 1e-6
NEG = -1e30
LANES = 128
MXU_TILE = 256
VMEM_LIMIT = 56 * 1024 * 1024

F32 = jnp.float32
BF16 = jnp.bfloat16
NT = (((1,), (1,)), ((), ()))
TN = (((0,), (0,)), ((), ()))


def _params(*sem):
    return pltpu.CompilerParams(dimension_semantics=sem, vmem_limit_bytes=VMEM_LIMIT)


def _const_spec(shape):
    n = len(shape)
    return pl.BlockSpec(shape, lambda *_: (0,) * n, pipeline_mode=pl.Buffered(1))


def _norm_mod(x, g, sc, sh):
    y = x * lax.rsqrt(jnp.mean(x * x, axis=-1, keepdims=True) + EPS)
    return (y * g) * (1.0 + sc) + sh


def _ada_kernel(cond_ref, w_ref, b_ref, o_ref):
    cnd = cond_ref[...]
    s = (cnd * jax.nn.sigmoid(cnd)).astype(BF16)
    o_ref[0] = jnp.dot(s, w_ref[0].astype(BF16), preferred_element_type=F32) + b_ref[0]


def _ada(cond, ada_w, ada_b):
    depth, d, d6 = ada_w.shape
    r = cond.shape[0]
    cw = 1536
    return pl.pallas_call(
        _ada_kernel,
        out_shape=jax.ShapeDtypeStruct((depth, r, d6), F32),
        grid=(depth, d6 // cw),
        in_specs=[pl.BlockSpec((r, d), lambda i, j: (0, 0)),
                  pl.BlockSpec((1, d, cw), lambda i, j: (i, 0, j)),
                  pl.BlockSpec((1, 1, cw), lambda i, j: (i, 0, j))],
        out_specs=pl.BlockSpec((1, r, cw), lambda i, j: (i, 0, j)),
        compiler_params=_params("arbitrary", "arbitrary"),
        name="ada_mod",
    )(cond, ada_w, ada_b.reshape(depth, 1, d6))


def _head_norm(t, gain, n_heads):
    t3 = t.reshape(n_heads, HEAD_DIM, t.shape[-1])
    ms = jnp.mean(t3 * t3, axis=1, keepdims=True)
    return (t3 * lax.rsqrt(ms + EPS)) * gain[None]


def _rope(t3, cos, sin):
    x1, x2 = t3[:, :ROPE_HALF], t3[:, ROPE_HALF:]
    c, s = cos[None], sin[None]
    return jnp.concatenate([x1 * c - x2 * s, x1 * s + x2 * c], axis=1)


def _qkv_kernel(*refs, n_q, rope):
    if rope:
        x_ref, sh_ref, sc_ref, g_ref, w_ref, qg_ref, kg_ref, cos_ref, sin_ref = refs[:9]
        outs = refs[9:]
    else:
        x_ref, sh_ref, sc_ref, g_ref, w_ref, qg_ref, kg_ref = refs[:7]
        outs = refs[7:]
    h = _norm_mod(x_ref[0], g_ref[...], sc_ref[0], sh_ref[0]).astype(BF16)
    nq = n_q * HEAD_DIM
    nk = N_KV_HEADS * HEAD_DIM
    tt = h.shape[0]

    def proj(r0, r1):
        return lax.dot_general(w_ref[r0:r1, :], h, NT, preferred_element_type=F32)

    q_rows = 8 * HEAD_DIM
    slabs = [(r0, min(r0 + q_rows, nq)) for r0 in range(0, nq, q_rows)] + [(nq, nq + 2 * nk)]
    if n_q:
        q_ref, k_ref, v_ref = outs
    else:
        k_ref, v_ref = outs
    nxt = proj(*slabs[0])
    for i, (r0, r1) in enumerate(slabs):
        t = nxt
        if i + 1 < len(slabs):
            nxt = proj(*slabs[i + 1])
        if r0 < nq:
            q3 = _head_norm(t, qg_ref[...], (r1 - r0) // HEAD_DIM)
            if rope:
                q3 = _rope(q3, cos_ref[...], sin_ref[...])
            q_ref[0, r0:r1, :] = (q3 * (ATTN_SCALE * LOG2E)).reshape(r1 - r0, tt).astype(BF16)
        else:
            k3 = _head_norm(t[:nk], kg_ref[...], N_KV_HEADS)
            if rope:
                k3 = _rope(k3, cos_ref[...], sin_ref[...])
            k_ref[0] = k3.reshape(nk, tt).astype(BF16)
            v_ref[0] = t[nk:].astype(BF16)


def _qkv(x, sh, sc, g, w_t, qg, kg, cos_t, sin_t, *, n_q, tile):
    b, l, d = x.shape
    nq, nk = n_q * HEAD_DIM, N_KV_HEADS * HEAD_DIM
    rope = cos_t is not None
    vec = pl.BlockSpec((1, 1, d), lambda i, j: (i, 0, 0))
    in_specs = [pl.BlockSpec((1, tile, d), lambda i, j: (i, j, 0)), vec, vec,
                _const_spec((1, d)), _const_spec(w_t.shape),
                _const_spec((HEAD_DIM, 1)), _const_spec((HEAD_DIM, 1))]
    args = [x, sh, sc, g, w_t, qg, kg]
    if rope:
        in_specs += [pl.BlockSpec((ROPE_HALF, tile), lambda i, j: (0, j))] * 2
        args += [cos_t, sin_t]
    out_shape, out_specs = [], []
    for rows in ([nq] if n_q else []) + [nk, nk]:
        out_shape.append(jax.ShapeDtypeStruct((b, rows, l), BF16))
        out_specs.append(pl.BlockSpec((1, rows, tile), lambda i, j: (i, 0, j)))
    return pl.pallas_call(
        functools.partial(_qkv_kernel, n_q=n_q, rope=rope),
        out_shape=out_shape, grid=(b, l // tile), in_specs=in_specs, out_specs=out_specs,
        compiler_params=_params("arbitrary", "arbitrary"),
        name="qkv_proj" if n_q else "ctx_kv_proj",
    )(*args)


HEADS_PER_TILE = GROUP


def _scores(q_ref, cols, kt_ref, r0, ctx0, biases, kh, heads):
    q4 = jnp.concatenate([q_ref[0, h * HEAD_DIM:(h + 1) * HEAD_DIM, cols] for h in heads], axis=1)
    zeros = jnp.zeros_like(q4)
    qz = jnp.concatenate([q4, zeros] if kh % 2 == 0 else [zeros, q4], axis=0)
    lanes = slice((kh // 2) * LANES, (kh // 2 + 1) * LANES)
    s_loc = jnp.dot(kt_ref[r0:r0 + 3 * BLOCK, lanes], qz, preferred_element_type=F32)
    s_ctx = jnp.dot(kt_ref[ctx0:, lanes], qz, preferred_element_type=F32)
    return [s_loc[:BLOCK] + biases[0], s_loc[BLOCK:2 * BLOCK], s_loc[2 * BLOCK:] + biases[1], s_ctx]


def _softmax_pv(sink_ref, o_ref, cols, vparts, kh, heads, parts, ones_rows):
    rows = slice(kh * HEAD_DIM, (kh + 1) * HEAD_DIM)
    sink = jnp.concatenate([jnp.full((1, BLOCK), sink_ref[h] * LOG2E, F32) for h in heads], axis=1)
    m = sink
    for part in parts:
        m = jnp.maximum(m, jnp.max(part, axis=0, keepdims=True))
    p = jnp.concatenate([jnp.exp2(part - m).astype(BF16) for part in parts], axis=0)
    vaug = jnp.concatenate([vp(rows) for vp in vparts], axis=1)
    vaug = jnp.concatenate([vaug, ones_rows], axis=0)
    o_aug = jnp.dot(vaug, p, preferred_element_type=F32)
    denom = jnp.exp2(sink - m) + o_aug[HEAD_DIM:HEAD_DIM + 1]
    o4 = o_aug[:HEAD_DIM] / denom
    for g, h in enumerate(heads):
        o_ref[0, h * HEAD_DIM:(h + 1) * HEAD_DIM, cols] = o4[:, g * BLOCK:(g + 1) * BLOCK].astype(BF16)


def _attn_kernel(sink_ref, q_ref, kp_ref, km_ref, kn_ref, vp_ref, vm_ref, vn_ref, kx_ref, vx_ref, o_ref, kt_ref,
                 *, qb):
    j = pl.program_id(1)
    nj = pl.num_programs(1)
    wq = HEADS_PER_TILE * BLOCK
    key = lax.broadcasted_iota(jnp.int32, (BLOCK, wq), 0)
    qry = lax.broadcasted_iota(jnp.int32, (BLOCK, wq), 1) % BLOCK
    band_prev = jnp.where(key >= qry, 0.0, NEG)
    band_next = jnp.where(key <= qry, 0.0, NEG)
    blk = lambda i: slice(i * BLOCK, (i + 1) * BLOCK)
    kt_ref[0:BLOCK, :] = kp_ref[0].T
    kt_ref[BLOCK:(qb + 1) * BLOCK, :] = km_ref[0].T
    kt_ref[(qb + 1) * BLOCK:(qb + 2) * BLOCK, :] = kn_ref[0].T
    ctx0 = (qb + 2) * BLOCK
    kt_ref[ctx0:, :] = kx_ref[0].T
    lc = kx_ref.shape[2]
    ones_rows = jnp.where(lax.broadcasted_iota(jnp.int32, (16, 3 * BLOCK + lc), 0) == 0, 1.0, 0.0).astype(BF16)

    def local_v(i):
        refs = [(lambda r: vp_ref[0, r]) if i == 0 else (lambda r, i=i: vm_ref[0, r, blk(i - 1)]),
                lambda r, i=i: vm_ref[0, r, blk(i)],
                (lambda r: vn_ref[0, r]) if i == qb - 1 else (lambda r, i=i: vm_ref[0, r, blk(i + 1)])]
        return refs + [lambda r: vx_ref[0, r]]

    work = []
    for i in range(qb):
        b_prev = jnp.where(j > 0, band_prev, NEG) if i == 0 else band_prev
        b_next = jnp.where(j < nj - 1, band_next, NEG) if i == qb - 1 else band_next
        for kh in range(N_KV_HEADS):
            for t in range(GROUP // HEADS_PER_TILE):
                heads = [kh * GROUP + t * HEADS_PER_TILE + g for g in range(HEADS_PER_TILE)]
                work.append((blk(i), i * BLOCK, local_v(i), (b_prev, b_next), kh, heads))
    score = lambda w: _scores(q_ref, w[0], kt_ref, w[1], ctx0, w[3], w[4], w[5])
    nxt = score(work[0])
    for n, w in enumerate(work):
        parts = nxt
        if n + 1 < len(work):
            nxt = score(work[n + 1])
        _softmax_pv(sink_ref, o_ref, w[0], w[2], w[4], w[5], parts, ones_rows)


def _attention(sink, q_t, k_t, v_t, kx_t, vx_t, *, qb):
    b, dq, l = q_t.shape
    nk = k_t.shape[1]
    lc = kx_t.shape[2]
    nb = l // BLOCK
    prev = pl.BlockSpec((1, nk, BLOCK), lambda i, j: (i, 0, jnp.maximum(qb * j - 1, 0)))
    mid = pl.BlockSpec((1, nk, qb * BLOCK), lambda i, j: (i, 0, j))
    nxt = pl.BlockSpec((1, nk, BLOCK), lambda i, j: (i, 0, jnp.minimum(qb * j + qb, nb - 1)))
    ctx = pl.BlockSpec((1, nk, lc), lambda i, j: (i, 0, 0))
    return pl.pallas_call(
        functools.partial(_attn_kernel, qb=qb),
        out_shape=jax.ShapeDtypeStruct((b, dq, l), BF16),
        grid=(b, nb // qb),
        in_specs=[pl.BlockSpec(memory_space=pltpu.SMEM),
                  pl.BlockSpec((1, dq, qb * BLOCK), lambda i, j: (i, 0, j)),
                  prev, mid, nxt, prev, mid, nxt, ctx, ctx],
        out_specs=pl.BlockSpec((1, dq, qb * BLOCK), lambda i, j: (i, 0, j)),
        scratch_shapes=[pltpu.VMEM(((qb + 2) * BLOCK + lc, nk), BF16)],
        compiler_params=_params("arbitrary", "arbitrary"),
        name="band_attention",
    )(sink, q_t, k_t, k_t, k_t, v_t, v_t, v_t, kx_t, vx_t)


def _proj_ffn_kernel(x_ref, a_ref, g1_ref, sh2_ref, sc2_ref, g2_ref, wo_ref, bo_ref, n2_ref,
                     w1_ref, w3_ref, w2_ref, o_ref, *, ff_chunk):
    proj = lax.dot_general(a_ref[0], wo_ref[...], TN, preferred_element_type=F32)
    x1 = x_ref[0] + g1_ref[0] * (proj + bo_ref[...])
    h = _norm_mod(x1, n2_ref[...], sc2_ref[0], sh2_ref[0]).astype(BF16)

    def up(c0, c1):
        return (jnp.dot(h, w1_ref[:, c0:c1], preferred_element_type=F32),
                jnp.dot(h, w3_ref[:, c0:c1], preferred_element_type=F32))

    ff = w1_ref.shape[1]
    spans = [(c0, min(c0 + ff_chunk, ff)) for c0 in range(0, ff, ff_chunk)]
    acc, nxt = None, up(*spans[0])
    for i, (c0, c1) in enumerate(spans):
        a, b = nxt
        if i + 1 < len(spans):
            nxt = up(*spans[i + 1])
        act = ((a * jax.nn.sigmoid(a)) * b).astype(BF16)
        part = jnp.dot(act, w2_ref[c0:c1, :], preferred_element_type=F32)
        acc = part if acc is None else acc + part
    o_ref[0] = x1 + g2_ref[0] * acc


def _proj_ffn(x, a_t, g1, sh2, sc2, g2, wo, bo, n2, w1, w3, w2, *, tile, ff_chunk):
    b, l, d = x.shape
    c = a_t.shape[1]
    vec = pl.BlockSpec((1, 1, d), lambda i, j: (i, 0, 0))
    return pl.pallas_call(
        functools.partial(_proj_ffn_kernel, ff_chunk=ff_chunk),
        out_shape=jax.ShapeDtypeStruct((b, l, d), F32),
        grid=(b, l // tile),
        in_specs=[pl.BlockSpec((1, tile, d), lambda i, j: (i, j, 0)),
                  pl.BlockSpec((1, c, tile), lambda i, j: (i, 0, j)),
                  vec, vec, vec, vec,
                  _const_spec(wo.shape), _const_spec((1, d)), _const_spec((1, d)),
                  _const_spec(w1.shape), _const_spec(w3.shape), _const_spec(w2.shape)],
        out_specs=pl.BlockSpec((1, tile, d), lambda i, j: (i, j, 0)),
        compiler_params=_params("arbitrary", "arbitrary"),
        name="proj_swiglu",
    )(x, a_t, g1, sh2, sc2, g2, wo, bo, n2, w1, w3, w2)


def _inproj_kernel(x_ref, sh_ref, sc_ref, g_ref, w_ref, b_ref, cw_ref, cb_ref, u_ref, x0_ref, zc_ref, lcol_ref,
                   *, slab):
    j = pl.program_id(1)
    nt = pl.num_programs(1) - 1
    d = u_ref.shape[1]
    tt = zc_ref.shape[1]
    nl = tt // LANES

    @pl.when(j == 0)
    def _():
        zc_ref[...] = jnp.zeros_like(zc_ref)
        lcol_ref[...] = jnp.zeros_like(lcol_ref)

    h = _norm_mod(x_ref[0], g_ref[...], sc_ref[0], sh_ref[0]).astype(BF16)
    sub = min(slab, LANES)
    lane = lax.broadcasted_iota(jnp.int32, (sub, LANES), 1)
    first, last = lane == 0, lane == LANES - 1
    has_next = j < nt
    zero = jnp.zeros((sub, LANES), BF16)

    def proj(c0):
        outs = []
        for p in range(3):
            rows = slice(p * d + c0, p * d + c0 + slab)
            z = lax.dot_general(w_ref[rows, :], h, NT, preferred_element_type=F32)
            bias = b_ref[rows, :]
            outs.append(jnp.concatenate([z[:, k * LANES:(k + 1) * LANES] + bias for k in range(nl)],
                                        axis=1).astype(BF16))
        return outs

    def conv_rows(p, r0, znew_first):
        rows = slice(p * d + r0, p * d + r0 + sub)
        w0, w1, w2 = [cw_ref[k, rows, :] for k in range(3)]
        bias = cb_ref[rows, :]
        z = [zc_ref[rows, k * LANES:(k + 1) * LANES] for k in range(nl)]
        right = jnp.where(has_next, znew_first, zero)
        fwd = [pltpu.roll(t, 1, axis=1) for t in [lcol_ref[rows, :]] + z]
        bwd = [pltpu.roll(t, LANES - 1, axis=1) for t in z + [right]]
        tiles = []
        for k in range(nl):
            prev = jnp.where(first, fwd[k], fwd[k + 1])
            nxt = jnp.where(last, bwd[k + 1], bwd[k])
            tiles.append(w0 * prev.astype(F32) + w1 * z[k].astype(F32) + w2 * nxt.astype(F32) + bias)
        lcol_ref[rows, :] = z[nl - 1]
        return tiles

    def finish(c0, znew):
        for s0 in range(0, slab, sub):
            r0 = c0 + s0
            x1 = conv_rows(1, r0, znew[1][s0:s0 + sub, :LANES])
            v = conv_rows(2, r0, znew[2][s0:s0 + sub, :LANES])
            for k in range(nl):
                u_ref[0, r0:r0 + sub, k * LANES:(k + 1) * LANES] = (v[k] * x1[k]).astype(BF16)
            x0 = conv_rows(0, r0, znew[0][s0:s0 + sub, :LANES])
            for k in range(nl):
                x0_ref[0, r0:r0 + sub, k * LANES:(k + 1) * LANES] = x0[k].astype(BF16)
        for p in range(3):
            zc_ref[p * d + c0:p * d + c0 + slab, :] = znew[p]

    starts = list(range(0, d, slab))
    nxt = proj(starts[0])
    for i, c0 in enumerate(starts):
        znew = nxt
        if i + 1 < len(starts):
            nxt = proj(starts[i + 1])
        finish(c0, znew)


def _inproj(x, sh, sc, g, w_t, b_vec, conv_w, conv_b, *, tile, slab):
    b, l, d = x.shape
    c3 = w_t.shape[0]
    c = c3 // 3
    nt = l // tile
    vec = pl.BlockSpec((1, 1, d), lambda i, j: (i, 0, 0))
    out = pl.BlockSpec((1, c, tile), lambda i, j: (i, 0, jnp.maximum(j - 1, 0)))
    wide = lambda a: jnp.broadcast_to(a[..., None], a.shape + (LANES,))
    return pl.pallas_call(
        functools.partial(_inproj_kernel, slab=slab),
        out_shape=[jax.ShapeDtypeStruct((b, c, l), BF16), jax.ShapeDtypeStruct((b, c, l), BF16)],
        grid=(b, nt + 1),
        in_specs=[pl.BlockSpec((1, tile, d), lambda i, j: (i, jnp.minimum(j, nt - 1), 0)), vec, vec,
                  _const_spec((1, d)), _const_spec(w_t.shape), _const_spec((c3, LANES)),
                  _const_spec((3, c3, LANES)), _const_spec((c3, LANES))],
        out_specs=[out, out],
        scratch_shapes=[pltpu.VMEM((c3, tile), BF16), pltpu.VMEM((c3, LANES), BF16)],
        compiler_params=_params("arbitrary", "arbitrary"),
        name="hyena_inproj_conv",
    )(x, sh, sc, g, w_t, wide(b_vec), wide(conv_w), wide(conv_b))


def _split(x):
    hi = x.astype(BF16)
    return hi, (x - hi.astype(F32)).astype(BF16)


def _dot3(a, b):
    a_hi, a_lo = _split(a)
    b_hi, b_lo = _split(b)
    d = functools.partial(jnp.dot, preferred_element_type=F32)
    return d(a_hi, b_hi) + (d(a_hi, b_lo) + d(a_lo, b_hi))


def _filter_taps_kernel(z_ref, t_ref, w1_ref, b1_ref, f1_ref, w2_ref, b2_ref, f2_ref, wo_ref, dec_ref,
                        k_ref, norm_ref, *, l):
    i = pl.program_id(0)
    h = jnp.sin(f1_ref[...] * (_dot3(w1_ref[...], z_ref[...]) + b1_ref[...]))
    h = jnp.sin(f2_ref[...] * (_dot3(w2_ref[...], h) + b2_ref[...]))
    win = jnp.exp(-t_ref[...] * jnp.abs(dec_ref[...]))
    n = z_ref.shape[1]
    slot = i * n + lax.broadcasted_iota(jnp.int32, (1, n), 1)
    k = jnp.where(slot == l, 0.0, _dot3(wo_ref[...], h) * win)

    @pl.when(i == 0)
    def _():
        norm_ref[...] = jnp.zeros_like(norm_ref)

    part = jnp.abs(k[:, :LANES])
    for j in range(1, n // LANES):
        part = part + jnp.abs(k[:, j * LANES:(j + 1) * LANES])
    norm_ref[...] += part
    k_ref[...] = k


def _filter_taps(zz_t, t_row, w1, b1, f1, w2, b2, f2, wout, dec, *, l, cols):
    e, n2 = zz_t.shape
    hid = w1.shape[1]
    c = dec.shape[0]
    nblk = n2 // cols
    col = lambda a: a.reshape(-1, 1)
    return pl.pallas_call(
        functools.partial(_filter_taps_kernel, l=l),
        out_shape=[jax.ShapeDtypeStruct((c, n2), F32), jax.ShapeDtypeStruct((c, LANES), F32)],
        grid=(nblk,),
        in_specs=[pl.BlockSpec((e, cols), lambda i: (0, i)), pl.BlockSpec((1, cols), lambda i: (0, i)),
                  _const_spec((hid, e)), _const_spec((hid, 1)), _const_spec((hid, 1)),
                  _const_spec((hid, hid)), _const_spec((hid, 1)), _const_spec((hid, 1)),
                  pl.BlockSpec((c, hid), lambda i: ((2 * i) // nblk, 0)), _const_spec((c, 1))],
        out_specs=[pl.BlockSpec((c, cols), lambda i: (0, i)), pl.BlockSpec((c, LANES), lambda i: (0, 0))],
        compiler_params=_params("arbitrary"),
        name="hyena_filter_taps",
    )(zz_t, t_row, w1.T, col(b1), col(f1), w2.T, col(b2), col(f2), wout.T, col(dec))


def _filter_fft_kernel(k_ref, norm_ref, f1_ref, twr_ref, twi_ref, g_ref, o_ref, s_ref, *, cb, n1, pitch, group):
    k = k_ref[...] / jnp.sum(norm_ref[...], axis=1, keepdims=True)
    for j in range(n1):
        s_ref[j * pitch:j * pitch + cb, :] = k[:, j * LANES:(j + 1) * LANES]
    tr, ti = twr_ref[...], twi_ref[...]

    def body(gi, carry):
        st = []
        for g in range(group):
            kc = s_ref[pl.ds(gi * group + g, n1, stride=pitch), :]
            ri = jnp.dot(f1_ref[...], kc.astype(BF16), preferred_element_type=F32)
            re, im = ri[:n1], ri[n1:]
            st.append(jnp.concatenate([re * tr - im * ti, re * ti + im * tr], axis=1))
        spec = jnp.dot(jnp.concatenate(st, axis=0).astype(BF16), g_ref[...], preferred_element_type=F32)
        for g in range(group):
            o_ref[gi * group + g] = spec[g * n1:(g + 1) * n1]
        return carry

    lax.fori_loop(0, cb // group, body, 0)


def _filter_fft(taps, norm, consts, *, cb, n1, group):
    c, n2 = taps.shape
    f1, twr, twi, g = consts
    pitch = cb + 8
    return pl.pallas_call(
        functools.partial(_filter_fft_kernel, cb=cb, n1=n1, pitch=pitch, group=group),
        out_shape=jax.ShapeDtypeStruct((c, n1, 2 * LANES), F32),
        grid=(c // cb,),
        in_specs=[pl.BlockSpec((cb, n2), lambda i: (i, 0)), pl.BlockSpec((cb, LANES), lambda i: (i, 0)),
                  _const_spec(f1.shape), _const_spec(twr.shape), _const_spec(twi.shape), _const_spec(g.shape)],
        out_specs=pl.BlockSpec((cb, n1, 2 * LANES), lambda i: (i, 0, 0)),
        scratch_shapes=[pltpu.VMEM((n1 * pitch, LANES), F32)],
        compiler_params=_params("arbitrary"),
        name="hyena_filter_fft",
    )(taps, norm, f1, twr, twi, g)


def _longconv_kernel(skip_ref, u_ref, x0_ref, kh_ref, a1_ref, twr_ref, twi_ref, g_ref, gc_ref, p_ref, o_ref,
                     s_ref, so_ref, *, cb, nh, n1, group, pitch):
    for bi in range(2):
        for j in range(nh):
            s_ref[bi, j * pitch:j * pitch + cb, :] = u_ref[bi, :, j * LANES:(j + 1) * LANES].astype(F32)
            s_ref[2 + bi, j * pitch:j * pitch + cb, :] = x0_ref[bi, :, j * LANES:(j + 1) * LANES].astype(F32)

    c_base = pl.program_id(0) * cb
    tr, ti = twr_ref[...], twi_ref[...]

    def body(gi, carry):
        parts = 2
        sub = group // parts
        chans = [[gi * group + h * sub + k for k in range(sub)] for h in range(parts)]
        us = [[jnp.concatenate([s_ref[0, pl.ds(ci, nh, stride=pitch), :],
                                s_ref[1, pl.ds(ci, nh, stride=pitch), :]], axis=0) for ci in cs]
              for cs in chans]
        first = [[jnp.dot(a1_ref[...], u.astype(BF16), preferred_element_type=F32) for u in uh]
                 for uh in us]
        spec = []
        for h in range(parts):
            st1 = [jnp.concatenate([ri[:n1] * tr - ri[n1:] * ti, ri[:n1] * ti + ri[n1:] * tr], axis=1)
                   for ri in first[h]]
            spec.append(jnp.dot(jnp.concatenate(st1, axis=0).astype(BF16), g_ref[...],
                                preferred_element_type=F32))
        back = []
        for h in range(parts):
            prods = []
            for k, ci in enumerate(chans[h]):
                xs = spec[h][k * n1:(k + 1) * n1]
                xr, xi = xs[:, :LANES], xs[:, LANES:]
                kk = kh_ref[ci]
                kr, ki = kk[:, :LANES], kk[:, LANES:]
                prods.append(jnp.concatenate([xr * kr - xi * ki, xr * ki + xi * kr], axis=1))
            back.append(jnp.dot(jnp.concatenate(prods, axis=0).astype(BF16), gc_ref[...],
                                preferred_element_type=F32))
        for h in range(parts):
            for k, ci in enumerate(chans[h]):
                bs = back[h][k * n1:(k + 1) * n1]
                br, bim = bs[:, :LANES], bs[:, LANES:]
                st = jnp.concatenate([br * tr + bim * ti, bim * tr - br * ti], axis=0)
                y = jnp.dot(p_ref[...], st.astype(BF16), preferred_element_type=F32)
                x0 = jnp.concatenate([s_ref[2, pl.ds(ci, nh, stride=pitch), :],
                                      s_ref[3, pl.ds(ci, nh, stride=pitch), :]], axis=0)
                gated = (y + us[h][k] * skip_ref[c_base + ci]) * x0
                so_ref[0, pl.ds(ci, nh, stride=pitch), :] = gated[:nh]
                so_ref[1, pl.ds(ci, nh, stride=pitch), :] = gated[nh:]
        return carry

    lax.fori_loop(0, cb // group, body, 0)
    for bi in range(2):
        for j in range(nh):
            o_ref[bi, :, j * LANES:(j + 1) * LANES] = so_ref[bi, j * pitch:j * pitch + cb, :].astype(BF16)


def _longconv(u_t, x0_t, khat, skip, consts, *, cb, nh, n1, group):
    b, c, l = u_t.shape
    a1, twr, twi, g, gc, p = consts
    pitch = cb + 8
    blk = pl.BlockSpec((2, cb, l), lambda i, j: (j, i, 0))
    return pl.pallas_call(
        functools.partial(_longconv_kernel, cb=cb, nh=nh, n1=n1, group=group, pitch=pitch),
        out_shape=jax.ShapeDtypeStruct((b, c, l), BF16),
        grid=(c // cb, b // 2),
        in_specs=[pl.BlockSpec(memory_space=pltpu.SMEM), blk, blk,
                  pl.BlockSpec((cb, n1, 2 * LANES), lambda i, j: (i, 0, 0)),
                  _const_spec(a1.shape), _const_spec(twr.shape), _const_spec(twi.shape),
                  _const_spec(g.shape), _const_spec(gc.shape), _const_spec(p.shape)],
        out_specs=blk,
        scratch_shapes=[pltpu.VMEM((4, nh * pitch, LANES), F32), pltpu.VMEM((2, nh * pitch, LANES), F32)],
        compiler_params=_params("arbitrary", "arbitrary"),
        name="hyena_longconv",
    )(skip, u_t, x0_t, khat, a1, twr, twi, g, gc, p)


def _dft_tables(l):
    n = 2 * l
    n1 = n // LANES
    nh = l // LANES
    f1 = np.exp(-2j * np.pi * np.outer(np.arange(n1), np.arange(n1)) / n1)
    f2 = np.exp(-2j * np.pi * np.outer(np.arange(LANES), np.arange(LANES)) / LANES)
    tw = np.exp(-2j * np.pi * np.outer(np.arange(n1), np.arange(LANES)) / n)
    fh = f1[:, :nh]
    a1 = np.block([[fh.real, -fh.imag], [fh.imag, fh.real]])
    g = np.block([[f2.real, f2.imag], [-f2.imag, f2.real]])
    gc = np.block([[f2.real, -f2.imag], [f2.imag, f2.real]])
    ci = np.conj(f1)[:nh, :]
    p = np.block([[ci.real, -ci.imag], [ci.imag, ci.real]]) / n
    f1full = np.concatenate([f1.real, f1.imag], axis=0)

    def bf(a):
        return jnp.asarray(a, F32).astype(BF16)

    data = (bf(a1), jnp.asarray(tw.real, F32), jnp.asarray(tw.imag, F32), bf(g), bf(gc), bf(p))
    filt = (bf(f1full), jnp.asarray(tw.real, F32), jnp.asarray(tw.imag, F32), bf(g))
    return data, filt, n1, nh


def _rope_tables_t(l):
    rows = l // GRID_W
    row = np.repeat(np.arange(rows, dtype=np.float64), GRID_W)
    col = np.tile(np.arange(GRID_W, dtype=np.float64), rows)
    inv_freq = ROPE_THETA ** (-np.arange(AXIS_FREQS, dtype=np.float64) / AXIS_FREQS)
    ang = np.concatenate([inv_freq[:, None] * row[None, :], inv_freq[:, None] * col[None, :]], axis=0)
    return jnp.asarray(np.cos(ang), F32), jnp.asarray(np.sin(ang), F32)


def _filter_positions(l):
    t = np.linspace(0.0, 1.0, l)
    w = 2.0 * math.pi * np.arange(l, dtype=np.float64) / l
    bands = np.linspace(1e-4, HY_BANDS - 1, HY_BANDS)[None, :]
    z = np.concatenate([t[:, None], np.cos(bands * w[:, None]), -np.sin(bands * w[:, None])], axis=-1)
    idx = np.concatenate([np.arange(l), np.zeros((1,), np.int64), np.arange(l - 1, 0, -1)])
    return jnp.asarray(z[idx].T, F32), jnp.asarray(t[idx][None, :], F32)


def kernel(x, c, ctx, c_ctx, ada_w, ada_b, norm1_g, norm2_g, attn_wqkv, attn_wo, attn_q_gain, attn_k_gain, attn_sink, hy_w_in, hy_b_in, hy_conv_w, hy_conv_b, hy_f_w1, hy_f_b1, hy_f_freq1, hy_f_w2, hy_f_b2, hy_f_freq2, hy_f_wout, hy_decay, hy_skip, hy_w_out, hy_b_out, ffn_w1, ffn_w3, ffn_w2):
    b, l, d = x.shape
    assert b % 2 == 0, "batch pairs share one complex long-conv transform"
    tile = min(512, l)
    ff = ffn_w1.shape[-1]
    ff_chunk = 1 * MXU_TILE

    pad = (-(b + 1)) % 8
    cond = jnp.concatenate([c, c_ctx[None, :], jnp.zeros((pad, d), F32)], axis=0)
    mod = _ada(cond, ada_w, ada_b)

    def chunks(i, rows):
        m = mod[i, rows][:, None, :]
        return [m[..., k * d:(k + 1) * d] for k in range(6)]

    row = lambda a: a.reshape(1, -1)
    col = lambda a: a.reshape(-1, 1)

    sh1, sc1, g1, sh2, sc2, g2 = chunks(0, slice(0, b))
    csh1, csc1 = [jnp.broadcast_to(m, (b, 1, d)) for m in chunks(0, slice(b, b + 1))[:2]]
    wqkv_t = attn_wqkv[0].T.astype(BF16)
    cos_t, sin_t = _rope_tables_t(l)
    qg, kg = col(attn_q_gain[0]), col(attn_k_gain[0])
    q_t, k_t, v_t = _qkv(x, sh1, sc1, row(norm1_g[0]), wqkv_t, qg, kg, cos_t, sin_t, n_q=N_HEADS, tile=min(2 * tile, l))
    kx_t, vx_t = _qkv(ctx, csh1, csc1, row(norm1_g[0]), wqkv_t[N_HEADS * HEAD_DIM:], qg, kg, None, None,
                      n_q=0, tile=ctx.shape[1])
    o_t = _attention(attn_sink[0], q_t, k_t, v_t, kx_t, vx_t, qb=4)
    x = _proj_ffn(x, o_t, g1, sh2, sc2, g2, attn_wo[0].astype(BF16), jnp.zeros((1, d), F32), row(norm2_g[0]),
                  ffn_w1[0].astype(BF16), ffn_w3[0].astype(BF16), ffn_w2[0].astype(BF16),
                  tile=min(2 * tile, l), ff_chunk=ff_chunk)

    sh1, sc1, g1, sh2, sc2, g2 = chunks(1, slice(0, b))
    data_consts, filt_consts, n1, nh = _dft_tables(l)
    zz_t, t_row = _filter_positions(l)
    taps, norm = _filter_taps(zz_t, t_row, hy_f_w1[0], hy_f_b1[0], hy_f_freq1[0], hy_f_w2[0], hy_f_b2[0],
                              hy_f_freq2[0], hy_f_wout[0], hy_decay[0], l=l, cols=min(1024, l))
    khat = _filter_fft(taps, norm, filt_consts, cb=32, n1=n1, group=4)
    u_t, x0_t = _inproj(x, sh1, sc1, row(norm1_g[1]), hy_w_in[0].T.astype(BF16), hy_b_in[0],
                        hy_conv_w[0], hy_conv_b[0], tile=tile, slab=min(1024, d))
    gated = _longconv(u_t, x0_t, khat, hy_skip[0], data_consts, cb=64, nh=nh, n1=n1, group=16)
    x = _proj_ffn(x, gated, g1, sh2, sc2, g2, hy_w_out[0].astype(BF16), row(hy_b_out[0]),
                  row(norm2_g[1]), ffn_w1[1].astype(BF16), ffn_w3[1].astype(BF16), ffn_w2[1].astype(BF16),
                  tile=min(2 * tile, l), ff_chunk=ff_chunk)
    return x
```

```python
import functools
import math

import numpy as np
import jax
import jax.numpy as jnp
from jax import lax
from jax.experimental import pallas as pl
from jax.experimental.pallas import tpu as pltpu

N_HEADS = 16
N_KV_HEADS = 4
HEAD_DIM = 64
GROUP = N_HEADS // N_KV_HEADS
ROPE_HALF = HEAD_DIM // 2
AXIS_FREQS = ROPE_HALF // 2
BLOCK = 128
GRID_W = 64
ROPE_THETA = 10000.0
ATTN_SCALE = HEAD_DIM ** -0.5
LOG2E = math.log2(math.e)
HY_BANDS = 16
EPS = 1e-6
NEG = -1e30
LANES = 128
SUBLANES = 8
BF16_ROWS = 16
MXU_TILE = 256
VMEM_LIMIT = 56 * 1024 * 1024

F32 = jnp.float32
BF16 = jnp.bfloat16
NT = (((1,), (1,)), ((), ()))
TN = (((0,), (0,)), ((), ()))


def _params(*sem):
    return pltpu.CompilerParams(dimension_semantics=sem, vmem_limit_bytes=VMEM_LIMIT)


def _const_spec(shape):
    n = len(shape)
    return pl.BlockSpec(shape, lambda *_: (0,) * n, pipeline_mode=pl.Buffered(1))


def _norm_mod(x, g, sc, sh):
    y = x * lax.rsqrt(jnp.mean(x * x, axis=-1, keepdims=True) + EPS)
    return (y * g) * (1.0 + sc) + sh


def _ada_kernel(cond_ref, w_ref, b_ref, o_ref):
    cnd = cond_ref[...]
    s = (cnd * jax.nn.sigmoid(cnd)).astype(BF16)
    o_ref[0] = jnp.dot(s, w_ref[0].astype(BF16), preferred_element_type=F32) + b_ref[0]


def _ada(cond, ada_w, ada_b):
    depth, d, d6 = ada_w.shape
    r = cond.shape[0]
    cw = d6 // 4
    return pl.pallas_call(
        _ada_kernel,
        out_shape=jax.ShapeDtypeStruct((depth, r, d6), F32),
        grid=(depth, d6 // cw),
        in_specs=[pl.BlockSpec((r, d), lambda i, j: (0, 0)),
                  pl.BlockSpec((1, d, cw), lambda i, j: (i, 0, j)),
                  pl.BlockSpec((1, 1, cw), lambda i, j: (i, 0, j))],
        out_specs=pl.BlockSpec((1, r, cw), lambda i, j: (i, 0, j)),
        compiler_params=_params("arbitrary", "arbitrary"),
        name="ada_mod",
    )(cond, ada_w, ada_b.reshape(depth, 1, d6))


def _head_norm(t, gain, n_heads):
    t3 = t.reshape(n_heads, HEAD_DIM, t.shape[-1])
    ms = jnp.mean(t3 * t3, axis=1, keepdims=True)
    return (t3 * lax.rsqrt(ms + EPS)) * gain[None]


def _rope(t3, cos, sin):
    x1, x2 = t3[:, :ROPE_HALF], t3[:, ROPE_HALF:]
    c, s = cos[None], sin[None]
    return jnp.concatenate([x1 * c - x2 * s, x1 * s + x2 * c], axis=1)


def _qkv_kernel(*refs, n_q, rope):
    if rope:
        x_ref, sh_ref, sc_ref, g_ref, w_ref, qg_ref, kg_ref, cos_ref, sin_ref = refs[:9]
        outs = refs[9:]
    else:
        x_ref, sh_ref, sc_ref, g_ref, w_ref, qg_ref, kg_ref = refs[:7]
        outs = refs[7:]
    h = _norm_mod(x_ref[0], g_ref[...], sc_ref[0], sh_ref[0]).astype(BF16)
    nq = n_q * HEAD_DIM
    nk = N_KV_HEADS * HEAD_DIM
    tt = h.shape[0]

    def proj(r0, r1):
        return lax.dot_general(w_ref[r0:r1, :], h, NT, preferred_element_type=F32)

    q_rows = 8 * HEAD_DIM
    slabs = [(r0, min(r0 + q_rows, nq)) for r0 in range(0, nq, q_rows)] + [(nq, nq + 2 * nk)]
    if n_q:
        q_ref, k_ref, v_ref = outs
    else:
        k_ref, v_ref = outs
    nxt = proj(*slabs[0])
    for i, (r0, r1) in enumerate(slabs):
        t = nxt
        if i + 1 < len(slabs):
            nxt = proj(*slabs[i + 1])
        if r0 < nq:
            q3 = _head_norm(t, qg_ref[...], (r1 - r0) // HEAD_DIM)
            if rope:
                q3 = _rope(q3, cos_ref[...], sin_ref[...])
            q_ref[0, r0:r1, :] = (q3 * (ATTN_SCALE * LOG2E)).reshape(r1 - r0, tt).astype(BF16)
        else:
            k3 = _head_norm(t[:nk], kg_ref[...], N_KV_HEADS)
            if rope:
                k3 = _rope(k3, cos_ref[...], sin_ref[...])
            k_ref[0] = k3.reshape(nk, tt).astype(BF16)
            v_ref[0] = t[nk:].astype(BF16)


def _qkv(x, sh, sc, g, w_t, qg, kg, cos_t, sin_t, *, n_q, tile):
    b, l, d = x.shape
    nq, nk = n_q * HEAD_DIM, N_KV_HEADS * HEAD_DIM
    rope = cos_t is not None
    vec = pl.BlockSpec((1, 1, d), lambda i, j: (i, 0, 0))
    in_specs = [pl.BlockSpec((1, tile, d), lambda i, j: (i, j, 0)), vec, vec,
                _const_spec((1, d)), _const_spec(w_t.shape),
                _const_spec((HEAD_DIM, 1)), _const_spec((HEAD_DIM, 1))]
    args = [x, sh, sc, g, w_t, qg, kg]
    if rope:
        in_specs += [pl.BlockSpec((ROPE_HALF, tile), lambda i, j: (0, j))] * 2
        args += [cos_t, sin_t]
    out_shape, out_specs = [], []
    for rows in ([nq] if n_q else []) + [nk, nk]:
        out_shape.append(jax.ShapeDtypeStruct((b, rows, l), BF16))
        out_specs.append(pl.BlockSpec((1, rows, tile), lambda i, j: (i, 0, j)))
    return pl.pallas_call(
        functools.partial(_qkv_kernel, n_q=n_q, rope=rope),
        out_shape=out_shape, grid=(b, l // tile), in_specs=in_specs, out_specs=out_specs,
        compiler_params=_params("arbitrary", "arbitrary"),
        name="qkv_proj" if n_q else "ctx_kv_proj",
    )(*args)


HEADS_PER_TILE = GROUP


def _scores(q_ref, cols, kt_ref, r0, ctx0, biases, kh, heads):
    q4 = jnp.concatenate([q_ref[0, h * HEAD_DIM:(h + 1) * HEAD_DIM, cols] for h in heads], axis=1)
    zeros = jnp.zeros_like(q4)
    qz = jnp.concatenate([q4, zeros] if kh % 2 == 0 else [zeros, q4], axis=0)
    lanes = slice((kh // 2) * LANES, (kh // 2 + 1) * LANES)
    s_loc = jnp.dot(kt_ref[r0:r0 + 3 * BLOCK, lanes], qz, preferred_element_type=F32)
    s_ctx = jnp.dot(kt_ref[ctx0:, lanes], qz, preferred_element_type=F32)
    return [s_loc[:BLOCK] + biases[0], s_loc[BLOCK:2 * BLOCK], s_loc[2 * BLOCK:] + biases[1], s_ctx]


def _softmax_pv(sink_ref, o_ref, cols, vparts, kh, heads, parts, ones_rows):
    rows = slice(kh * HEAD_DIM, (kh + 1) * HEAD_DIM)
    sink = jnp.concatenate([jnp.full((1, BLOCK), sink_ref[h] * LOG2E, F32) for h in heads], axis=1)
    m = sink
    for part in parts:
        m = jnp.maximum(m, jnp.max(part, axis=0, keepdims=True))
    p = jnp.concatenate([jnp.exp2(part - m).astype(BF16) for part in parts], axis=0)
    vaug = jnp.concatenate([vp(rows) for vp in vparts], axis=1)
    vaug = jnp.concatenate([vaug, ones_rows], axis=0)
    o_aug = jnp.dot(vaug, p, preferred_element_type=F32)
    denom = jnp.exp2(sink - m) + o_aug[HEAD_DIM:HEAD_DIM + 1]
    o4 = o_aug[:HEAD_DIM] / denom
    for g, h in enumerate(heads):
        o_ref[0, h * HEAD_DIM:(h + 1) * HEAD_DIM, cols] = o4[:, g * BLOCK:(g + 1) * BLOCK].astype(BF16)


def _attn_kernel(sink_ref, q_ref, kp_ref, km_ref, kn_ref, vp_ref, vm_ref, vn_ref, kx_ref, vx_ref, o_ref, kt_ref,
                 *, qb):
    j = pl.program_id(1)
    nj = pl.num_programs(1)
    wq = HEADS_PER_TILE * BLOCK
    key = lax.broadcasted_iota(jnp.int32, (BLOCK, wq), 0)
    qry = lax.broadcasted_iota(jnp.int32, (BLOCK, wq), 1) % BLOCK
    band_prev = jnp.where(key >= qry, 0.0, NEG)
    band_next = jnp.where(key <= qry, 0.0, NEG)
    blk = lambda i: slice(i * BLOCK, (i + 1) * BLOCK)
    kt_ref[0:BLOCK, :] = kp_ref[0].T
    kt_ref[BLOCK:(qb + 1) * BLOCK, :] = km_ref[0].T
    kt_ref[(qb + 1) * BLOCK:(qb + 2) * BLOCK, :] = kn_ref[0].T
    ctx0 = (qb + 2) * BLOCK
    kt_ref[ctx0:, :] = kx_ref[0].T
    lc = kx_ref.shape[2]
    ones_rows = jnp.where(lax.broadcasted_iota(jnp.int32, (BF16_ROWS, 3 * BLOCK + lc), 0) == 0,
                          1.0, 0.0).astype(BF16)

    def local_v(i):
        refs = [(lambda r: vp_ref[0, r]) if i == 0 else (lambda r, i=i: vm_ref[0, r, blk(i - 1)]),
                lambda r, i=i: vm_ref[0, r, blk(i)],
                (lambda r: vn_ref[0, r]) if i == qb - 1 else (lambda r, i=i: vm_ref[0, r, blk(i + 1)])]
        return refs + [lambda r: vx_ref[0, r]]

    work = []
    for i in range(qb):
        b_prev = jnp.where(j > 0, band_prev, NEG) if i == 0 else band_prev
        b_next = jnp.where(j < nj - 1, band_next, NEG) if i == qb - 1 else band_next
        for kh in range(N_KV_HEADS):
            for t in range(GROUP // HEADS_PER_TILE):
                heads = [kh * GROUP + t * HEADS_PER_TILE + g for g in range(HEADS_PER_TILE)]
                work.append((blk(i), i * BLOCK, local_v(i), (b_prev, b_next), kh, heads))
    score = lambda w: _scores(q_ref, w[0], kt_ref, w[1], ctx0, w[3], w[4], w[5])
    nxt = score(work[0])
    for n, w in enumerate(work):
        parts = nxt
        if n + 1 < len(work):
            nxt = score(work[n + 1])
        _softmax_pv(sink_ref, o_ref, w[0], w[2], w[4], w[5], parts, ones_rows)


def _attention(sink, q_t, k_t, v_t, kx_t, vx_t, *, qb):
    b, dq, l = q_t.shape
    nk = k_t.shape[1]
    lc = kx_t.shape[2]
    nb = l // BLOCK
    prev = pl.BlockSpec((1, nk, BLOCK), lambda i, j: (i, 0, jnp.maximum(qb * j - 1, 0)))
    mid = pl.BlockSpec((1, nk, qb * BLOCK), lambda i, j: (i, 0, j))
    nxt = pl.BlockSpec((1, nk, BLOCK), lambda i, j: (i, 0, jnp.minimum(qb * j + qb, nb - 1)))
    ctx = pl.BlockSpec((1, nk, lc), lambda i, j: (i, 0, 0))
    return pl.pallas_call(
        functools.partial(_attn_kernel, qb=qb),
        out_shape=jax.ShapeDtypeStruct((b, dq, l), BF16),
        grid=(b, nb // qb),
        in_specs=[pl.BlockSpec(memory_space=pltpu.SMEM),
                  pl.BlockSpec((1, dq, qb * BLOCK), lambda i, j: (i, 0, j)),
                  prev, mid, nxt, prev, mid, nxt, ctx, ctx],
        out_specs=pl.BlockSpec((1, dq, qb * BLOCK), lambda i, j: (i, 0, j)),
        scratch_shapes=[pltpu.VMEM(((qb + 2) * BLOCK + lc, nk), BF16)],
        compiler_params=_params("arbitrary", "arbitrary"),
        name="band_attention",
    )(sink, q_t, k_t, k_t, k_t, v_t, v_t, v_t, kx_t, vx_t)


def _proj_ffn_kernel(x_ref, a_ref, g1_ref, sh2_ref, sc2_ref, g2_ref, wo_ref, bo_ref, n2_ref,
                     w1_ref, w3_ref, w2_ref, o_ref, *, ff_chunk):
    proj = lax.dot_general(a_ref[0], wo_ref[...], TN, preferred_element_type=F32)
    x1 = x_ref[0] + g1_ref[0] * (proj + bo_ref[...])
    h = _norm_mod(x1, n2_ref[...], sc2_ref[0], sh2_ref[0]).astype(BF16)

    def up(c0, c1):
        return (jnp.dot(h, w1_ref[:, c0:c1], preferred_element_type=F32),
                jnp.dot(h, w3_ref[:, c0:c1], preferred_element_type=F32))

    ff = w1_ref.shape[1]
    spans = [(c0, min(c0 + ff_chunk, ff)) for c0 in range(0, ff, ff_chunk)]
    acc, nxt = None, up(*spans[0])
    for i, (c0, c1) in enumerate(spans):
        a, b = nxt
        if i + 1 < len(spans):
            nxt = up(*spans[i + 1])
        act = ((a * jax.nn.sigmoid(a)) * b).astype(BF16)
        part = jnp.dot(act, w2_ref[c0:c1, :], preferred_element_type=F32)
        acc = part if acc is None else acc + part
    o_ref[0] = x1 + g2_ref[0] * acc


def _proj_ffn(x, a_t, g1, sh2, sc2, g2, wo, bo, n2, w1, w3, w2, *, tile, ff_chunk):
    b, l, d = x.shape
    c = a_t.shape[1]
    vec = pl.BlockSpec((1, 1, d), lambda i, j: (i, 0, 0))
    return pl.pallas_call(
        functools.partial(_proj_ffn_kernel, ff_chunk=ff_chunk),
        out_shape=jax.ShapeDtypeStruct((b, l, d), F32),
        grid=(b, l // tile),
        in_specs=[pl.BlockSpec((1, tile, d), lambda i, j: (i, j, 0)),
                  pl.BlockSpec((1, c, tile), lambda i, j: (i, 0, j)),
                  vec, vec, vec, vec,
                  _const_spec(wo.shape), _const_spec((1, d)), _const_spec((1, d)),
                  _const_spec(w1.shape), _const_spec(w3.shape), _const_spec(w2.shape)],
        out_specs=pl.BlockSpec((1, tile, d), lambda i, j: (i, j, 0)),
        compiler_params=_params("arbitrary", "arbitrary"),
        name="proj_swiglu",
    )(x, a_t, g1, sh2, sc2, g2, wo, bo, n2, w1, w3, w2)


def _inproj_kernel(x_ref, sh_ref, sc_ref, g_ref, w_ref, b_ref, cw_ref, cb_ref, u_ref, x0_ref, zc_ref, lcol_ref,
                   *, slab):
    j = pl.program_id(1)
    nt = pl.num_programs(1) - 1
    d = u_ref.shape[1]
    tt = zc_ref.shape[1]
    nl = tt // LANES

    @pl.when(j == 0)
    def _():
        zc_ref[...] = jnp.zeros_like(zc_ref)
        lcol_ref[...] = jnp.zeros_like(lcol_ref)

    h = _norm_mod(x_ref[0], g_ref[...], sc_ref[0], sh_ref[0]).astype(BF16)
    sub = min(slab, LANES)
    lane = lax.broadcasted_iota(jnp.int32, (sub, LANES), 1)
    first, last = lane == 0, lane == LANES - 1
    has_next = j < nt
    zero = jnp.zeros((sub, LANES), BF16)

    def proj(c0):
        outs = []
        for p in range(3):
            rows = slice(p * d + c0, p * d + c0 + slab)
            z = lax.dot_general(w_ref[rows, :], h, NT, preferred_element_type=F32)
            bias = b_ref[rows, :]
            outs.append(jnp.concatenate([z[:, k * LANES:(k + 1) * LANES] + bias for k in range(nl)],
                                        axis=1).astype(BF16))
        return outs

    def conv_rows(p, r0, znew_first):
        rows = slice(p * d + r0, p * d + r0 + sub)
        w0, w1, w2 = [cw_ref[k, rows, :] for k in range(3)]
        bias = cb_ref[rows, :]
        z = [zc_ref[rows, k * LANES:(k + 1) * LANES] for k in range(nl)]
        right = jnp.where(has_next, znew_first, zero)
        fwd = [pltpu.roll(t, 1, axis=1) for t in [lcol_ref[rows, :]] + z]
        bwd = [pltpu.roll(t, LANES - 1, axis=1) for t in z + [right]]
        tiles = []
        for k in range(nl):
            prev = jnp.where(first, fwd[k], fwd[k + 1])
            nxt = jnp.where(last, bwd[k + 1], bwd[k])
            tiles.append(w0 * prev.astype(F32) + w1 * z[k].astype(F32) + w2 * nxt.astype(F32) + bias)
        lcol_ref[rows, :] = z[nl - 1]
        return tiles

    def finish(c0, znew):
        for s0 in range(0, slab, sub):
            r0 = c0 + s0
            x1 = conv_rows(1, r0, znew[1][s0:s0 + sub, :LANES])
            v = conv_rows(2, r0, znew[2][s0:s0 + sub, :LANES])
            for k in range(nl):
                u_ref[0, r0:r0 + sub, k * LANES:(k + 1) * LANES] = (v[k] * x1[k]).astype(BF16)
            x0 = conv_rows(0, r0, znew[0][s0:s0 + sub, :LANES])
            for k in range(nl):
                x0_ref[0, r0:r0 + sub, k * LANES:(k + 1) * LANES] = x0[k].astype(BF16)
        for p in range(3):
            zc_ref[p * d + c0:p * d + c0 + slab, :] = znew[p]

    starts = list(range(0, d, slab))
    nxt = proj(starts[0])
    for i, c0 in enumerate(starts):
        znew = nxt
        if i + 1 < len(starts):
            nxt = proj(starts[i + 1])
        finish(c0, znew)


def _inproj(x, sh, sc, g, w_t, b_vec, conv_w, conv_b, *, tile, slab):
    b, l, d = x.shape
    c3 = w_t.shape[0]
    c = c3 // 3
    nt = l // tile
    vec = pl.BlockSpec((1, 1, d), lambda i, j: (i, 0, 0))
    out = pl.BlockSpec((1, c, tile), lambda i, j: (i, 0, jnp.maximum(j - 1, 0)))
    wide = lambda a: jnp.broadcast_to(a[..., None], a.shape + (LANES,))
    return pl.pallas_call(
        functools.partial(_inproj_kernel, slab=slab),
        out_shape=[jax.ShapeDtypeStruct((b, c, l), BF16), jax.ShapeDtypeStruct((b, c, l), BF16)],
        grid=(b, nt + 1),
        in_specs=[pl.BlockSpec((1, tile, d), lambda i, j: (i, jnp.minimum(j, nt - 1), 0)), vec, vec,
                  _const_spec((1, d)), _const_spec(w_t.shape), _const_spec((c3, LANES)),
                  _const_spec((3, c3, LANES)), _const_spec((c3, LANES))],
        out_specs=[out, out],
        scratch_shapes=[pltpu.VMEM((c3, tile), BF16), pltpu.VMEM((c3, LANES), BF16)],
        compiler_params=_params("arbitrary", "arbitrary"),
        name="hyena_inproj_conv",
    )(x, sh, sc, g, w_t, wide(b_vec), wide(conv_w), wide(conv_b))


def _split(x):
    hi = x.astype(BF16)
    return hi, (x - hi.astype(F32)).astype(BF16)


def _dot3(a, b):
    a_hi, a_lo = _split(a)
    b_hi, b_lo = _split(b)
    d = functools.partial(jnp.dot, preferred_element_type=F32)
    return d(a_hi, b_hi) + (d(a_hi, b_lo) + d(a_lo, b_hi))


def _filter_taps_kernel(z_ref, t_ref, w1_ref, b1_ref, f1_ref, w2_ref, b2_ref, f2_ref, wo_ref, dec_ref,
                        k_ref, norm_ref, *, l):
    i = pl.program_id(0)
    h = jnp.sin(f1_ref[...] * (_dot3(w1_ref[...], z_ref[...]) + b1_ref[...]))
    h = jnp.sin(f2_ref[...] * (_dot3(w2_ref[...], h) + b2_ref[...]))
    win = jnp.exp(-t_ref[...] * jnp.abs(dec_ref[...]))
    n = z_ref.shape[1]
    slot = i * n + lax.broadcasted_iota(jnp.int32, (1, n), 1)
    k = jnp.where(slot == l, 0.0, _dot3(wo_ref[...], h) * win)

    @pl.when(i == 0)
    def _():
        norm_ref[...] = jnp.zeros_like(norm_ref)

    part = jnp.abs(k[:, :LANES])
    for j in range(1, n // LANES):
        part = part + jnp.abs(k[:, j * LANES:(j + 1) * LANES])
    norm_ref[...] += part
    k_ref[...] = k


def _filter_taps(zz_t, t_row, w1, b1, f1, w2, b2, f2, wout, dec, *, l, cols):
    e, n2 = zz_t.shape
    hid = w1.shape[1]
    c = dec.shape[0]
    nblk = n2 // cols
    col = lambda a: a.reshape(-1, 1)
    return pl.pallas_call(
        functools.partial(_filter_taps_kernel, l=l),
        out_shape=[jax.ShapeDtypeStruct((c, n2), F32), jax.ShapeDtypeStruct((c, LANES), F32)],
        grid=(nblk,),
        in_specs=[pl.BlockSpec((e, cols), lambda i: (0, i)), pl.BlockSpec((1, cols), lambda i: (0, i)),
                  _const_spec((hid, e)), _const_spec((hid, 1)), _const_spec((hid, 1)),
                  _const_spec((hid, hid)), _const_spec((hid, 1)), _const_spec((hid, 1)),
                  pl.BlockSpec((c, hid), lambda i: ((2 * i) // nblk, 0)), _const_spec((c, 1))],
        out_specs=[pl.BlockSpec((c, cols), lambda i: (0, i)), pl.BlockSpec((c, LANES), lambda i: (0, 0))],
        compiler_params=_params("arbitrary"),
        name="hyena_filter_taps",
    )(zz_t, t_row, w1.T, col(b1), col(f1), w2.T, col(b2), col(f2), wout.T, col(dec))


def _filter_fft_kernel(k_ref, norm_ref, f1_ref, twr_ref, twi_ref, g_ref, o_ref, s_ref, *, cb, n1, pitch, group):
    k = k_ref[...] / jnp.sum(norm_ref[...], axis=1, keepdims=True)
    for j in range(n1):
        s_ref[j * pitch:j * pitch + cb, :] = k[:, j * LANES:(j + 1) * LANES]
    tr, ti = twr_ref[...], twi_ref[...]

    def body(gi, carry):
        st = []
        for g in range(group):
            kc = s_ref[pl.ds(gi * group + g, n1, stride=pitch), :]
            ri = jnp.dot(f1_ref[...], kc.astype(BF16), preferred_element_type=F32)
            re, im = ri[:n1], ri[n1:]
            st.append(jnp.concatenate([re * tr - im * ti, re * ti + im * tr], axis=1))
        spec = jnp.dot(jnp.concatenate(st, axis=0).astype(BF16), g_ref[...], preferred_element_type=F32)
        for g in range(group):
            o_ref[gi * group + g] = spec[g * n1:(g + 1) * n1]
        return carry

    lax.fori_loop(0, cb // group, body, 0)


def _filter_fft(taps, norm, consts, *, cb, n1, group):
    c, n2 = taps.shape
    f1, twr, twi, g = consts
    pitch = cb + SUBLANES
    return pl.pallas_call(
        functools.partial(_filter_fft_kernel, cb=cb, n1=n1, pitch=pitch, group=group),
        out_shape=jax.ShapeDtypeStruct((c, n1, 2 * LANES), F32),
        grid=(c // cb,),
        in_specs=[pl.BlockSpec((cb, n2), lambda i: (i, 0)), pl.BlockSpec((cb, LANES), lambda i: (i, 0)),
                  _const_spec(f1.shape), _const_spec(twr.shape), _const_spec(twi.shape), _const_spec(g.shape)],
        out_specs=pl.BlockSpec((cb, n1, 2 * LANES), lambda i: (i, 0, 0)),
        scratch_shapes=[pltpu.VMEM((n1 * pitch, LANES), F32)],
        compiler_params=_params("arbitrary"),
        name="hyena_filter_fft",
    )(taps, norm, f1, twr, twi, g)


def _longconv_kernel(skip_ref, u_ref, x0_ref, kh_ref, a1_ref, twr_ref, twi_ref, g_ref, gc_ref, p_ref, o_ref,
                     s_ref, so_ref, *, cb, nh, n1, group, pitch):
    for bi in range(2):
        for j in range(nh):
            s_ref[bi, j * pitch:j * pitch + cb, :] = u_ref[bi, :, j * LANES:(j + 1) * LANES].astype(F32)
            s_ref[2 + bi, j * pitch:j * pitch + cb, :] = x0_ref[bi, :, j * LANES:(j + 1) * LANES].astype(F32)

    c_base = pl.program_id(0) * cb
    tr, ti = twr_ref[...], twi_ref[...]

    def body(gi, carry):
        parts = 2
        sub = group // parts
        chans = [[gi * group + h * sub + k for k in range(sub)] for h in range(parts)]
        us = [[jnp.concatenate([s_ref[0, pl.ds(ci, nh, stride=pitch), :],
                                s_ref[1, pl.ds(ci, nh, stride=pitch), :]], axis=0) for ci in cs]
              for cs in chans]
        first = [[jnp.dot(a1_ref[...], u.astype(BF16), preferred_element_type=F32) for u in uh]
                 for uh in us]
        spec = []
        for h in range(parts):
            st1 = [jnp.concatenate([ri[:n1] * tr - ri[n1:] * ti, ri[:n1] * ti + ri[n1:] * tr], axis=1)
                   for ri in first[h]]
            spec.append(jnp.dot(jnp.concatenate(st1, axis=0).astype(BF16), g_ref[...],
                                preferred_element_type=F32))
        back = []
        for h in range(parts):
            prods = []
            for k, ci in enumerate(chans[h]):
                xs = spec[h][k * n1:(k + 1) * n1]
                xr, xi = xs[:, :LANES], xs[:, LANES:]
                kk = kh_ref[ci]
                kr, ki = kk[:, :LANES], kk[:, LANES:]
                prods.append(jnp.concatenate([xr * kr - xi * ki, xr * ki + xi * kr], axis=1))
            back.append(jnp.dot(jnp.concatenate(prods, axis=0).astype(BF16), gc_ref[...],
                                preferred_element_type=F32))
        for h in range(parts):
            for k, ci in enumerate(chans[h]):
                bs = back[h][k * n1:(k + 1) * n1]
                br, bim = bs[:, :LANES], bs[:, LANES:]
                st = jnp.concatenate([br * tr + bim * ti, bim * tr - br * ti], axis=0)
                y = jnp.dot(p_ref[...], st.astype(BF16), preferred_element_type=F32)
                x0 = jnp.concatenate([s_ref[2, pl.ds(ci, nh, stride=pitch), :],
                                      s_ref[3, pl.ds(ci, nh, stride=pitch), :]], axis=0)
                gated = (y + us[h][k] * skip_ref[c_base + ci]) * x0
                so_ref[0, pl.ds(ci, nh, stride=pitch), :] = gated[:nh]
                so_ref[1, pl.ds(ci, nh, stride=pitch), :] = gated[nh:]
        return carry

    lax.fori_loop(0, cb // group, body, 0)
    for bi in range(2):
        for j in range(nh):
            o_ref[bi, :, j * LANES:(j + 1) * LANES] = so_ref[bi, j * pitch:j * pitch + cb, :].astype(BF16)


def _longconv(u_t, x0_t, khat, skip, consts, *, cb, nh, n1, group):
    b, c, l = u_t.shape
    a1, twr, twi, g, gc, p = consts
    pitch = cb + SUBLANES
    blk = pl.BlockSpec((2, cb, l), lambda i, j: (j, i, 0))
    return pl.pallas_call(
        functools.partial(_longconv_kernel, cb=cb, nh=nh, n1=n1, group=group, pitch=pitch),
        out_shape=jax.ShapeDtypeStruct((b, c, l), BF16),
        grid=(c // cb, b // 2),
        in_specs=[pl.BlockSpec(memory_space=pltpu.SMEM), blk, blk,
                  pl.BlockSpec((cb, n1, 2 * LANES), lambda i, j: (i, 0, 0)),
                  _const_spec(a1.shape), _const_spec(twr.shape), _const_spec(twi.shape),
                  _const_spec(g.shape), _const_spec(gc.shape), _const_spec(p.shape)],
        out_specs=blk,
        scratch_shapes=[pltpu.VMEM((4, nh * pitch, LANES), F32), pltpu.VMEM((2, nh * pitch, LANES), F32)],
        compiler_params=_params("arbitrary", "arbitrary"),
        name="hyena_longconv",
    )(skip, u_t, x0_t, khat, a1, twr, twi, g, gc, p)


def _dft_tables(l):
    n = 2 * l
    n1 = n // LANES
    nh = l // LANES
    f1 = np.exp(-2j * np.pi * np.outer(np.arange(n1), np.arange(n1)) / n1)
    f2 = np.exp(-2j * np.pi * np.outer(np.arange(LANES), np.arange(LANES)) / LANES)
    tw = np.exp(-2j * np.pi * np.outer(np.arange(n1), np.arange(LANES)) / n)
    fh = f1[:, :nh]
    a1 = np.block([[fh.real, -fh.imag], [fh.imag, fh.real]])
    g = np.block([[f2.real, f2.imag], [-f2.imag, f2.real]])
    gc = np.block([[f2.real, -f2.imag], [f2.imag, f2.real]])
    ci = np.conj(f1)[:nh, :]
    p = np.block([[ci.real, -ci.imag], [ci.imag, ci.real]]) / n
    f1full = np.concatenate([f1.real, f1.imag], axis=0)

    def bf(a):
        return jnp.asarray(a, F32).astype(BF16)

    data = (bf(a1), jnp.asarray(tw.real, F32), jnp.asarray(tw.imag, F32), bf(g), bf(gc), bf(p))
    filt = (bf(f1full), jnp.asarray(tw.real, F32), jnp.asarray(tw.imag, F32), bf(g))
    return data, filt, n1, nh


def _rope_tables_t(l):
    rows = l // GRID_W
    row = np.repeat(np.arange(rows, dtype=np.float64), GRID_W)
    col = np.tile(np.arange(GRID_W, dtype=np.float64), rows)
    inv_freq = ROPE_THETA ** (-np.arange(AXIS_FREQS, dtype=np.float64) / AXIS_FREQS)
    ang = np.concatenate([inv_freq[:, None] * row[None, :], inv_freq[:, None] * col[None, :]], axis=0)
    return jnp.asarray(np.cos(ang), F32), jnp.asarray(np.sin(ang), F32)


def _filter_positions(l):
    t = np.linspace(0.0, 1.0, l)
    w = 2.0 * math.pi * np.arange(l, dtype=np.float64) / l
    bands = np.linspace(1e-4, HY_BANDS - 1, HY_BANDS)[None, :]
    z = np.concatenate([t[:, None], np.cos(bands * w[:, None]), -np.sin(bands * w[:, None])], axis=-1)
    idx = np.concatenate([np.arange(l), np.zeros((1,), np.int64), np.arange(l - 1, 0, -1)])
    return jnp.asarray(z[idx].T, F32), jnp.asarray(t[idx][None, :], F32)


def kernel(x, c, ctx, c_ctx, ada_w, ada_b, norm1_g, norm2_g, attn_wqkv, attn_wo, attn_q_gain, attn_k_gain, attn_sink, hy_w_in, hy_b_in, hy_conv_w, hy_conv_b, hy_f_w1, hy_f_b1, hy_f_freq1, hy_f_w2, hy_f_b2, hy_f_freq2, hy_f_wout, hy_decay, hy_skip, hy_w_out, hy_b_out, ffn_w1, ffn_w3, ffn_w2):
    b, l, d = x.shape
    assert b % 2 == 0, "batch pairs share one complex long-conv transform"
    tile = min(512, l)
    ff = ffn_w1.shape[-1]
    ff_chunk = 1 * MXU_TILE

    pad = (-(b + 1)) % SUBLANES
    cond = jnp.concatenate([c, c_ctx[None, :], jnp.zeros((pad, d), F32)], axis=0)
    mod = _ada(cond, ada_w, ada_b)

    def chunks(i, rows):
        m = mod[i, rows][:, None, :]
        return [m[..., k * d:(k + 1) * d] for k in range(6)]

    row = lambda a: a.reshape(1, -1)
    col = lambda a: a.reshape(-1, 1)

    sh1, sc1, g1, sh2, sc2, g2 = chunks(0, slice(0, b))
    csh1, csc1 = [jnp.broadcast_to(m, (b, 1, d)) for m in chunks(0, slice(b, b + 1))[:2]]
    wqkv_t = attn_wqkv[0].T.astype(BF16)
    cos_t, sin_t = _rope_tables_t(l)
    qg, kg = col(attn_q_gain[0]), col(attn_k_gain[0])
    q_t, k_t, v_t = _qkv(x, sh1, sc1, row(norm1_g[0]), wqkv_t, qg, kg, cos_t, sin_t, n_q=N_HEADS, tile=min(2 * tile, l))
    kx_t, vx_t = _qkv(ctx, csh1, csc1, row(norm1_g[0]), wqkv_t[N_HEADS * HEAD_DIM:], qg, kg, None, None,
                      n_q=0, tile=ctx.shape[1])
    o_t = _attention(attn_sink[0], q_t, k_t, v_t, kx_t, vx_t, qb=4)
    x = _proj_ffn(x, o_t, g1, sh2, sc2, g2, attn_wo[0].astype(BF16), jnp.zeros((1, d), F32), row(norm2_g[0]),
                  ffn_w1[0].astype(BF16), ffn_w3[0].astype(BF16), ffn_w2[0].astype(BF16),
                  tile=min(2 * tile, l), ff_chunk=ff_chunk)

    sh1, sc1, g1, sh2, sc2, g2 = chunks(1, slice(0, b))
    data_consts, filt_consts, n1, nh = _dft_tables(l)
    zz_t, t_row = _filter_positions(l)
    taps, norm = _filter_taps(zz_t, t_row, hy_f_w1[0], hy_f_b1[0], hy_f_freq1[0], hy_f_w2[0], hy_f_b2[0],
                              hy_f_freq2[0], hy_f_wout[0], hy_decay[0], l=l, cols=min(1024, l))
    khat = _filter_fft(taps, norm, filt_consts, cb=64, n1=n1, group=4)
    u_t, x0_t = _inproj(x, sh1, sc1, row(norm1_g[1]), hy_w_in[0].T.astype(BF16), hy_b_in[0],
                        hy_conv_w[0], hy_conv_b[0], tile=tile, slab=min(1024, d))
    gated = _longconv(u_t, x0_t, khat, hy_skip[0], data_consts, cb=64, nh=nh, n1=n1, group=16)
    x = _proj_ffn(x, gated, g1, sh2, sc2, g2, hy_w_out[0].astype(BF16), row(hy_b_out[0]),
                  row(norm2_g[1]), ffn_w1[1].astype(BF16), ffn_w3[1].astype(BF16), ffn_w2[1].astype(BF16),
                  tile=min(2 * tile, l), ff_chunk=ff_chunk)
    return x
```
